```python
import math
import jax, jax.numpy as jnp
from jax import lax
import numpy as np

D_MODEL = 2048
BATCH = 8
SEQ = 2048
DEPTH = 1
DEC_BATCH = 2
DEC_SEQ = 16384
PAST_LEN = 128

HEAD_DIM = 128
N_HEADS_NA = 4
N_HEADS_PER_DIL = 4
DIL_CONFIGS = ((128, 1), (512, 4), (2048, 16))
N_HEADS_DIL = N_HEADS_PER_DIL * len(DIL_CONFIGS)
W_NA = N_HEADS_NA * HEAD_DIM
W_DIL = N_HEADS_DIL * HEAD_DIM
W_DIL_OUT = N_HEADS_PER_DIL * HEAD_DIM
GRID_W = 64
NA_ROWS = 8
NA_COLS = 16
NA_Q_COLS = 16
NA_KEY_COLS = 32
ROT_DIM = HEAD_DIM // 4
ROPE_THETA = 500000.0
BAND_BLOCK = 128
N_GROUPS = 4
EXPERTS_PER_GROUP = 8
N_EXPERTS = N_GROUPS * EXPERTS_PER_GROUP
TOP_K = 2
D_EXPERT = 1024
MOE_BLOCK = 128
EPS = 1e-6
NEG = -1e30
IN_SIZES = [W_NA, W_NA, W_NA, W_DIL, W_DIL, W_DIL, D_MODEL, D_MODEL]
IN_WIDTH = sum(IN_SIZES)
IN_OFFSETS = [int(v) for v in np.cumsum(IN_SIZES)[:-1]]

kernel_name = "hybrid_natten_dilated_hmoe_encoder"


def rms_norm(x, g):
    xf = x.astype(jnp.float32)
    y = xf * lax.rsqrt(jnp.mean(xf * xf, axis=-1, keepdims=True) + EPS)
    return (y * g.astype(jnp.float32)).astype(x.dtype)


def partial_rope(x, pos):
    half = ROT_DIM // 2
    inv = 1.0 / (ROPE_THETA ** (jnp.arange(half, dtype=jnp.float32) * (2.0 / ROT_DIM)))
    ang = pos[:, None] * inv[None, :]
    cos = jnp.cos(ang)[None, :, None, :]
    sin = jnp.sin(ang)[None, :, None, :]
    xf = x.astype(jnp.float32)
    x1 = xf[..., :half]
    x2 = xf[..., half:ROT_DIM]
    out = jnp.concatenate([x1 * cos - x2 * sin, x2 * cos + x1 * sin, xf[..., ROT_DIM:]], axis=-1)
    return out.astype(x.dtype)


def neighbourhood_attention(q, k, v, rpb):
    B, T, H, hd = q.shape
    rows = T // GRID_W
    win_r = min(NA_ROWS, rows)
    r = np.arange(rows)
    rs = np.clip(r - win_r // 2, 0, rows - win_r)
    row_idx = rs[:, None] + np.arange(win_r)[None, :]
    dr_idx = row_idx - r[:, None] + (NA_ROWS - 1)
    grid = lambda t: t.reshape(B, rows, GRID_W, H, hd).transpose(0, 3, 1, 2, 4)
    qg, kg, vg = grid(q), grid(k), grid(v)
    scale = HEAD_DIM ** -0.5
    outs = []
    for qc0 in range(0, GRID_W, NA_Q_COLS):
        kc0 = min(max(qc0 - NA_COLS // 2, 0), GRID_W - NA_KEY_COLS)
        c = np.arange(qc0, qc0 + NA_Q_COLS)
        cs = np.clip(c - NA_COLS // 2, 0, GRID_W - NA_COLS)
        cc = np.arange(kc0, kc0 + NA_KEY_COLS)
        valid = (cc[None, :] >= cs[:, None]) & (cc[None, :] < cs[:, None] + NA_COLS)
        dc_idx = np.clip(cc[None, :] - c[:, None] + NA_COLS - 1, 0, 2 * NA_COLS - 2)
        kb = kg[:, :, row_idx, kc0:kc0 + NA_KEY_COLS]
        vb = vg[:, :, row_idx, kc0:kc0 + NA_KEY_COLS]
        qb = qg[:, :, :, qc0:qc0 + NA_Q_COLS]
        s = jnp.einsum('bhrqd,bhrwkd->bhrqwk', qb, kb).astype(jnp.float32) * scale
        bias = rpb[:, dr_idx[:, None, :, None], dc_idx[None, :, None, :]]
        s = s + bias[None].astype(jnp.float32)
        s = jnp.where(valid[:, None, :], s, NEG)
        p = jax.nn.softmax(s, axis=(-2, -1))
        outs.append(jnp.einsum('bhrqwk,bhrwkd->bhrqd', p.astype(v.dtype), vb))
    o = jnp.concatenate(outs, axis=3)
    return o.transpose(0, 2, 3, 1, 4).reshape(B, T, H * hd)


def band_attend(q, k, v, half):
    lead = q.shape[:-2]
    L, hd = q.shape[-2], q.shape[-1]
    blk = min(BAND_BLOCK, L)
    nb = -(-L // blk)
    Lp = nb * blk
    kb_len = blk + 2 * half
    pad_lead = [(0, 0)] * len(lead)
    qp = jnp.pad(q, pad_lead + [(0, Lp - L), (0, 0)])
    kp = jnp.pad(k, pad_lead + [(half, Lp - L + half), (0, 0)])
    vp = jnp.pad(v, pad_lead + [(half, Lp - L + half), (0, 0)])
    idx = np.arange(nb)[:, None] * blk + np.arange(kb_len)[None, :]
    kb = kp[..., idx, :]
    vb = vp[..., idx, :]
    qb = qp.reshape(lead + (nb, blk, hd))
    s = jnp.einsum('...nqd,...nkd->...nqk', qb, kb).astype(jnp.float32) * (HEAD_DIM ** -0.5)
    base = np.arange(nb)[:, None, None] * blk
    qpos = base + np.arange(blk)[None, :, None]
    kpos = base + np.arange(kb_len)[None, None, :] - half
    valid = (np.abs(kpos - qpos) <= half) & (kpos >= 0) & (kpos < L)
    s = jnp.where(valid, s, NEG)
    lse = jax.nn.logsumexp(s, axis=-1)
    p = jnp.exp(s - lse[..., None])
    o = jnp.einsum('...nqk,...nkd->...nqd', p.astype(v.dtype), vb)
    o = o.reshape(lead + (Lp, hd))[..., :L, :]
    lse = lse.reshape(lead + (Lp,))[..., :L]
    return o, lse


def dilated_attention(q, k, v):
    B, T, _, hd = q.shape
    Hg = N_HEADS_PER_DIL
    outs, lses = [], []
    for g, (window, dil) in enumerate(DIL_CONFIGS):
        half = window // 2 // dil
        L = T // dil
        regroup = lambda t: t[:, :, g * Hg:(g + 1) * Hg].reshape(B, L, dil, Hg, hd).transpose(0, 2, 3, 1, 4)
        o, lse = band_attend(regroup(q), regroup(k), regroup(v), half)
        outs.append(o.transpose(0, 3, 1, 2, 4).reshape(B, T, Hg, hd))
        lses.append(lse.transpose(0, 3, 1, 2).reshape(B, T, Hg))
    alpha = jax.nn.softmax(jnp.stack(lses, axis=0), axis=0)
    o = jnp.sum(alpha[..., None].astype(q.dtype) * jnp.stack(outs, axis=0), axis=0)
    return o.reshape(B, T, Hg * hd)


def hier_moe(h, wrg, brg, wre, bre, w_gate, w_up, w_down):
    N, D = h.shape
    glog = (h @ wrg).astype(jnp.float32) + brg.astype(jnp.float32)
    gprob = jax.nn.softmax(glog, axis=-1)
    gsel = jnp.argmax(glog, axis=-1)
    elog = (h @ wre).astype(jnp.float32).reshape(N, N_GROUPS, EXPERTS_PER_GROUP) \
        + bre.astype(jnp.float32).reshape(N_GROUPS, EXPERTS_PER_GROUP)
    elog_sel = jnp.take_along_axis(elog, gsel[:, None, None], axis=1)[:, 0]
    eprob = jax.nn.softmax(elog_sel, axis=-1)
    topv, topi = lax.top_k(eprob, TOP_K)
    gp = jnp.take_along_axis(gprob, gsel[:, None], axis=1)
    gate = gp * topv / jnp.sum(topv, axis=-1, keepdims=True)
    eid = (gsel[:, None] * EXPERTS_PER_GROUP + topi).reshape(-1)
    tok = jnp.arange(N * TOP_K) // TOP_K
    order = jnp.argsort(eid)
    se, st, sg = eid[order], tok[order], gate.reshape(-1)[order]
    counts = jnp.bincount(eid, length=N_EXPERTS)
    starts = jnp.cumsum(counts) - counts
    pcounts = (counts + MOE_BLOCK - 1) // MOE_BLOCK * MOE_BLOCK
    pends = jnp.cumsum(pcounts)
    pstarts = pends - pcounts
    pos = pstarts[se] + jnp.arange(N * TOP_K) - starts[se]
    nblk = -(-(N * TOP_K) // MOE_BLOCK) + N_EXPERTS
    P = nblk * MOE_BLOCK
    xbuf = jnp.zeros((P, D), h.dtype).at[pos].set(h[st])
    bexp = jnp.minimum(jnp.searchsorted(pends, jnp.arange(nblk) * MOE_BLOCK, side='right'), N_EXPERTS - 1)

    def expert_block(args):
        xb, e = args
        a = xb @ w_gate[e]
        u = xb @ w_up[e]
        return (jax.nn.silu(a) * u) @ w_down[e]

    ybuf = lax.map(expert_block, (xbuf.reshape(nblk, MOE_BLOCK, D), bexp)).reshape(P, D)
    return jnp.zeros((N, D), h.dtype).at[st].add(ybuf[pos] * sg[:, None].astype(h.dtype))


def encoder_layer(x, norm1_g, w_in, qn_a, kn_a, rpb_a, qn_b, kn_b, w_branch_a, w_branch_b, w_out,
                  norm2_g, router_group_w, router_group_b, router_expert_w, router_expert_b,
                  w_gate, w_up, w_down):
    B, T, D = x.shape
    h = rms_norm(x, norm1_g)
    proj = h @ w_in
    qa, ka, va, qb, kb, vb, ga, gb = jnp.split(proj, IN_OFFSETS, axis=-1)
    heads = lambda t, n: t.reshape(B, T, n, HEAD_DIM)
    qa = rms_norm(heads(qa, N_HEADS_NA), qn_a)
    ka = rms_norm(heads(ka, N_HEADS_NA), kn_a)
    ya = neighbourhood_attention(qa, ka, heads(va, N_HEADS_NA), rpb_a) @ w_branch_a
    pos = jnp.arange(T, dtype=jnp.float32)
    qb = partial_rope(rms_norm(heads(qb, N_HEADS_DIL), qn_b), pos)
    kb = partial_rope(rms_norm(heads(kb, N_HEADS_DIL), kn_b), pos)
    yb = dilated_attention(qb, kb, heads(vb, N_HEADS_DIL)) @ w_branch_b
    mix = jax.nn.sigmoid(ga) * ya + jax.nn.sigmoid(gb) * yb
    x = x + mix @ w_out
    h2 = rms_norm(x, norm2_g).reshape(B * T, D)
    y = hier_moe(h2, router_group_w, router_group_b, router_expert_w, router_expert_b, w_gate, w_up, w_down)
    return x + y.reshape(B, T, D)


def setup_inputs(seed: int = 0) -> dict:
    key = jax.random.key(seed)
    ks = jax.random.split(key, 24)
    nrm = lambda k, shape, scale: jax.random.normal(k, shape, jnp.float32) * scale
    return {
        "x_prompt": nrm(ks[0], (BATCH, SEQ, D_MODEL), 1.0),
        "x_sample": nrm(ks[1], (DEC_BATCH, DEC_SEQ, D_MODEL), 1.0),
        "norm1_g": 1.0 + nrm(ks[2], (DEPTH, D_MODEL), 0.02),
        "w_in": nrm(ks[3], (DEPTH, D_MODEL, IN_WIDTH), D_MODEL ** -0.5),
        "qn_a": 1.0 + nrm(ks[4], (DEPTH, HEAD_DIM), 0.02),
        "kn_a": 1.0 + nrm(ks[5], (DEPTH, HEAD_DIM), 0.02),
        "rpb_a": nrm(ks[6], (DEPTH, N_HEADS_NA, 2 * NA_ROWS - 1, 2 * NA_COLS - 1), 0.1),
        "qn_b": 1.0 + nrm(ks[7], (DEPTH, HEAD_DIM), 0.02),
        "kn_b": 1.0 + nrm(ks[8], (DEPTH, HEAD_DIM), 0.02),
        "w_branch_a": nrm(ks[9], (DEPTH, W_NA, D_MODEL), W_NA ** -0.5),
        "w_branch_b": nrm(ks[10], (DEPTH, W_DIL_OUT, D_MODEL), W_DIL_OUT ** -0.5),
        "w_out": nrm(ks[11], (DEPTH, D_MODEL, D_MODEL), D_MODEL ** -0.5),
        "norm2_g": 1.0 + nrm(ks[12], (DEPTH, D_MODEL), 0.02),
        "router_group_w": nrm(ks[13], (DEPTH, D_MODEL, N_GROUPS), D_MODEL ** -0.5),
        "router_group_b": nrm(ks[14], (DEPTH, N_GROUPS), 0.01),
        "router_expert_w": nrm(ks[15], (DEPTH, D_MODEL, N_EXPERTS), D_MODEL ** -0.5),
        "router_expert_b": nrm(ks[16], (DEPTH, N_EXPERTS), 0.01),
        "w_gate": nrm(ks[17], (DEPTH, N_EXPERTS, D_MODEL, D_EXPERT), D_MODEL ** -0.5),
        "w_up": nrm(ks[18], (DEPTH, N_EXPERTS, D_MODEL, D_EXPERT), D_MODEL ** -0.5),
        "w_down": nrm(ks[19], (DEPTH, N_EXPERTS, D_EXPERT, D_MODEL), D_EXPERT ** -0.5),
    }


def reference(x_prompt, x_sample, norm1_g, w_in, qn_a, kn_a, rpb_a, qn_b, kn_b, w_branch_a,
              w_branch_b, w_out, norm2_g, router_group_w, router_group_b, router_expert_w,
              router_expert_b, w_gate, w_up, w_down):
    y_prompt = x_prompt
    y_sample = x_sample
    for l in range(DEPTH):
        args = (norm1_g[l], w_in[l], qn_a[l], kn_a[l], rpb_a[l], qn_b[l], kn_b[l], w_branch_a[l],
                w_branch_b[l], w_out[l], norm2_g[l], router_group_w[l], router_group_b[l],
                router_expert_w[l], router_expert_b[l], w_gate[l], w_up[l], w_down[l])
        y_prompt = encoder_layer(y_prompt, *args)
        y_sample = encoder_layer(y_sample, *args)
    return (y_prompt, y_sample)
```

```python
import functools

import jax
import jax.numpy as jnp
import numpy as np
from jax import lax
from jax.experimental import pallas as pl
from jax.experimental.pallas import tpu as pltpu

F32 = jnp.float32
BF16 = jnp.bfloat16

D_MODEL = 2048
HEAD_DIM = 128
N_HEADS_NA = 4
N_HEADS_PER_DIL = 4
DIL_CONFIGS = ((128, 1), (512, 4), (2048, 16))
W_NA = N_HEADS_NA * HEAD_DIM
W_DIL = N_HEADS_PER_DIL * len(DIL_CONFIGS) * HEAD_DIM
W_GRP = N_HEADS_PER_DIL * HEAD_DIM
GRID_W = 64
NA_ROWS = 8
NA_COLS = 16
ROT_DIM = HEAD_DIM // 4
ROPE_THETA = 500000.0
BAND_BLOCK = 128
N_GROUPS = 4
EXPERTS_PER_GROUP = 8
N_EXPERTS = N_GROUPS * EXPERTS_PER_GROUP
D_EXPERT = 1024
EPS = 1e-6
NEG = -1e30
IN_WIDTH = 3 * W_NA + 3 * W_DIL + 2 * D_MODEL
SCALE = HEAD_DIM ** -0.5

COL_BLK = 512
CB_QA, CB_KA, CB_VA = 0, 1, 2
CB_QB, CB_KB, CB_VB = 3, 6, 9
CB_GATE = 12
N_COL_BLK = IN_WIDTH // COL_BLK

LANES = 128
VMEM_LIMIT = 56 * 1024 * 1024


def _cparams(sem, vmem=VMEM_LIMIT):
    return pltpu.CompilerParams(dimension_semantics=sem, vmem_limit_bytes=vmem)


def _inproj_kernel(x_ref, g1_ref, w_ref, gains_ref, cos_ref, sina_ref, sinb_ref, o_ref, h_ref):
    j = pl.program_id(1)

    @pl.when(j == 0)
    def _():
        x = x_ref[...]
        ms = jnp.mean(x * x, axis=-1, keepdims=True)
        h_ref[...] = (x * lax.rsqrt(ms + EPS) * g1_ref[...]).astype(BF16)

    acc = jnp.dot(h_ref[...], w_ref[...], preferred_element_type=F32)

    def head_norm(hh, gain):
        xh = acc[:, hh * HEAD_DIM:(hh + 1) * HEAD_DIM]
        ms = jnp.mean(xh * xh, axis=-1, keepdims=True)
        return xh * lax.rsqrt(ms + EPS) * gain

    @pl.when(j < CB_VA)
    def _():
        gain = gains_ref[pl.ds(j, 1), :]
        for hh in range(COL_BLK // HEAD_DIM):
            o_ref[:, hh * HEAD_DIM:(hh + 1) * HEAD_DIM] = head_norm(hh, gain).astype(BF16)

    @pl.when((j >= CB_QB) & (j < CB_VB))
    def _():
        gain = gains_ref[pl.ds(jnp.where(j < CB_KB, 2, 3), 1), :]
        cos, sina, sinb = cos_ref[...], sina_ref[...], sinb_ref[...]
        for hh in range(COL_BLK // HEAD_DIM):
            y = head_norm(hh, gain)
            y = y * cos + pltpu.roll(y, HEAD_DIM - ROT_DIM // 2, 1) * sina + pltpu.roll(y, ROT_DIM // 2, 1) * sinb
            o_ref[:, hh * HEAD_DIM:(hh + 1) * HEAD_DIM] = y.astype(BF16)

    @pl.when((j == CB_VA) | ((j >= CB_VB) & (j < CB_GATE)))
    def _():
        o_ref[...] = acc.astype(BF16)

    @pl.when(j >= CB_GATE)
    def _():
        o_ref[...] = jax.nn.sigmoid(acc).astype(BF16)


def _inproj(x2, g1, w_in_bf, gains, cos_t, sina_t, sinb_t, T, tm):
    n = x2.shape[0]
    tpos = T // tm
    return pl.pallas_call(
        _inproj_kernel,
        grid=(n // tm, N_COL_BLK),
        in_specs=[
            pl.BlockSpec((tm, D_MODEL), lambda i, j: (i, 0)),
            pl.BlockSpec((1, D_MODEL), lambda i, j: (0, 0)),
            pl.BlockSpec((D_MODEL, COL_BLK), lambda i, j: (0, j)),
            pl.BlockSpec((8, HEAD_DIM), lambda i, j: (0, 0)),
            pl.BlockSpec((tm, HEAD_DIM), lambda i, j: (i % tpos, 0)),
            pl.BlockSpec((tm, HEAD_DIM), lambda i, j: (i % tpos, 0)),
            pl.BlockSpec((tm, HEAD_DIM), lambda i, j: (i % tpos, 0)),
        ],
        out_specs=pl.BlockSpec((tm, COL_BLK), lambda i, j: (i, j)),
        out_shape=jax.ShapeDtypeStruct((n, IN_WIDTH), BF16),
        scratch_shapes=[pltpu.VMEM((tm, D_MODEL), BF16)],
        compiler_params=_cparams(("arbitrary", "arbitrary")),
        name="inproj",
    )(x2, g1, w_in_bf, gains, cos_t, sina_t, sinb_t)


NA_CHUNK_ROWS = 16
NA_HALO_ROWS = 4
NA_TQ = NA_CHUNK_ROWS * GRID_W
NA_TH = NA_HALO_ROWS * GRID_W
NA_WIN = NA_ROWS * GRID_W


def _natten_kernel(q_ref, kp_ref, km_ref, kn_ref, vp_ref, vm_ref, vn_ref, bias_ref, o_ref, kw_ref, vw_ref, *, rows):
    i = pl.program_id(1)
    kw_ref[0:NA_TH] = kp_ref[...]
    kw_ref[NA_TH:NA_TH + NA_TQ] = km_ref[...]
    kw_ref[NA_TH + NA_TQ:] = kn_ref[...]
    vw_ref[0:NA_TH] = vp_ref[...]
    vw_ref[NA_TH:NA_TH + NA_TQ] = vm_ref[...]
    vw_ref[NA_TH + NA_TQ:] = vn_ref[...]

    def body(rr, carry):
        r = i * NA_CHUNK_ROWS + rr
        rs = jnp.clip(r - NA_ROWS // 2, 0, rows - NA_ROWS)
        var = r - rs
        off = pl.multiple_of((rs - (i * NA_CHUNK_ROWS - NA_HALO_ROWS)) * GRID_W, GRID_W)
        qoff = pl.multiple_of(rr * GRID_W, GRID_W)
        for h in range(N_HEADS_NA):
            cols = slice(h * HEAD_DIM, (h + 1) * HEAD_DIM)
            q = q_ref[pl.ds(qoff, GRID_W), cols]
            k = kw_ref[pl.ds(off, NA_WIN), cols]
            v = vw_ref[pl.ds(off, NA_WIN), cols]
            s = lax.dot_general(q, k, (((1,), (1,)), ((), ())), preferred_element_type=F32)
            s = s * SCALE + bias_ref[var, h]
            m = jnp.max(s, axis=-1, keepdims=True)
            p = jnp.exp(s - m)
            l = jnp.sum(p, axis=-1, keepdims=True)
            o = jnp.dot(p.astype(BF16), v, preferred_element_type=F32) / l
            o_ref[pl.ds(qoff, GRID_W), cols] = o.astype(BF16)
        return carry

    lax.fori_loop(0, NA_CHUNK_ROWS, body, 0)


def _natten(proj3, bias_tab, B, T):
    rows = T // GRID_W
    nh = T // NA_TH
    ratio = NA_TQ // NA_TH
    blk = lambda rws, fn: pl.BlockSpec((None, rws, COL_BLK), fn)
    return pl.pallas_call(
        functools.partial(_natten_kernel, rows=rows),
        grid=(B, T // NA_TQ),
        in_specs=[
            blk(NA_TQ, lambda b, i: (b, i, CB_QA)),
            blk(NA_TH, lambda b, i: (b, jnp.maximum(i * ratio - 1, 0), CB_KA)),
            blk(NA_TQ, lambda b, i: (b, i, CB_KA)),
            blk(NA_TH, lambda b, i: (b, jnp.minimum((i + 1) * ratio, nh - 1), CB_KA)),
            blk(NA_TH, lambda b, i: (b, jnp.maximum(i * ratio - 1, 0), CB_VA)),
            blk(NA_TQ, lambda b, i: (b, i, CB_VA)),
            blk(NA_TH, lambda b, i: (b, jnp.minimum((i + 1) * ratio, nh - 1), CB_VA)),
            pl.BlockSpec((NA_ROWS, N_HEADS_NA, GRID_W, NA_WIN), lambda b, i: (0, 0, 0, 0)),
        ],
        out_specs=pl.BlockSpec((None, NA_TQ, W_NA), lambda b, i: (b, i, 0)),
        out_shape=jax.ShapeDtypeStruct((B, T, W_NA), BF16),
        scratch_shapes=[pltpu.VMEM((NA_TQ + 2 * NA_TH, COL_BLK), BF16)] * 2,
        compiler_params=_cparams(("arbitrary", "arbitrary")),
        name="natten",
    )(proj3, proj3, proj3, proj3, proj3, proj3, proj3, bias_tab)


def _na_bias_table(rpb):
    var = np.arange(NA_ROWS)
    kr = np.arange(NA_ROWS)
    dr_idx = kr[None, :] - var[:, None] + (NA_ROWS - 1)
    c = np.arange(GRID_W)
    cs = np.clip(c - NA_COLS // 2, 0, GRID_W - NA_COLS)
    cc = np.arange(GRID_W)
    valid = (cc[None, :] >= cs[:, None]) & (cc[None, :] < cs[:, None] + NA_COLS)
    dc_idx = np.clip(cc[None, :] - c[:, None] + NA_COLS - 1, 0, 2 * NA_COLS - 2)
    bias = rpb[:, dr_idx[:, None, :, None], dc_idx[None, :, None, :]]
    bias = jnp.where(valid[None, None, :, None, :], bias.astype(F32), NEG)
    return bias.transpose(1, 0, 2, 3, 4).reshape(NA_ROWS, N_HEADS_NA, GRID_W, NA_WIN)


DIL_HALF = 64
DIL_KB = BAND_BLOCK + 2 * DIL_HALF


def _dilattn_kernel(q_ref, kp_ref, km_ref, kn_ref, vp_ref, vm_ref, vn_ref, o_ref, lse_ref, kw_ref, vw_ref, *, L, tq):
    i = pl.program_id(2)
    kw_ref[0:DIL_HALF] = kp_ref[...]
    kw_ref[DIL_HALF:DIL_HALF + tq] = km_ref[...]
    kw_ref[DIL_HALF + tq:] = kn_ref[...]
    vw_ref[0:DIL_HALF] = vp_ref[...]
    vw_ref[DIL_HALF:DIL_HALF + tq] = vm_ref[...]
    vw_ref[DIL_HALF + tq:] = vn_ref[...]

    r_io = lax.broadcasted_iota(jnp.int32, (BAND_BLOCK, DIL_KB), 0)
    c_io = lax.broadcasted_iota(jnp.int32, (BAND_BLOCK, DIL_KB), 1)
    lane = lax.broadcasted_iota(jnp.int32, (BAND_BLOCK, LANES), 1)
    for s in range(tq // BAND_BLOCK):
        base = i * tq + s * BAND_BLOCK - DIL_HALF
        lo = jnp.maximum(r_io, -base)
        hi = jnp.minimum(r_io + 2 * DIL_HALF, L - 1 - base)
        valid = (c_io >= lo) & (c_io <= hi)
        lse_tile = jnp.zeros((BAND_BLOCK, LANES), F32)
        for h in range(N_HEADS_PER_DIL):
            cols = slice(h * HEAD_DIM, (h + 1) * HEAD_DIM)
            q = q_ref[s * BAND_BLOCK:(s + 1) * BAND_BLOCK, cols]
            k = kw_ref[s * BAND_BLOCK:s * BAND_BLOCK + DIL_KB, cols]
            v = vw_ref[s * BAND_BLOCK:s * BAND_BLOCK + DIL_KB, cols]
            sc = lax.dot_general(q, k, (((1,), (1,)), ((), ())), preferred_element_type=F32) * SCALE
            sc = jnp.where(valid, sc, NEG)
            m = jnp.max(sc, axis=-1, keepdims=True)
            p = jnp.exp(sc - m)
            l = jnp.sum(p, axis=-1, keepdims=True)
            o = jnp.dot(p.astype(BF16), v, preferred_element_type=F32) / l
            o_ref[s * BAND_BLOCK:(s + 1) * BAND_BLOCK, cols] = o.astype(BF16)
            lse_tile = jnp.where(lane == h, m + jnp.log(l), lse_tile)
        lse_ref[s * BAND_BLOCK:(s + 1) * BAND_BLOCK, :] = lse_tile


def _dilattn(proj, B, T, g):
    dil = DIL_CONFIGS[g][1]
    assert DIL_CONFIGS[g][0] // 2 // dil == DIL_HALF
    L = T // dil
    tq = min(512, L)
    nh = L // DIL_HALF
    ratio = tq // DIL_HALF
    pv = proj.reshape(B, L, dil * IN_WIDTH)
    cq, ck, cv = CB_QB + g, CB_KB + g, CB_VB + g
    blk = lambda rws, fn: pl.BlockSpec((None, rws, COL_BLK), fn)
    prev = lambda i: jnp.maximum(i * ratio - 1, 0)
    nxt = lambda i: jnp.minimum((i + 1) * ratio, nh - 1)
    o, lse = pl.pallas_call(
        functools.partial(_dilattn_kernel, L=L, tq=tq),
        grid=(B, dil, L // tq),
        in_specs=[
            blk(tq, lambda b, j, i: (b, i, j * N_COL_BLK + cq)),
            blk(DIL_HALF, lambda b, j, i: (b, prev(i), j * N_COL_BLK + ck)),
            blk(tq, lambda b, j, i: (b, i, j * N_COL_BLK + ck)),
            blk(DIL_HALF, lambda b, j, i: (b, nxt(i), j * N_COL_BLK + ck)),
            blk(DIL_HALF, lambda b, j, i: (b, prev(i), j * N_COL_BLK + cv)),
            blk(tq, lambda b, j, i: (b, i, j * N_COL_BLK + cv)),
            blk(DIL_HALF, lambda b, j, i: (b, nxt(i), j * N_COL_BLK + cv)),
        ],
        out_specs=[
            pl.BlockSpec((None, tq, W_GRP), lambda b, j, i: (b, i, j)),
            pl.BlockSpec((None, tq, LANES), lambda b, j, i: (b, i, j)),
        ],
        out_shape=[
            jax.ShapeDtypeStruct((B, L, dil * W_GRP), BF16),
            jax.ShapeDtypeStruct((B, L, dil * LANES), F32),
        ],
        scratch_shapes=[pltpu.VMEM((tq + 2 * DIL_HALF, COL_BLK), BF16)] * 2,
        compiler_params=_cparams(("arbitrary", "arbitrary", "arbitrary")),
        name=f"dilattn{g}",
    )(pv, pv, pv, pv, pv, pv, pv)
    return o.reshape(B * T, W_GRP), lse.reshape(B * T, LANES)


def _outproj_kernel(a_ref, o0_ref, o1_ref, o2_ref, l0_ref, l1_ref, l2_ref, ga_ref, gb_ref, x_ref,
                    wa_ref, wb_ref, wo_ref, g2_ref, wr_ref, x1_ref, lg_ref):
    o_refs = (o0_ref, o1_ref, o2_ref)
    lses = [r[...] for r in (l0_ref, l1_ref, l2_ref)]
    parts = []
    for h in range(N_HEADS_PER_DIL):
        cols = slice(h * HEAD_DIM, (h + 1) * HEAD_DIM)
        lh = [l[:, h:h + 1] for l in lses]
        mx = jnp.maximum(jnp.maximum(lh[0], lh[1]), lh[2])
        e = [jnp.exp(v - mx) for v in lh]
        den = e[0] + e[1] + e[2]
        acc = None
        for g in range(len(DIL_CONFIGS)):
            t = (e[g] / den) * o_refs[g][:, cols].astype(F32)
            acc = t if acc is None else acc + t
        parts.append(acc.astype(BF16))
    att_b = jnp.concatenate(parts, axis=-1)
    ya = jnp.dot(a_ref[...], wa_ref[...], preferred_element_type=F32)
    yb = jnp.dot(att_b, wb_ref[...], preferred_element_type=F32)
    mix = ga_ref[...].astype(F32) * ya + gb_ref[...].astype(F32) * yb
    x1 = x_ref[...] + jnp.dot(mix.astype(BF16), wo_ref[...], preferred_element_type=F32)
    x1_ref[...] = x1
    ms = jnp.mean(x1 * x1, axis=-1, keepdims=True)
    h2 = x1 * lax.rsqrt(ms + EPS) * g2_ref[...]
    lg_ref[...] = jnp.dot(h2.astype(BF16), wr_ref[...], preferred_element_type=F32)


def _outproj(att_a, os_, lses, proj, x2, wa, wb, wo, g2, wr, tm):
    n = x2.shape[0]
    row = lambda w: pl.BlockSpec((tm, w), lambda i: (i, 0))
    full = lambda a: pl.BlockSpec(a.shape, lambda i: (0,) * a.ndim)
    gate_blk = CB_GATE * COL_BLK // D_MODEL
    return pl.pallas_call(
        _outproj_kernel,
        grid=(n // tm,),
        in_specs=[row(W_NA), row(W_GRP), row(W_GRP), row(W_GRP), row(LANES), row(LANES), row(LANES),
                  pl.BlockSpec((tm, D_MODEL), lambda i: (i, gate_blk)),
                  pl.BlockSpec((tm, D_MODEL), lambda i: (i, gate_blk + 1)),
                  row(D_MODEL), full(wa), full(wb), full(wo), full(g2), full(wr)],
        out_specs=[row(D_MODEL), row(LANES)],
        out_shape=[jax.ShapeDtypeStruct((n, D_MODEL), F32), jax.ShapeDtypeStruct((n, LANES), F32)],
        compiler_params=_cparams(("arbitrary",)),
        name="outproj",
    )(att_a, *os_, *lses, proj, proj, x2, wa, wb, wo, g2, wr)


ROUTE_E0 = N_GROUPS


def _route_kernel(lg_ref, b_ref, gate_ref, eid_ref, rank_ref, cnt_ref, carry_ref):
    i = pl.program_id(0)

    @pl.when(i == 0)
    def _():
        carry_ref[...] = jnp.zeros_like(carry_ref)

    z = lg_ref[...] + b_ref[...]
    tm = z.shape[0]
    lane = lax.broadcasted_iota(jnp.int32, z.shape, 1)
    lane_f = lane.astype(F32)
    rmax = lambda v: jnp.max(v, axis=-1, keepdims=True)
    rmin = lambda v: jnp.min(v, axis=-1, keepdims=True)
    rsum = lambda v: jnp.sum(v, axis=-1, keepdims=True)

    gmask = lane < N_GROUPS
    gmax = rmax(jnp.where(gmask, z, -jnp.inf))
    gsel = rmin(jnp.where(gmask & (z == gmax), lane_f, float(LANES)))
    gp = 1.0 / rsum(jnp.where(gmask, jnp.exp(z - gmax), 0.0))

    elo = float(ROUTE_E0) + float(EXPERTS_PER_GROUP) * gsel
    emask = (lane_f >= elo) & (lane_f < elo + float(EXPERTS_PER_GROUP))
    emax = rmax(jnp.where(emask, z, -jnp.inf))
    ee = jnp.where(emask, jnp.exp(z - emax), 0.0)
    eprob = ee / rsum(ee)
    p1 = rmax(jnp.where(emask, eprob, -1.0))
    i1 = rmin(jnp.where(emask & (eprob == p1), lane_f, float(LANES)))
    m2 = emask & (lane_f != i1)
    p2 = rmax(jnp.where(m2, eprob, -1.0))
    i2 = rmin(jnp.where(m2 & (eprob == p2), lane_f, float(LANES)))
    den = p1 + p2
    g1 = gp * p1 / den
    g2 = gp * p2 / den

    oh1 = (lane_f == i1).astype(F32)
    oh2 = (lane_f == i2).astype(F32)
    oh = oh1 + oh2
    tri = (lax.broadcasted_iota(jnp.int32, (tm, tm), 1) < lax.broadcasted_iota(jnp.int32, (tm, tm), 0)).astype(BF16)
    tot = carry_ref[...] + jnp.dot(tri, oh.astype(BF16), preferred_element_type=F32)
    r1 = rsum(oh1 * tot)
    r2 = rsum(oh2 * tot)
    carry = carry_ref[...] + jnp.sum(oh, axis=0, keepdims=True)
    carry_ref[...] = carry
    cnt_ref[...] = carry

    two = lambda a, b: jnp.where(lane == 0, a, jnp.where(lane == 1, b, 0.0))
    gate_ref[...] = two(g1, g2)
    eid_ref[...] = two(i1 - float(ROUTE_E0), i2 - float(ROUTE_E0)).astype(jnp.int32)
    rank_ref[...] = two(r1, r2).astype(jnp.int32)


def _route(logits, rbias, tm):
    n = logits.shape[0]
    row = pl.BlockSpec((tm, LANES), lambda i: (i, 0))
    one = pl.BlockSpec((1, LANES), lambda i: (0, 0))
    return pl.pallas_call(
        _route_kernel,
        grid=(n // tm,),
        in_specs=[row, one],
        out_specs=[row, row, row, one],
        out_shape=[jax.ShapeDtypeStruct((n, LANES), F32), jax.ShapeDtypeStruct((n, LANES), jnp.int32),
                   jax.ShapeDtypeStruct((n, LANES), jnp.int32), jax.ShapeDtypeStruct((1, LANES), F32)],
        scratch_shapes=[pltpu.VMEM((1, LANES), F32)],
        compiler_params=_cparams(("arbitrary",)),
        name="route",
    )(logits, rbias)


TOP_K = 2


def _dispatch_kernel(pos_ref, x1_ref, g2_ref, xs_ref, h_ref, sem):
    x1 = x1_ref[...]
    tm = x1.shape[0]
    ms = jnp.mean(x1 * x1, axis=-1, keepdims=True)
    h_ref[...] = x1 * lax.rsqrt(ms + EPS) * g2_ref[...]

    def row_copy(r, p):
        return pltpu.make_async_copy(h_ref.at[pl.ds(r, 1)], xs_ref.at[pl.ds(p, 1)], sem)

    def start(r, c):
        for k in range(TOP_K):
            row_copy(r, pos_ref[0, TOP_K * r + k]).start()
        return c

    def wait(r, c):
        for k in range(TOP_K):
            row_copy(0, 0).wait()
        return c

    lax.fori_loop(0, tm, start, 0)
    lax.fori_loop(0, tm, wait, 0)


def _dispatch(pos3, x1, g2, tm):
    n = x1.shape[0]
    return pl.pallas_call(
        _dispatch_kernel,
        grid=(n // tm,),
        in_specs=[pl.BlockSpec((None, 1, TOP_K * tm), lambda i: (i, 0, 0), memory_space=pltpu.SMEM),
                  pl.BlockSpec((tm, D_MODEL), lambda i: (i, 0)),
                  pl.BlockSpec((1, D_MODEL), lambda i: (0, 0))],
        out_specs=pl.BlockSpec(memory_space=pl.ANY),
        out_shape=jax.ShapeDtypeStruct((n * TOP_K, D_MODEL), F32),
        scratch_shapes=[pltpu.VMEM((tm, D_MODEL), F32), pltpu.SemaphoreType.DMA(())],
        compiler_params=_cparams(("arbitrary",)),
        name="dispatch",
    )(pos3, x1, g2)


def _combine_kernel(pos_ref, gate_ref, x1_ref, ys_ref, y_ref, buf_ref, sem):
    tm = x1_ref.shape[0]

    def row_copy(r, k, p):
        return pltpu.make_async_copy(ys_ref.at[pl.ds(p, 1)], buf_ref.at[k, pl.ds(r, 1)], sem)

    def start(r, c):
        for k in range(TOP_K):
            row_copy(r, k, pos_ref[0, TOP_K * r + k]).start()
        return c

    def wait(r, c):
        for k in range(TOP_K):
            row_copy(0, k, 0).wait()
        return c

    lax.fori_loop(0, tm, start, 0)
    lax.fori_loop(0, tm, wait, 0)
    gates = gate_ref[...]
    y_ref[...] = x1_ref[...] + (gates[:, 0:1] * buf_ref[0] + gates[:, 1:2] * buf_ref[1])


def _combine(pos3, gates, x1, ys, tm):
    n = x1.shape[0]
    return pl.pallas_call(
        _combine_kernel,
        grid=(n // tm,),
        in_specs=[pl.BlockSpec((None, 1, TOP_K * tm), lambda i: (i, 0, 0), memory_space=pltpu.SMEM),
                  pl.BlockSpec((tm, LANES), lambda i: (i, 0)),
                  pl.BlockSpec((tm, D_MODEL), lambda i: (i, 0)),
                  pl.BlockSpec(memory_space=pl.ANY)],
        out_specs=pl.BlockSpec((tm, D_MODEL), lambda i: (i, 0)),
        out_shape=jax.ShapeDtypeStruct((n, D_MODEL), F32),
        scratch_shapes=[pltpu.VMEM((TOP_K, tm, D_MODEL), F32), pltpu.SemaphoreType.DMA(())],
        compiler_params=_cparams(("arbitrary",)),
        name="combine",
    )(pos3, gates, x1, ys)


def _experts_kernel(tile_s, exp_s, lo_s, hi_s, first_s, x_ref, wg_ref, wu_ref, wd_ref, y_ref):
    w = pl.program_id(0)
    lo, hi = lo_s[w], hi_s[w]

    @pl.when(hi > lo)
    def _():
        x = x_ref[...].astype(BF16)
        a = jnp.dot(x, wg_ref[...], preferred_element_type=F32)
        u = jnp.dot(x, wu_ref[...], preferred_element_type=F32)
        hm = (a * jax.nn.sigmoid(a) * u).astype(BF16)
        y = jnp.dot(hm, wd_ref[...], preferred_element_type=F32)
        rows = lax.broadcasted_iota(jnp.int32, y.shape, 0)
        mine = (rows >= lo) & (rows < hi)

        @pl.when(first_s[w] == 1)
        def _():
            y_ref[...] = jnp.where(mine, y, 0.0)

        @pl.when(first_s[w] == 0)
        def _():
            y_ref[...] = jnp.where(mine, y, y_ref[...])


def _experts(meta, xs, wg, wu, wd, bm):
    p = xs.shape[0]
    nw = p // bm + N_EXPERTS
    grid_spec = pltpu.PrefetchScalarGridSpec(
        num_scalar_prefetch=5,
        grid=(nw,),
        in_specs=[
            pl.BlockSpec((bm, D_MODEL), lambda w, t, e, lo, hi, f: (t[w], 0)),
            pl.BlockSpec((None, D_MODEL, D_EXPERT), lambda w, t, e, lo, hi, f: (e[w], 0, 0)),
            pl.BlockSpec((None, D_MODEL, D_EXPERT), lambda w, t, e, lo, hi, f: (e[w], 0, 0)),
            pl.BlockSpec((None, D_EXPERT, D_MODEL), lambda w, t, e, lo, hi, f: (e[w], 0, 0)),
        ],
        out_specs=pl.BlockSpec((bm, D_MODEL), lambda w, t, e, lo, hi, f: (t[w], 0)),
    )
    return pl.pallas_call(
        _experts_kernel,
        grid_spec=grid_spec,
        out_shape=jax.ShapeDtypeStruct((p, D_MODEL), F32),
        compiler_params=_cparams(("arbitrary",)),
        name="experts",
    )(*meta, xs, wg, wu, wd)


def _expert_work_items(counts, p, bm):
    ntiles = p // bm
    nw = ntiles + N_EXPERTS
    ends = jnp.cumsum(counts)
    starts = ends - counts
    ft = starts // bm
    nt = jnp.where(counts > 0, (ends - 1) // bm - ft + 1, 0)
    wend = jnp.cumsum(nt)
    wstart = wend - nt
    w = jnp.arange(nw, dtype=jnp.int32)
    used = w < wend[-1]
    e = jnp.minimum(jnp.searchsorted(wend, w, side="right"), N_EXPERTS - 1).astype(jnp.int32)
    e_last = jnp.max(jnp.where(counts > 0, jnp.arange(N_EXPERTS, dtype=jnp.int32), 0))
    e = jnp.where(used, e, e_last)
    tile = jnp.where(used, ft[e] + (w - wstart[e]), ntiles - 1).astype(jnp.int32)
    lo = jnp.where(used, jnp.maximum(starts[e], tile * bm) - tile * bm, 0).astype(jnp.int32)
    hi = jnp.where(used, jnp.minimum(ends[e], (tile + 1) * bm) - tile * bm, 0).astype(jnp.int32)
    first = jnp.concatenate([jnp.ones((1,), jnp.int32), (tile[1:] != tile[:-1]).astype(jnp.int32)])
    return (tile, e, lo, hi, first), starts


def _rope_tables(T):
    half = ROT_DIM // 2
    inv = 1.0 / (ROPE_THETA ** (jnp.arange(half, dtype=F32) * (2.0 / ROT_DIM)))
    ang = jnp.arange(T, dtype=F32)[:, None] * inv[None, :]
    cos, sin = jnp.cos(ang), jnp.sin(ang)
    rest = HEAD_DIM - ROT_DIM
    z_half, z_rest = jnp.zeros((T, half), F32), jnp.zeros((T, rest), F32)
    cos_t = jnp.concatenate([cos, cos, jnp.ones((T, rest), F32)], axis=1)
    sina_t = jnp.concatenate([-sin, z_half, z_rest], axis=1)
    sinb_t = jnp.concatenate([z_half, sin, z_rest], axis=1)
    return cos_t, sina_t, sinb_t


def _layer(x, p):
    B, T, D = x.shape
    n = B * T
    assert D == D_MODEL and T % NA_TQ == 0 and T // GRID_W >= NA_ROWS + NA_HALO_ROWS
    x2 = x.reshape(n, D)
    tm_in = 1024
    tm_out = 256
    tm_route = 512
    tm_move = 256
    bm = 256

    proj = _inproj(x2, p["g1"], p["w_in"], p["gains"], *_rope_tables(T), T, tm_in)
    att_a = _natten(proj.reshape(B, T, IN_WIDTH), p["na_bias"], B, T).reshape(n, W_NA)
    dil = [_dilattn(proj, B, T, g) for g in range(len(DIL_CONFIGS))]
    x1, logits = _outproj(att_a, [d[0] for d in dil], [d[1] for d in dil], proj, x2,
                          p["wa"], p["wb"], p["wo"], p["g2"], p["wr"], tm_out)

    gates, eid, rank, cnt = _route(logits, p["rbias"], tm_route)
    counts = cnt[0, ROUTE_E0:ROUTE_E0 + N_EXPERTS].astype(jnp.int32)
    meta, starts = _expert_work_items(counts, n * TOP_K, bm)
    pos = (starts[eid[:, :TOP_K]] + rank[:, :TOP_K]).astype(jnp.int32)
    pos3 = pos.reshape(n // tm_move, 1, TOP_K * tm_move)

    xs = _dispatch(pos3, x1, p["g2"], tm_move)
    ys = _experts(meta, xs, p["wg"], p["wu"], p["wd"], bm)
    y = _combine(pos3, gates, x1, ys, tm_move)
    return y.reshape(B, T, D)


def _prepare_params(norm1_g, w_in, qn_a, kn_a, rpb_a, qn_b, kn_b, w_branch_a, w_branch_b, w_out,
                    norm2_g, router_group_w, router_group_b, router_expert_w, router_expert_b, w_gate, w_up, w_down):
    assert norm1_g.shape[0] == 1
    l = 0
    pad_r = LANES - N_GROUPS - N_EXPERTS
    zrow = jnp.zeros((4, HEAD_DIM), F32)
    return {
        "g1": norm1_g[l][None, :],
        "w_in": w_in[l].astype(BF16),
        "gains": jnp.concatenate([qn_a[l][None], kn_a[l][None], qn_b[l][None], kn_b[l][None], zrow], axis=0),
        "na_bias": _na_bias_table(rpb_a[l]),
        "wa": w_branch_a[l].astype(BF16),
        "wb": w_branch_b[l].astype(BF16),
        "wo": w_out[l].astype(BF16),
        "g2": norm2_g[l][None, :],
        "wr": jnp.concatenate([router_group_w[l], router_expert_w[l], jnp.zeros((D_MODEL, pad_r), F32)],
                              axis=1).astype(BF16),
        "rbias": jnp.concatenate([router_group_b[l], router_expert_b[l], jnp.zeros((pad_r,), F32)])[None, :],
        "wg": w_gate[l].astype(BF16),
        "wu": w_up[l].astype(BF16),
        "wd": w_down[l].astype(BF16),
    }


def kernel(x_prompt, x_sample, norm1_g, w_in, qn_a, kn_a, rpb_a, qn_b, kn_b, w_branch_a, w_branch_b, w_out,
           norm2_g, router_group_w, router_group_b, router_expert_w, router_expert_b, w_gate, w_up, w_down):
    p = _prepare_params(norm1_g, w_in, qn_a, kn_a, rpb_a, qn_b, kn_b, w_branch_a, w_branch_b, w_out, norm2_g,
                        router_group_w, router_group_b, router_expert_w, router_expert_b, w_gate, w_up, w_down)
    return (_layer(x_prompt, p), _layer(x_sample, p))
```

```python
import functools

import jax
import jax.numpy as jnp
import numpy as np
from jax import lax
from jax.experimental import pallas as pl
from jax.experimental.pallas import tpu as pltpu

F32 = jnp.float32
BF16 = jnp.bfloat16

D_MODEL = 2048
HEAD_DIM = 128
N_HEADS_NA = 4
N_HEADS_PER_DIL = 4
DIL_CONFIGS = ((128, 1), (512, 4), (2048, 16))
W_NA = N_HEADS_NA * HEAD_DIM
W_DIL = N_HEADS_PER_DIL * len(DIL_CONFIGS) * HEAD_DIM
W_GRP = N_HEADS_PER_DIL * HEAD_DIM
GRID_W = 64
NA_ROWS = 8
NA_COLS = 16
ROT_DIM = HEAD_DIM // 4
ROPE_THETA = 500000.0
BAND_BLOCK = 128
N_GROUPS = 4
EXPERTS_PER_GROUP = 8
N_EXPERTS = N_GROUPS * EXPERTS_PER_GROUP
D_EXPERT = 1024
EPS = 1e-6
NEG = -1e30
IN_WIDTH = 3 * W_NA + 3 * W_DIL + 2 * D_MODEL
SCALE = HEAD_DIM ** -0.5

COL_BLK = 512
JB_G0, JB_G1, JB_G2, JB_GATE = 3, 6, 9, 12
N_COL_BLK = IN_WIDTH // COL_BLK

LANES = 128
VMEM_LIMIT = 56 * 1024 * 1024


def _cparams(sem, vmem=VMEM_LIMIT):
    return pltpu.CompilerParams(dimension_semantics=sem, vmem_limit_bytes=vmem)


def _inproj_kernel(x_ref, g1_ref, w_ref, gains_ref, cos_ref, sina_ref, sinb_ref,
                   main_ref, d1_ref, d2_ref, gate_ref, h_ref, y_ref):
    j = pl.program_id(1)
    nh = COL_BLK // HEAD_DIM

    @pl.when(j == 0)
    def _():
        x = x_ref[...]
        ms = jnp.mean(x * x, axis=-1, keepdims=True)
        h_ref[...] = (x * lax.rsqrt(ms + EPS) * g1_ref[...]).astype(BF16)

    acc = jnp.dot(h_ref[...], w_ref[...], preferred_element_type=F32)
    kind = j % 3
    is_qk = (j < JB_GATE) & (kind != 2)
    gain_row = kind + jnp.where(j < JB_G0, 0, 2)

    def head_norm(hh, gain):
        xh = acc[:, hh * HEAD_DIM:(hh + 1) * HEAD_DIM]
        ms = jnp.mean(xh * xh, axis=-1, keepdims=True)
        return xh * lax.rsqrt(ms + EPS) * gain

    def head_norm_rope(hh, gain, cos, sina, sinb):
        y = head_norm(hh, gain)
        return y * cos + pltpu.roll(y, HEAD_DIM - ROT_DIM // 2, 1) * sina + pltpu.roll(y, ROT_DIM // 2, 1) * sinb

    @pl.when(is_qk & (j < JB_G0))
    def _():
        gain = gains_ref[pl.ds(gain_row, 1), :]
        for hh in range(nh):
            main_ref[:, hh * HEAD_DIM:(hh + 1) * HEAD_DIM] = head_norm(hh, gain).astype(BF16)

    @pl.when(is_qk & (j >= JB_G0) & (j < JB_G1))
    def _():
        gain = gains_ref[pl.ds(gain_row, 1), :]
        cos, sina, sinb = cos_ref[...], sina_ref[...], sinb_ref[...]
        for hh in range(nh):
            main_ref[:, hh * HEAD_DIM:(hh + 1) * HEAD_DIM] = head_norm_rope(hh, gain, cos, sina, sinb).astype(BF16)

    @pl.when((kind == 2) & (j < JB_G1))
    def _():
        main_ref[...] = acc.astype(BF16)

    @pl.when(is_qk & (j >= JB_G1))
    def _():
        gain = gains_ref[pl.ds(gain_row, 1), :]
        cos, sina, sinb = cos_ref[...], sina_ref[...], sinb_ref[...]
        for hh in range(nh):
            y_ref[hh] = head_norm_rope(hh, gain, cos, sina, sinb)

    @pl.when((kind == 2) & (j >= JB_G1) & (j < JB_GATE))
    def _():
        for hh in range(nh):
            y_ref[hh] = acc[:, hh * HEAD_DIM:(hh + 1) * HEAD_DIM]

    def deinterleave(dst_ref, dil):
        rows = y_ref.shape[1] // dil
        for ph in range(dil):
            for hh in range(nh):
                dst_ref[ph, :, hh * HEAD_DIM:(hh + 1) * HEAD_DIM] = (
                    y_ref[hh, pl.ds(ph, rows, stride=dil), :].astype(BF16))

    @pl.when((j >= JB_G1) & (j < JB_G2))
    def _():
        deinterleave(d1_ref, DIL_CONFIGS[1][1])

    @pl.when((j >= JB_G2) & (j < JB_GATE))
    def _():
        deinterleave(d2_ref, DIL_CONFIGS[2][1])

    @pl.when(j >= JB_GATE)
    def _():
        gate_ref[...] = jax.nn.sigmoid(acc).astype(BF16)


def _inproj(x2, g1, w_in_bf, gains, cos_t, sina_t, sinb_t, B, T, tm):
    n = x2.shape[0]
    tpos = T // tm
    d1, d2 = DIL_CONFIGS[1][1], DIL_CONFIGS[2][1]
    tab = pl.BlockSpec((tm, HEAD_DIM), lambda i, j: (i % tpos, 0))
    return pl.pallas_call(
        _inproj_kernel,
        grid=(n // tm, N_COL_BLK),
        in_specs=[
            pl.BlockSpec((tm, D_MODEL), lambda i, j: (i, 0)),
            pl.BlockSpec((1, D_MODEL), lambda i, j: (0, 0)),
            pl.BlockSpec((D_MODEL, COL_BLK), lambda i, j: (0, j)),
            pl.BlockSpec((8, HEAD_DIM), lambda i, j: (0, 0)),
            tab, tab, tab,
        ],
        out_specs=[
            pl.BlockSpec((tm, COL_BLK), lambda i, j: (i, jnp.minimum(j, JB_G1 - 1))),
            pl.BlockSpec((None, d1, tm // d1, COL_BLK),
                         lambda i, j: (i // tpos, 0, i % tpos, jnp.clip(j - JB_G1, 0, 2))),
            pl.BlockSpec((None, d2, tm // d2, COL_BLK),
                         lambda i, j: (i // tpos, 0, i % tpos, jnp.clip(j - JB_G2, 0, 2))),
            pl.BlockSpec((tm, COL_BLK), lambda i, j: (i, jnp.clip(j - JB_GATE, 0, N_COL_BLK - JB_GATE - 1))),
        ],
        out_shape=[
            jax.ShapeDtypeStruct((n, JB_G1 * COL_BLK), BF16),
            jax.ShapeDtypeStruct((B, d1, T // d1, 3 * COL_BLK), BF16),
            jax.ShapeDtypeStruct((B, d2, T // d2, 3 * COL_BLK), BF16),
            jax.ShapeDtypeStruct((n, 2 * D_MODEL), BF16),
        ],
        scratch_shapes=[pltpu.VMEM((tm, D_MODEL), BF16), pltpu.VMEM((COL_BLK // HEAD_DIM, tm, HEAD_DIM), F32)],
        compiler_params=_cparams(("arbitrary", "arbitrary")),
        name="inproj",
    )(x2, g1, w_in_bf, gains, cos_t, sina_t, sinb_t)


NA_CHUNK_ROWS = 16
NA_HALO_ROWS = 4
NA_TQ = NA_CHUNK_ROWS * GRID_W
NA_TH = NA_HALO_ROWS * GRID_W
NA_WIN = NA_ROWS * GRID_W


def _natten_kernel(q_ref, kp_ref, km_ref, kn_ref, vp_ref, vm_ref, vn_ref, bias_ref, o_ref, kw_ref, vw_ref, *, rows):
    i = pl.program_id(1)
    kw_ref[0:NA_TH] = kp_ref[...]
    kw_ref[NA_TH:NA_TH + NA_TQ] = km_ref[...]
    kw_ref[NA_TH + NA_TQ:] = kn_ref[...]
    vw_ref[0:NA_TH] = vp_ref[...]
    vw_ref[NA_TH:NA_TH + NA_TQ] = vm_ref[...]
    vw_ref[NA_TH + NA_TQ:] = vn_ref[...]

    def body(rr, carry):
        r = i * NA_CHUNK_ROWS + rr
        rs = jnp.clip(r - NA_ROWS // 2, 0, rows - NA_ROWS)
        var = r - rs
        off = pl.multiple_of((rs - (i * NA_CHUNK_ROWS - NA_HALO_ROWS)) * GRID_W, GRID_W)
        qoff = pl.multiple_of(rr * GRID_W, GRID_W)
        for h in range(N_HEADS_NA):
            cols = slice(h * HEAD_DIM, (h + 1) * HEAD_DIM)
            q = q_ref[pl.ds(qoff, GRID_W), cols]
            k = kw_ref[pl.ds(off, NA_WIN), cols]
            v = vw_ref[pl.ds(off, NA_WIN), cols]
            s = lax.dot_general(q, k, (((1,), (1,)), ((), ())), preferred_element_type=F32)
            s = s * SCALE + bias_ref[var, h]
            m = jnp.max(s, axis=-1, keepdims=True)
            p = jnp.exp(s - m)
            l = jnp.sum(p, axis=-1, keepdims=True)
            o = jnp.dot(p.astype(BF16), v, preferred_element_type=F32) / l
            o_ref[pl.ds(qoff, GRID_W), cols] = o.astype(BF16)
        return carry

    lax.fori_loop(0, NA_CHUNK_ROWS, body, 0)


def _natten(proj3, bias_tab, B, T):
    rows = T // GRID_W
    nh = T // NA_TH
    ratio = NA_TQ // NA_TH
    blk = lambda rws, fn: pl.BlockSpec((None, rws, COL_BLK), fn)
    return pl.pallas_call(
        functools.partial(_natten_kernel, rows=rows),
        grid=(B, T // NA_TQ),
        in_specs=[
            blk(NA_TQ, lambda b, i: (b, i, 0)),
            blk(NA_TH, lambda b, i: (b, jnp.maximum(i * ratio - 1, 0), 1)),
            blk(NA_TQ, lambda b, i: (b, i, 1)),
            blk(NA_TH, lambda b, i: (b, jnp.minimum((i + 1) * ratio, nh - 1), 1)),
            blk(NA_TH, lambda b, i: (b, jnp.maximum(i * ratio - 1, 0), 2)),
            blk(NA_TQ, lambda b, i: (b, i, 2)),
            blk(NA_TH, lambda b, i: (b, jnp.minimum((i + 1) * ratio, nh - 1), 2)),
            pl.BlockSpec((NA_ROWS, N_HEADS_NA, GRID_W, NA_WIN), lambda b, i: (0, 0, 0, 0)),
        ],
        out_specs=pl.BlockSpec((None, NA_TQ, W_NA), lambda b, i: (b, i, 0)),
        out_shape=jax.ShapeDtypeStruct((B, T, W_NA), BF16),
        scratch_shapes=[pltpu.VMEM((NA_TQ + 2 * NA_TH, COL_BLK), BF16)] * 2,
        compiler_params=_cparams(("arbitrary", "arbitrary")),
        name="natten",
    )(proj3, proj3, proj3, proj3, proj3, proj3, proj3, bias_tab)


def _na_bias_table(rpb):
    c = np.arange(GRID_W)
    cs = np.clip(c - NA_COLS // 2, 0, GRID_W - NA_COLS)
    cc = np.arange(GRID_W)
    valid = (cc[None, :] >= cs[:, None]) & (cc[None, :] < cs[:, None] + NA_COLS)
    dc_idx = np.clip(cc[None, :] - c[:, None] + NA_COLS - 1, 0, 2 * NA_COLS - 2)
    pick = (dc_idx[..., None] == np.arange(2 * NA_COLS - 1)).astype(np.float32)
    t = jnp.einsum("qkd,hrd->hrqk", pick, rpb.astype(F32), precision=lax.Precision.HIGHEST)
    t = jnp.where(valid[None, None], t, NEG)
    tab = jnp.stack([t[:, NA_ROWS - 1 - var:2 * NA_ROWS - 1 - var] for var in range(NA_ROWS)], axis=0)
    return tab.transpose(0, 1, 3, 2, 4).reshape(NA_ROWS, N_HEADS_NA, GRID_W, NA_WIN)


DIL_HALF = 64
DIL_KB = BAND_BLOCK + 2 * DIL_HALF
DIL_TOKENS_PER_STEP = (1024, 1024, 2048)


def _dilattn_kernel(q_ref, kp_ref, km_ref, kn_ref, vp_ref, vm_ref, vn_ref, o_ref, lse_ref,
                    kw_ref, vw_ref, os_ref, ls_ref, *, L, tq, dil):
    i = pl.program_id(1)
    kw_ref[:, 0:DIL_HALF] = kp_ref[...]
    kw_ref[:, DIL_HALF:DIL_HALF + tq] = km_ref[...]
    kw_ref[:, DIL_HALF + tq:] = kn_ref[...]
    vw_ref[:, 0:DIL_HALF] = vp_ref[...]
    vw_ref[:, DIL_HALF:DIL_HALF + tq] = vm_ref[...]
    vw_ref[:, DIL_HALF + tq:] = vn_ref[...]

    def phase(j):
        r_io = lax.broadcasted_iota(jnp.int32, (BAND_BLOCK, DIL_KB), 0)
        c_io = lax.broadcasted_iota(jnp.int32, (BAND_BLOCK, DIL_KB), 1)
        lane = lax.broadcasted_iota(jnp.int32, (BAND_BLOCK, LANES), 1)
        for s in range(tq // BAND_BLOCK):
            base = i * tq + s * BAND_BLOCK - DIL_HALF
            lo = jnp.maximum(r_io, -base)
            hi = jnp.minimum(r_io + 2 * DIL_HALF, L - 1 - base)
            valid = (c_io >= lo) & (c_io <= hi)
            out_rows = pl.ds(s * BAND_BLOCK * dil + j, BAND_BLOCK, stride=dil)
            lse_tile = jnp.zeros((BAND_BLOCK, LANES), F32)
            for h in range(N_HEADS_PER_DIL):
                cols = slice(h * HEAD_DIM, (h + 1) * HEAD_DIM)
                q = q_ref[j, s * BAND_BLOCK:(s + 1) * BAND_BLOCK, cols]
                k = kw_ref[j, s * BAND_BLOCK:s * BAND_BLOCK + DIL_KB, cols]
                v = vw_ref[j, s * BAND_BLOCK:s * BAND_BLOCK + DIL_KB, cols]
                sc = lax.dot_general(q, k, (((1,), (1,)), ((), ())), preferred_element_type=F32) * SCALE
                sc = jnp.where(valid, sc, NEG)
                m = jnp.max(sc, axis=-1, keepdims=True)
                p = jnp.exp(sc - m)
                l = jnp.sum(p, axis=-1, keepdims=True)
                os_ref[h, out_rows, :] = jnp.dot(p.astype(BF16), v, preferred_element_type=F32) / l
                lse_tile = jnp.where(lane == h, m + jnp.log(l), lse_tile)
            ls_ref[out_rows, :] = lse_tile

    if dil == 1:
        phase(0)
    else:
        def body(j, c):
            phase(j)
            return c
        lax.fori_loop(0, dil, body, 0)
    for h in range(N_HEADS_PER_DIL):
        o_ref[:, h * HEAD_DIM:(h + 1) * HEAD_DIM] = os_ref[h].astype(BF16)
    lse_ref[...] = ls_ref[...]


def _dilattn(qkv, col0, B, T, g):
    dil = DIL_CONFIGS[g][1]
    assert DIL_CONFIGS[g][0] // 2 // dil == DIL_HALF
    L = T // dil
    tq = min(DIL_TOKENS_PER_STEP[g] // dil, L)
    nh = L // DIL_HALF
    ratio = tq // DIL_HALF
    blk = lambda rws, fn: pl.BlockSpec((None, dil, rws, COL_BLK), fn)
    prev = lambda i: jnp.maximum(i * ratio - 1, 0)
    nxt = lambda i: jnp.minimum((i + 1) * ratio, nh - 1)
    o, lse = pl.pallas_call(
        functools.partial(_dilattn_kernel, L=L, tq=tq, dil=dil),
        grid=(B, L // tq),
        in_specs=[
            blk(tq, lambda b, i: (b, 0, i, col0)),
            blk(DIL_HALF, lambda b, i: (b, 0, prev(i), col0 + 1)),
            blk(tq, lambda b, i: (b, 0, i, col0 + 1)),
            blk(DIL_HALF, lambda b, i: (b, 0, nxt(i), col0 + 1)),
            blk(DIL_HALF, lambda b, i: (b, 0, prev(i), col0 + 2)),
            blk(tq, lambda b, i: (b, 0, i, col0 + 2)),
            blk(DIL_HALF, lambda b, i: (b, 0, nxt(i), col0 + 2)),
        ],
        out_specs=[
            pl.BlockSpec((None, tq * dil, W_GRP), lambda b, i: (b, i, 0)),
            pl.BlockSpec((None, tq * dil, LANES), lambda b, i: (b, i, 0)),
        ],
        out_shape=[
            jax.ShapeDtypeStruct((B, T, W_GRP), BF16),
            jax.ShapeDtypeStruct((B, T, LANES), F32),
        ],
        scratch_shapes=[pltpu.VMEM((dil, tq + 2 * DIL_HALF, COL_BLK), BF16)] * 2
        + [pltpu.VMEM((N_HEADS_PER_DIL, tq * dil, HEAD_DIM), F32), pltpu.VMEM((tq * dil, LANES), F32)],
        compiler_params=_cparams(("arbitrary", "arbitrary")),
        name=f"dilattn{g}",
    )(qkv, qkv, qkv, qkv, qkv, qkv, qkv)
    return o.reshape(B * T, W_GRP), lse.reshape(B * T, LANES)


def _outproj_kernel(a_ref, o0_ref, o1_ref, o2_ref, l0_ref, l1_ref, l2_ref, ga_ref, gb_ref, x_ref,
                    wa_ref, wb_ref, wo_ref, g2_ref, wr_ref, x1_ref, lg_ref):
    o_refs = (o0_ref, o1_ref, o2_ref)
    lses = [r[...] for r in (l0_ref, l1_ref, l2_ref)]
    parts = []
    for h in range(N_HEADS_PER_DIL):
        cols = slice(h * HEAD_DIM, (h + 1) * HEAD_DIM)
        lh = [l[:, h:h + 1] for l in lses]
        mx = jnp.maximum(jnp.maximum(lh[0], lh[1]), lh[2])
        e = [jnp.exp(v - mx) for v in lh]
        den = e[0] + e[1] + e[2]
        acc = None
        for g in range(len(DIL_CONFIGS)):
            t = (e[g] / den) * o_refs[g][:, cols].astype(F32)
            acc = t if acc is None else acc + t
        parts.append(acc.astype(BF16))
    att_b = jnp.concatenate(parts, axis=-1)
    ya = jnp.dot(a_ref[...], wa_ref[...], preferred_element_type=F32)
    yb = jnp.dot(att_b, wb_ref[...], preferred_element_type=F32)
    mix = ga_ref[...].astype(F32) * ya + gb_ref[...].astype(F32) * yb
    x1 = x_ref[...] + jnp.dot(mix.astype(BF16), wo_ref[...], preferred_element_type=F32)
    x1_ref[...] = x1
    ms = jnp.mean(x1 * x1, axis=-1, keepdims=True)
    h2 = x1 * lax.rsqrt(ms + EPS) * g2_ref[...]
    lg_ref[...] = jnp.dot(h2.astype(BF16), wr_ref[...], preferred_element_type=F32)


def _outproj(att_a, os_, lses, gates, x2, wa, wb, wo, g2, wr, tm):
    n = x2.shape[0]
    row = lambda w: pl.BlockSpec((tm, w), lambda i: (i, 0))
    full = lambda a: pl.BlockSpec(a.shape, lambda i: (0,) * a.ndim)
    return pl.pallas_call(
        _outproj_kernel,
        grid=(n // tm,),
        in_specs=[row(W_NA), row(W_GRP), row(W_GRP), row(W_GRP), row(LANES), row(LANES), row(LANES),
                  pl.BlockSpec((tm, D_MODEL), lambda i: (i, 0)),
                  pl.BlockSpec((tm, D_MODEL), lambda i: (i, 1)),
                  row(D_MODEL), full(wa), full(wb), full(wo), full(g2), full(wr)],
        out_specs=[row(D_MODEL), row(LANES)],
        out_shape=[jax.ShapeDtypeStruct((n, D_MODEL), F32), jax.ShapeDtypeStruct((n, LANES), F32)],
        compiler_params=_cparams(("arbitrary",)),
        name="outproj",
    )(att_a, *os_, *lses, gates, gates, x2, wa, wb, wo, g2, wr)


ROUTE_E0 = N_GROUPS


def _route_kernel(lg_ref, b_ref, gate_ref, eid_ref, rank_ref, cnt_ref, carry_ref):
    i = pl.program_id(0)

    @pl.when(i == 0)
    def _():
        carry_ref[...] = jnp.zeros_like(carry_ref)

    z = lg_ref[...] + b_ref[...]
    tm = z.shape[0]
    lane = lax.broadcasted_iota(jnp.int32, z.shape, 1)
    lane_f = lane.astype(F32)
    rmax = lambda v: jnp.max(v, axis=-1, keepdims=True)
    rmin = lambda v: jnp.min(v, axis=-1, keepdims=True)
    rsum = lambda v: jnp.sum(v, axis=-1, keepdims=True)

    gmask = lane < N_GROUPS
    gmax = rmax(jnp.where(gmask, z, -jnp.inf))
    gsel = rmin(jnp.where(gmask & (z == gmax), lane_f, float(LANES)))
    gp = 1.0 / rsum(jnp.where(gmask, jnp.exp(z - gmax), 0.0))

    elo = float(ROUTE_E0) + float(EXPERTS_PER_GROUP) * gsel
    emask = (lane_f >= elo) & (lane_f < elo + float(EXPERTS_PER_GROUP))
    emax = rmax(jnp.where(emask, z, -jnp.inf))
    ee = jnp.where(emask, jnp.exp(z - emax), 0.0)
    eprob = ee / rsum(ee)
    p1 = rmax(jnp.where(emask, eprob, -1.0))
    i1 = rmin(jnp.where(emask & (eprob == p1), lane_f, float(LANES)))
    m2 = emask & (lane_f != i1)
    p2 = rmax(jnp.where(m2, eprob, -1.0))
    i2 = rmin(jnp.where(m2 & (eprob == p2), lane_f, float(LANES)))
    den = p1 + p2
    g1 = gp * p1 / den
    g2 = gp * p2 / den

    oh1 = (lane_f == i1).astype(F32)
    oh2 = (lane_f == i2).astype(F32)
    oh = oh1 + oh2
    tri = (lax.broadcasted_iota(jnp.int32, (tm, tm), 1) < lax.broadcasted_iota(jnp.int32, (tm, tm), 0)).astype(BF16)
    tot = carry_ref[...] + jnp.dot(tri, oh.astype(BF16), preferred_element_type=F32)
    r1 = rsum(oh1 * tot)
    r2 = rsum(oh2 * tot)
    carry = carry_ref[...] + jnp.sum(oh, axis=0, keepdims=True)
    carry_ref[...] = carry
    cnt_ref[...] = carry

    two = lambda a, b: jnp.where(lane == 0, a, jnp.where(lane == 1, b, 0.0))
    gate_ref[...] = two(g1, g2)
    eid_ref[...] = two(i1 - float(ROUTE_E0), i2 - float(ROUTE_E0)).astype(jnp.int32)
    rank_ref[...] = two(r1, r2).astype(jnp.int32)


def _route(logits, rbias, tm):
    n = logits.shape[0]
    row = pl.BlockSpec((tm, LANES), lambda i: (i, 0))
    one = pl.BlockSpec((1, LANES), lambda i: (0, 0))
    return pl.pallas_call(
        _route_kernel,
        grid=(n // tm,),
        in_specs=[row, one],
        out_specs=[row, row, row, one],
        out_shape=[jax.ShapeDtypeStruct((n, LANES), F32), jax.ShapeDtypeStruct((n, LANES), jnp.int32),
                   jax.ShapeDtypeStruct((n, LANES), jnp.int32), jax.ShapeDtypeStruct((1, LANES), F32)],
        scratch_shapes=[pltpu.VMEM((1, LANES), F32)],
        compiler_params=_cparams(("arbitrary",)),
        name="route",
    )(logits, rbias)


TOP_K = 2


def _dispatch_kernel(pos_ref, x1_ref, g2_ref, xs_ref, h_ref, sem):
    x1 = x1_ref[...]
    tm = x1.shape[0]
    ms = jnp.mean(x1 * x1, axis=-1, keepdims=True)
    h_ref[...] = x1 * lax.rsqrt(ms + EPS) * g2_ref[...]

    def row_copy(r, p):
        return pltpu.make_async_copy(h_ref.at[pl.ds(r, 1)], xs_ref.at[pl.ds(p, 1)], sem)

    def start(r, c):
        for k in range(TOP_K):
            row_copy(r, pos_ref[0, TOP_K * r + k]).start()
        return c

    def wait(r, c):
        for k in range(TOP_K):
            row_copy(0, 0).wait()
        return c

    lax.fori_loop(0, tm, start, 0)
    lax.fori_loop(0, tm, wait, 0)


def _dispatch(pos3, x1, g2, tm):
    n = x1.shape[0]
    return pl.pallas_call(
        _dispatch_kernel,
        grid=(n // tm,),
        in_specs=[pl.BlockSpec((None, 1, TOP_K * tm), lambda i: (i, 0, 0), memory_space=pltpu.SMEM),
                  pl.BlockSpec((tm, D_MODEL), lambda i: (i, 0)),
                  pl.BlockSpec((1, D_MODEL), lambda i: (0, 0))],
        out_specs=pl.BlockSpec(memory_space=pl.ANY),
        out_shape=jax.ShapeDtypeStruct((n * TOP_K, D_MODEL), F32),
        scratch_shapes=[pltpu.VMEM((tm, D_MODEL), F32), pltpu.SemaphoreType.DMA(())],
        compiler_params=_cparams(("arbitrary",)),
        name="dispatch",
    )(pos3, x1, g2)


def _combine_kernel(pos_ref, gate_ref, x1_ref, ys_ref, y_ref, buf_ref, sem):
    tm = x1_ref.shape[0]

    def row_copy(r, k, p):
        return pltpu.make_async_copy(ys_ref.at[pl.ds(p, 1)], buf_ref.at[k, pl.ds(r, 1)], sem)

    def start(r, c):
        for k in range(TOP_K):
            row_copy(r, k, pos_ref[0, TOP_K * r + k]).start()
        return c

    def wait(r, c):
        for k in range(TOP_K):
            row_copy(0, k, 0).wait()
        return c

    lax.fori_loop(0, tm, start, 0)
    lax.fori_loop(0, tm, wait, 0)
    gates = gate_ref[...]
    y_ref[...] = x1_ref[...] + (gates[:, 0:1] * buf_ref[0] + gates[:, 1:2] * buf_ref[1])


def _combine(pos3, gates, x1, ys, tm):
    n = x1.shape[0]
    return pl.pallas_call(
        _combine_kernel,
        grid=(n // tm,),
        in_specs=[pl.BlockSpec((None, 1, TOP_K * tm), lambda i: (i, 0, 0), memory_space=pltpu.SMEM),
                  pl.BlockSpec((tm, LANES), lambda i: (i, 0)),
                  pl.BlockSpec((tm, D_MODEL), lambda i: (i, 0)),
                  pl.BlockSpec(memory_space=pl.ANY)],
        out_specs=pl.BlockSpec((tm, D_MODEL), lambda i: (i, 0)),
        out_shape=jax.ShapeDtypeStruct((n, D_MODEL), F32),
        scratch_shapes=[pltpu.VMEM((TOP_K, tm, D_MODEL), F32), pltpu.SemaphoreType.DMA(())],
        compiler_params=_cparams(("arbitrary",)),
        name="combine",
    )(pos3, gates, x1, ys)


def _experts_kernel(tile_s, exp_s, lo_s, hi_s, first_s, x_ref, wg_ref, wu_ref, wd_ref, y_ref):
    w = pl.program_id(0)
    lo, hi = lo_s[w], hi_s[w]

    @pl.when(hi > lo)
    def _():
        x = x_ref[...].astype(BF16)
        a = jnp.dot(x, wg_ref[...], preferred_element_type=F32)
        u = jnp.dot(x, wu_ref[...], preferred_element_type=F32)
        hm = (a * jax.nn.sigmoid(a) * u).astype(BF16)
        y = jnp.dot(hm, wd_ref[...], preferred_element_type=F32)
        rows = lax.broadcasted_iota(jnp.int32, y.shape, 0)
        mine = (rows >= lo) & (rows < hi)

        @pl.when(first_s[w] == 1)
        def _():
            y_ref[...] = jnp.where(mine, y, 0.0)

        @pl.when(first_s[w] == 0)
        def _():
            y_ref[...] = jnp.where(mine, y, y_ref[...])


def _experts(meta, xs, wg, wu, wd, bm):
    p = xs.shape[0]
    nw = p // bm + N_EXPERTS
    grid_spec = pltpu.PrefetchScalarGridSpec(
        num_scalar_prefetch=5,
        grid=(nw,),
        in_specs=[
            pl.BlockSpec((bm, D_MODEL), lambda w, t, e, lo, hi, f: (t[w], 0)),
            pl.BlockSpec((None, D_MODEL, D_EXPERT), lambda w, t, e, lo, hi, f: (e[w], 0, 0)),
            pl.BlockSpec((None, D_MODEL, D_EXPERT), lambda w, t, e, lo, hi, f: (e[w], 0, 0)),
            pl.BlockSpec((None, D_EXPERT, D_MODEL), lambda w, t, e, lo, hi, f: (e[w], 0, 0)),
        ],
        out_specs=pl.BlockSpec((bm, D_MODEL), lambda w, t, e, lo, hi, f: (t[w], 0)),
    )
    return pl.pallas_call(
        _experts_kernel,
        grid_spec=grid_spec,
        out_shape=jax.ShapeDtypeStruct((p, D_MODEL), F32),
        compiler_params=_cparams(("arbitrary",)),
        name="experts",
    )(*meta, xs, wg, wu, wd)


def _expert_work_items(counts, p, bm):
    ntiles = p // bm
    nw = ntiles + N_EXPERTS
    ends = jnp.cumsum(counts)
    starts = ends - counts
    ft = starts // bm
    nt = jnp.where(counts > 0, (ends - 1) // bm - ft + 1, 0)
    wend = jnp.cumsum(nt)
    wstart = wend - nt
    w = jnp.arange(nw, dtype=jnp.int32)
    used = w < wend[-1]
    e = jnp.minimum(jnp.sum((w[:, None] >= wend[None, :]).astype(jnp.int32), axis=1), N_EXPERTS - 1)
    e_last = jnp.max(jnp.where(counts > 0, jnp.arange(N_EXPERTS, dtype=jnp.int32), 0))
    e = jnp.where(used, e, e_last)
    tile = jnp.where(used, ft[e] + (w - wstart[e]), ntiles - 1).astype(jnp.int32)
    lo = jnp.where(used, jnp.maximum(starts[e], tile * bm) - tile * bm, 0).astype(jnp.int32)
    hi = jnp.where(used, jnp.minimum(ends[e], (tile + 1) * bm) - tile * bm, 0).astype(jnp.int32)
    first = jnp.concatenate([jnp.ones((1,), jnp.int32), (tile[1:] != tile[:-1]).astype(jnp.int32)])
    return (tile, e, lo, hi, first), starts


def _rope_tables(T):
    half = ROT_DIM // 2
    inv = 1.0 / (ROPE_THETA ** (np.arange(half, dtype=np.float64) * (2.0 / ROT_DIM)))
    ang = np.arange(T, dtype=np.float64)[:, None] * inv[None, :]
    cos, sin = np.cos(ang), np.sin(ang)
    rest = HEAD_DIM - ROT_DIM
    z_half, z_rest = np.zeros((T, half)), np.zeros((T, rest))
    cos_t = np.concatenate([cos, cos, np.ones((T, rest))], axis=1)
    sina_t = np.concatenate([-sin, z_half, z_rest], axis=1)
    sinb_t = np.concatenate([z_half, sin, z_rest], axis=1)
    return tuple(jnp.asarray(t, dtype=F32) for t in (cos_t, sina_t, sinb_t))


def _layer(x, p):
    B, T, D = x.shape
    n = B * T
    assert D == D_MODEL and T % NA_TQ == 0 and T // GRID_W >= NA_ROWS + NA_HALO_ROWS
    x2 = x.reshape(n, D)
    tm_in = 1024
    tm_out = 256
    tm_route = 512
    tm_move = 256
    bm = 256

    main, qkv1, qkv2, sig = _inproj(x2, p["g1"], p["w_in"], p["gains"], *_rope_tables(T), B, T, tm_in)
    main3 = main.reshape(B, T, JB_G1 * COL_BLK)
    att_a = _natten(main3, p["na_bias"], B, T).reshape(n, W_NA)
    dil = [_dilattn(main3[:, None], JB_G0, B, T, 0), _dilattn(qkv1, 0, B, T, 1), _dilattn(qkv2, 0, B, T, 2)]
    x1, logits = _outproj(att_a, [d[0] for d in dil], [d[1] for d in dil], sig, x2,
                          p["wa"], p["wb"], p["wo"], p["g2"], p["wr"], tm_out)

    gates, eid, rank, cnt = _route(logits, p["rbias"], tm_route)
    counts = cnt[0, ROUTE_E0:ROUTE_E0 + N_EXPERTS].astype(jnp.int32)
    meta, starts = _expert_work_items(counts, n * TOP_K, bm)
    is_e = eid[:, :TOP_K, None] == jnp.arange(N_EXPERTS, dtype=jnp.int32)
    pos = jnp.sum(jnp.where(is_e, starts, 0), axis=-1) + rank[:, :TOP_K]
    pos3 = pos.astype(jnp.int32).reshape(n // tm_move, 1, TOP_K * tm_move)

    xs = _dispatch(pos3, x1, p["g2"], tm_move)
    ys = _experts(meta, xs, p["wg"], p["wu"], p["wd"], bm)
    y = _combine(pos3, gates, x1, ys, tm_move)
    return y.reshape(B, T, D)


def _prepare_params(norm1_g, w_in, qn_a, kn_a, rpb_a, qn_b, kn_b, w_branch_a, w_branch_b, w_out,
                    norm2_g, router_group_w, router_group_b, router_expert_w, router_expert_b, w_gate, w_up, w_down):
    assert norm1_g.shape[0] == 1
    l = 0
    pad_r = LANES - N_GROUPS - N_EXPERTS
    zrow = jnp.zeros((4, HEAD_DIM), F32)
    col = lambda start, width: w_in[l][:, start:start + width]
    qb0, kb0, vb0 = 3 * W_NA, 3 * W_NA + W_DIL, 3 * W_NA + 2 * W_DIL
    pieces = [col(0, 3 * W_NA)]
    for g in range(len(DIL_CONFIGS)):
        pieces += [col(qb0 + g * W_GRP, W_GRP), col(kb0 + g * W_GRP, W_GRP), col(vb0 + g * W_GRP, W_GRP)]
    pieces.append(col(3 * W_NA + 3 * W_DIL, 2 * D_MODEL))
    return {
        "g1": norm1_g[l][None, :],
        "w_in": jnp.concatenate(pieces, axis=1).astype(BF16),
        "gains": jnp.concatenate([qn_a[l][None], kn_a[l][None], qn_b[l][None], kn_b[l][None], zrow], axis=0),
        "na_bias": _na_bias_table(rpb_a[l]),
        "wa": w_branch_a[l].astype(BF16),
        "wb": w_branch_b[l].astype(BF16),
        "wo": w_out[l].astype(BF16),
        "g2": norm2_g[l][None, :],
        "wr": jnp.concatenate([router_group_w[l], router_expert_w[l], jnp.zeros((D_MODEL, pad_r), F32)],
                              axis=1).astype(BF16),
        "rbias": jnp.concatenate([router_group_b[l], router_expert_b[l], jnp.zeros((pad_r,), F32)])[None, :],
        "wg": w_gate[l].astype(BF16),
        "wu": w_up[l].astype(BF16),
        "wd": w_down[l].astype(BF16),
    }


def kernel(x_prompt, x_sample, norm1_g, w_in, qn_a, kn_a, rpb_a, qn_b, kn_b, w_branch_a, w_branch_b, w_out,
           norm2_g, router_group_w, router_group_b, router_expert_w, router_expert_b, w_gate, w_up, w_down):
    p = _prepare_params(norm1_g, w_in, qn_a, kn_a, rpb_a, qn_b, kn_b, w_branch_a, w_branch_b, w_out, norm2_g,
                        router_group_w, router_group_b, router_expert_w, router_expert_b, w_gate, w_up, w_down)
    return (_layer(x_prompt, p), _layer(x_sample, p))
```

```python
import functools

import jax
import jax.numpy as jnp
import numpy as np
from jax import lax
from jax.experimental import pallas as pl
from jax.experimental.pallas import tpu as pltpu

F32 = jnp.float32
BF16 = jnp.bfloat16

D_MODEL = 2048
HEAD_DIM = 128
N_HEADS_NA = 4
N_HEADS_PER_DIL = 4
DIL_CONFIGS = ((128, 1), (512, 4), (2048, 16))
W_NA = N_HEADS_NA * HEAD_DIM
W_DIL = N_HEADS_PER_DIL * len(DIL_CONFIGS) * HEAD_DIM
W_GRP = N_HEADS_PER_DIL * HEAD_DIM
GRID_W = 64
NA_ROWS = 8
NA_COLS = 16
ROT_DIM = HEAD_DIM // 4
ROPE_THETA = 500000.0
BAND_BLOCK = 128
N_GROUPS = 4
EXPERTS_PER_GROUP = 8
N_EXPERTS = N_GROUPS * EXPERTS_PER_GROUP
D_EXPERT = 1024
EPS = 1e-6
NEG = -1e30
IN_WIDTH = 3 * W_NA + 3 * W_DIL + 2 * D_MODEL
SCALE = HEAD_DIM ** -0.5

COL_BLK = 512
JB_G0, JB_G1, JB_G2, JB_GATE = 3, 6, 9, 12
N_COL_BLK = IN_WIDTH // COL_BLK

LANES = 128
VMEM_LIMIT = 56 * 1024 * 1024


def _cparams(sem, vmem=VMEM_LIMIT):
    return pltpu.CompilerParams(dimension_semantics=sem, vmem_limit_bytes=vmem)


def _inproj_kernel(x_ref, g1_ref, w_ref, gains_ref, cos_ref, sin_ref, ones_ref, perm_ref,
                   main_ref, d1_ref, d2_ref, gate_ref, h_ref, y_ref):
    j = pl.program_id(1)
    nh = COL_BLK // HEAD_DIM

    @pl.when(j == 0)
    def _():
        x = x_ref[...]
        ms = jnp.mean(x * x, axis=-1, keepdims=True)
        h_ref[...] = (x * lax.rsqrt(ms + EPS) * g1_ref[...]).astype(BF16)

    acc = jnp.dot(h_ref[...], w_ref[...], preferred_element_type=F32)
    kind = j % 3
    is_qk = (j < JB_GATE) & (kind != 2)
    gain_row = kind + jnp.where(j < JB_G0, 0, 2)

    def head_norm(hh, gain):
        xh = acc[:, hh * HEAD_DIM:(hh + 1) * HEAD_DIM]
        ss = jnp.dot((xh * xh).astype(BF16), ones_ref[...], preferred_element_type=F32)
        return xh * lax.rsqrt(ss * (1.0 / HEAD_DIM) + EPS) * gain

    def head_norm_rope(hh, gain, cos, sin):
        y = head_norm(hh, gain)
        return y * cos + jnp.dot(y.astype(BF16), perm_ref[...], preferred_element_type=F32) * sin

    @pl.when(is_qk & (j < JB_G0))
    def _():
        gain = gains_ref[pl.ds(gain_row, 1), :]
        for hh in range(nh):
            main_ref[:, hh * HEAD_DIM:(hh + 1) * HEAD_DIM] = head_norm(hh, gain).astype(BF16)

    @pl.when(is_qk & (j >= JB_G0) & (j < JB_G1))
    def _():
        gain = gains_ref[pl.ds(gain_row, 1), :]
        cos, sin = cos_ref[...], sin_ref[...]
        for hh in range(nh):
            main_ref[:, hh * HEAD_DIM:(hh + 1) * HEAD_DIM] = head_norm_rope(hh, gain, cos, sin).astype(BF16)

    @pl.when((kind == 2) & (j < JB_G1))
    def _():
        main_ref[...] = acc.astype(BF16)

    @pl.when(is_qk & (j >= JB_G1))
    def _():
        gain = gains_ref[pl.ds(gain_row, 1), :]
        cos, sin = cos_ref[...], sin_ref[...]
        for hh in range(nh):
            y_ref[hh] = head_norm_rope(hh, gain, cos, sin)

    @pl.when((kind == 2) & (j >= JB_G1) & (j < JB_GATE))
    def _():
        for hh in range(nh):
            y_ref[hh] = acc[:, hh * HEAD_DIM:(hh + 1) * HEAD_DIM]

    def deinterleave(dst_ref, dil):
        rows = y_ref.shape[1] // dil
        for ph in range(dil):
            for hh in range(nh):
                dst_ref[ph, :, hh * HEAD_DIM:(hh + 1) * HEAD_DIM] = (
                    y_ref[hh, pl.ds(ph, rows, stride=dil), :].astype(BF16))

    @pl.when((j >= JB_G1) & (j < JB_G2))
    def _():
        deinterleave(d1_ref, DIL_CONFIGS[1][1])

    @pl.when((j >= JB_G2) & (j < JB_GATE))
    def _():
        deinterleave(d2_ref, DIL_CONFIGS[2][1])

    @pl.when(j >= JB_GATE)
    def _():
        gate_ref[...] = (0.5 * jnp.tanh(0.5 * acc) + 0.5).astype(BF16)


def _inproj(x2, g1, w_in_bf, gains, cos_t, sin_t, B, T, tm):
    n = x2.shape[0]
    tpos = T // tm
    d1, d2 = DIL_CONFIGS[1][1], DIL_CONFIGS[2][1]
    tab = pl.BlockSpec((tm, HEAD_DIM), lambda i, j: (i % tpos, 0))
    sq = pl.BlockSpec((HEAD_DIM, HEAD_DIM), lambda i, j: (0, 0))
    half = ROT_DIM // 2
    lane = np.arange(HEAD_DIM)
    partner = np.where(lane < half, lane + half, lane - half)
    perm = ((lane[:, None] == partner[None, :]) & (lane[None, :] < ROT_DIM)).astype(np.float32)
    ones = np.ones((HEAD_DIM, HEAD_DIM), np.float32)
    return pl.pallas_call(
        _inproj_kernel,
        grid=(n // tm, N_COL_BLK),
        in_specs=[
            pl.BlockSpec((tm, D_MODEL), lambda i, j: (i, 0)),
            pl.BlockSpec((1, D_MODEL), lambda i, j: (0, 0)),
            pl.BlockSpec((D_MODEL, COL_BLK), lambda i, j: (0, j)),
            pl.BlockSpec((8, HEAD_DIM), lambda i, j: (0, 0)),
            tab, tab, sq, sq,
        ],
        out_specs=[
            pl.BlockSpec((tm, COL_BLK), lambda i, j: (i, jnp.minimum(j, JB_G1 - 1))),
            pl.BlockSpec((None, d1, tm // d1, COL_BLK),
                         lambda i, j: (i // tpos, 0, i % tpos, jnp.clip(j - JB_G1, 0, 2))),
            pl.BlockSpec((None, d2, tm // d2, COL_BLK),
                         lambda i, j: (i // tpos, 0, i % tpos, jnp.clip(j - JB_G2, 0, 2))),
            pl.BlockSpec((tm, COL_BLK), lambda i, j: (i, jnp.clip(j - JB_GATE, 0, N_COL_BLK - JB_GATE - 1))),
        ],
        out_shape=[
            jax.ShapeDtypeStruct((n, JB_G1 * COL_BLK), BF16),
            jax.ShapeDtypeStruct((B, d1, T // d1, 3 * COL_BLK), BF16),
            jax.ShapeDtypeStruct((B, d2, T // d2, 3 * COL_BLK), BF16),
            jax.ShapeDtypeStruct((n, 2 * D_MODEL), BF16),
        ],
        scratch_shapes=[pltpu.VMEM((tm, D_MODEL), BF16), pltpu.VMEM((COL_BLK // HEAD_DIM, tm, HEAD_DIM), F32)],
        compiler_params=_cparams(("arbitrary", "arbitrary")),
        name="inproj",
    )(x2, g1, w_in_bf, gains, cos_t, sin_t, jnp.asarray(ones, BF16), jnp.asarray(perm, BF16))


NA_CHUNK_ROWS = 16
NA_HALO_ROWS = 4
NA_TQ = NA_CHUNK_ROWS * GRID_W
NA_TH = NA_HALO_ROWS * GRID_W
NA_WIN = NA_ROWS * GRID_W


def _natten_kernel(q_ref, kp_ref, km_ref, kn_ref, vp_ref, vm_ref, vn_ref, bias_ref, o_ref, kw_ref, vw_ref, *, rows):
    i = pl.program_id(1)
    kw_ref[0:NA_TH] = kp_ref[...]
    kw_ref[NA_TH:NA_TH + NA_TQ] = km_ref[...]
    kw_ref[NA_TH + NA_TQ:] = kn_ref[...]
    vw_ref[0:NA_TH] = vp_ref[...]
    vw_ref[NA_TH:NA_TH + NA_TQ] = vm_ref[...]
    vw_ref[NA_TH + NA_TQ:] = vn_ref[...]

    def body(rr, carry):
        r = i * NA_CHUNK_ROWS + rr
        rs = jnp.clip(r - NA_ROWS // 2, 0, rows - NA_ROWS)
        var = r - rs
        off = pl.multiple_of((rs - (i * NA_CHUNK_ROWS - NA_HALO_ROWS)) * GRID_W, GRID_W)
        qoff = pl.multiple_of(rr * GRID_W, GRID_W)
        for h in range(N_HEADS_NA):
            cols = slice(h * HEAD_DIM, (h + 1) * HEAD_DIM)
            q = q_ref[pl.ds(qoff, GRID_W), cols]
            k = kw_ref[pl.ds(off, NA_WIN), cols]
            v = vw_ref[pl.ds(off, NA_WIN), cols]
            s = lax.dot_general(q, k, (((1,), (1,)), ((), ())), preferred_element_type=F32)
            s = s * SCALE + bias_ref[var, h]
            m = jnp.max(s, axis=-1, keepdims=True)
            p = jnp.exp(s - m)
            l = jnp.sum(p, axis=-1, keepdims=True)
            o = jnp.dot(p.astype(BF16), v, preferred_element_type=F32) / l
            o_ref[pl.ds(qoff, GRID_W), cols] = o.astype(BF16)
        return carry

    lax.fori_loop(0, NA_CHUNK_ROWS, body, 0, unroll=2)


def _natten(proj3, bias_tab, B, T):
    rows = T // GRID_W
    nh = T // NA_TH
    ratio = NA_TQ // NA_TH
    blk = lambda rws, fn: pl.BlockSpec((None, rws, COL_BLK), fn)
    return pl.pallas_call(
        functools.partial(_natten_kernel, rows=rows),
        grid=(B, T // NA_TQ),
        in_specs=[
            blk(NA_TQ, lambda b, i: (b, i, 0)),
            blk(NA_TH, lambda b, i: (b, jnp.maximum(i * ratio - 1, 0), 1)),
            blk(NA_TQ, lambda b, i: (b, i, 1)),
            blk(NA_TH, lambda b, i: (b, jnp.minimum((i + 1) * ratio, nh - 1), 1)),
            blk(NA_TH, lambda b, i: (b, jnp.maximum(i * ratio - 1, 0), 2)),
            blk(NA_TQ, lambda b, i: (b, i, 2)),
            blk(NA_TH, lambda b, i: (b, jnp.minimum((i + 1) * ratio, nh - 1), 2)),
            pl.BlockSpec((NA_ROWS, N_HEADS_NA, GRID_W, NA_WIN), lambda b, i: (0, 0, 0, 0)),
        ],
        out_specs=pl.BlockSpec((None, NA_TQ, W_NA), lambda b, i: (b, i, 0)),
        out_shape=jax.ShapeDtypeStruct((B, T, W_NA), BF16),
        scratch_shapes=[pltpu.VMEM((NA_TQ + 2 * NA_TH, COL_BLK), BF16)] * 2,
        compiler_params=_cparams(("arbitrary", "arbitrary")),
        name="natten",
    )(proj3, proj3, proj3, proj3, proj3, proj3, proj3, bias_tab)


def _na_bias_table(rpb):
    c = np.arange(GRID_W)
    cs = np.clip(c - NA_COLS // 2, 0, GRID_W - NA_COLS)
    cc = np.arange(GRID_W)
    valid = (cc[None, :] >= cs[:, None]) & (cc[None, :] < cs[:, None] + NA_COLS)
    dc_idx = np.clip(cc[None, :] - c[:, None] + NA_COLS - 1, 0, 2 * NA_COLS - 2)
    pick = (dc_idx[..., None] == np.arange(2 * NA_COLS - 1)).astype(np.float32)
    t = jnp.einsum("qkd,hrd->hrqk", pick, rpb.astype(F32), precision=lax.Precision.HIGHEST)
    t = jnp.where(valid[None, None], t, NEG)
    tab = jnp.stack([t[:, NA_ROWS - 1 - var:2 * NA_ROWS - 1 - var] for var in range(NA_ROWS)], axis=0)
    return tab.transpose(0, 1, 3, 2, 4).reshape(NA_ROWS, N_HEADS_NA, GRID_W, NA_WIN)


DIL_HALF = 64
DIL_KB = BAND_BLOCK + 2 * DIL_HALF
DIL_TOKENS_PER_STEP = (1024, 1024, 2048)


def _dilattn_kernel(q_ref, kp_ref, km_ref, kn_ref, vp_ref, vm_ref, vn_ref, o_ref, lse_ref,
                    kw_ref, vw_ref, os_ref, ls_ref, *, L, tq, dil):
    i = pl.program_id(1)
    kw_ref[:, 0:DIL_HALF] = kp_ref[...]
    kw_ref[:, DIL_HALF:DIL_HALF + tq] = km_ref[...]
    kw_ref[:, DIL_HALF + tq:] = kn_ref[...]
    vw_ref[:, 0:DIL_HALF] = vp_ref[...]
    vw_ref[:, DIL_HALF:DIL_HALF + tq] = vm_ref[...]
    vw_ref[:, DIL_HALF + tq:] = vn_ref[...]

    def phase(j):
        r_io = lax.broadcasted_iota(jnp.int32, (BAND_BLOCK, DIL_KB), 0)
        c_io = lax.broadcasted_iota(jnp.int32, (BAND_BLOCK, DIL_KB), 1)
        lane = lax.broadcasted_iota(jnp.int32, (BAND_BLOCK, LANES), 1)
        for s in range(tq // BAND_BLOCK):
            base = i * tq + s * BAND_BLOCK - DIL_HALF
            lo = jnp.maximum(r_io, -base)
            hi = jnp.minimum(r_io + 2 * DIL_HALF, L - 1 - base)
            valid = (c_io >= lo) & (c_io <= hi)
            out_rows = pl.ds(s * BAND_BLOCK * dil + j, BAND_BLOCK, stride=dil)
            lse_tile = jnp.zeros((BAND_BLOCK, LANES), F32)
            for h in range(N_HEADS_PER_DIL):
                cols = slice(h * HEAD_DIM, (h + 1) * HEAD_DIM)
                q = q_ref[j, s * BAND_BLOCK:(s + 1) * BAND_BLOCK, cols]
                k = kw_ref[j, s * BAND_BLOCK:s * BAND_BLOCK + DIL_KB, cols]
                v = vw_ref[j, s * BAND_BLOCK:s * BAND_BLOCK + DIL_KB, cols]
                sc = lax.dot_general(q, k, (((1,), (1,)), ((), ())), preferred_element_type=F32) * SCALE
                sc = jnp.where(valid, sc, NEG)
                m = jnp.max(sc, axis=-1, keepdims=True)
                p = jnp.exp(sc - m)
                l = jnp.sum(p, axis=-1, keepdims=True)
                os_ref[h, out_rows, :] = jnp.dot(p.astype(BF16), v, preferred_element_type=F32) / l
                lse_tile = jnp.where(lane == h, m + jnp.log(l), lse_tile)
            ls_ref[out_rows, :] = lse_tile

    if dil == 1:
        phase(0)
    else:
        def body(j, c):
            phase(j)
            return c
        lax.fori_loop(0, dil, body, 0)
    for h in range(N_HEADS_PER_DIL):
        o_ref[:, h * HEAD_DIM:(h + 1) * HEAD_DIM] = os_ref[h].astype(BF16)
    lse_ref[...] = ls_ref[...]


def _dilattn(qkv, col0, B, T, g):
    dil = DIL_CONFIGS[g][1]
    assert DIL_CONFIGS[g][0] // 2 // dil == DIL_HALF
    L = T // dil
    tq = min(DIL_TOKENS_PER_STEP[g] // dil, L)
    nh = L // DIL_HALF
    ratio = tq // DIL_HALF
    blk = lambda rws, fn: pl.BlockSpec((None, dil, rws, COL_BLK), fn)
    prev = lambda i: jnp.maximum(i * ratio - 1, 0)
    nxt = lambda i: jnp.minimum((i + 1) * ratio, nh - 1)
    o, lse = pl.pallas_call(
        functools.partial(_dilattn_kernel, L=L, tq=tq, dil=dil),
        grid=(B, L // tq),
        in_specs=[
            blk(tq, lambda b, i: (b, 0, i, col0)),
            blk(DIL_HALF, lambda b, i: (b, 0, prev(i), col0 + 1)),
            blk(tq, lambda b, i: (b, 0, i, col0 + 1)),
            blk(DIL_HALF, lambda b, i: (b, 0, nxt(i), col0 + 1)),
            blk(DIL_HALF, lambda b, i: (b, 0, prev(i), col0 + 2)),
            blk(tq, lambda b, i: (b, 0, i, col0 + 2)),
            blk(DIL_HALF, lambda b, i: (b, 0, nxt(i), col0 + 2)),
        ],
        out_specs=[
            pl.BlockSpec((None, tq * dil, W_GRP), lambda b, i: (b, i, 0)),
            pl.BlockSpec((None, tq * dil, LANES), lambda b, i: (b, i, 0)),
        ],
        out_shape=[
            jax.ShapeDtypeStruct((B, T, W_GRP), BF16),
            jax.ShapeDtypeStruct((B, T, LANES), F32),
        ],
        scratch_shapes=[pltpu.VMEM((dil, tq + 2 * DIL_HALF, COL_BLK), BF16)] * 2
        + [pltpu.VMEM((N_HEADS_PER_DIL, tq * dil, HEAD_DIM), F32), pltpu.VMEM((tq * dil, LANES), F32)],
        compiler_params=_cparams(("arbitrary", "arbitrary")),
        name=f"dilattn{g}",
    )(qkv, qkv, qkv, qkv, qkv, qkv, qkv)
    return o.reshape(B * T, W_GRP), lse.reshape(B * T, LANES)


def _outproj_kernel(a_ref, o0_ref, o1_ref, o2_ref, l0_ref, l1_ref, l2_ref, ga_ref, gb_ref, x_ref,
                    wa_ref, wb_ref, wo_ref, g2_ref, wr_ref, x1_ref, lg_ref):
    o_refs = (o0_ref, o1_ref, o2_ref)
    lses = [r[...] for r in (l0_ref, l1_ref, l2_ref)]
    parts = []
    for h in range(N_HEADS_PER_DIL):
        cols = slice(h * HEAD_DIM, (h + 1) * HEAD_DIM)
        lh = [l[:, h:h + 1] for l in lses]
        mx = jnp.maximum(jnp.maximum(lh[0], lh[1]), lh[2])
        e = [jnp.exp(v - mx) for v in lh]
        den = e[0] + e[1] + e[2]
        acc = None
        for g in range(len(DIL_CONFIGS)):
            t = (e[g] / den) * o_refs[g][:, cols].astype(F32)
            acc = t if acc is None else acc + t
        parts.append(acc.astype(BF16))
    att_b = jnp.concatenate(parts, axis=-1)
    ya = jnp.dot(a_ref[...], wa_ref[...], preferred_element_type=F32)
    yb = jnp.dot(att_b, wb_ref[...], preferred_element_type=F32)
    mix = ga_ref[...].astype(F32) * ya + gb_ref[...].astype(F32) * yb
    x1 = x_ref[...] + jnp.dot(mix.astype(BF16), wo_ref[...], preferred_element_type=F32)
    x1_ref[...] = x1
    ms = jnp.mean(x1 * x1, axis=-1, keepdims=True)
    h2 = x1 * lax.rsqrt(ms + EPS) * g2_ref[...]
    lg_ref[...] = jnp.dot(h2.astype(BF16), wr_ref[...], preferred_element_type=F32)


def _outproj(att_a, os_, lses, gates, x2, wa, wb, wo, g2, wr, tm):
    n = x2.shape[0]
    row = lambda w: pl.BlockSpec((tm, w), lambda i: (i, 0))
    full = lambda a: pl.BlockSpec(a.shape, lambda i: (0,) * a.ndim)
    return pl.pallas_call(
        _outproj_kernel,
        grid=(n // tm,),
        in_specs=[row(W_NA), row(W_GRP), row(W_GRP), row(W_GRP), row(LANES), row(LANES), row(LANES),
                  pl.BlockSpec((tm, D_MODEL), lambda i: (i, 0)),
                  pl.BlockSpec((tm, D_MODEL), lambda i: (i, 1)),
                  row(D_MODEL), full(wa), full(wb), full(wo), full(g2), full(wr)],
        out_specs=[row(D_MODEL), row(LANES)],
        out_shape=[jax.ShapeDtypeStruct((n, D_MODEL), F32), jax.ShapeDtypeStruct((n, LANES), F32)],
        compiler_params=_cparams(("arbitrary",)),
        name="outproj",
    )(att_a, *os_, *lses, gates, gates, x2, wa, wb, wo, g2, wr)


ROUTE_E0 = N_GROUPS


def _route_kernel(lg_ref, b_ref, gate_ref, eid_ref, rank_ref, cnt_ref, carry_ref):
    i = pl.program_id(0)

    @pl.when(i == 0)
    def _():
        carry_ref[...] = jnp.zeros_like(carry_ref)

    z = lg_ref[...] + b_ref[...]
    tm = z.shape[0]
    lane = lax.broadcasted_iota(jnp.int32, z.shape, 1)
    lane_f = lane.astype(F32)
    rmax = lambda v: jnp.max(v, axis=-1, keepdims=True)
    rmin = lambda v: jnp.min(v, axis=-1, keepdims=True)
    rsum = lambda v: jnp.sum(v, axis=-1, keepdims=True)

    gmask = lane < N_GROUPS
    gmax = rmax(jnp.where(gmask, z, -jnp.inf))
    gsel = rmin(jnp.where(gmask & (z == gmax), lane_f, float(LANES)))
    gp = 1.0 / rsum(jnp.where(gmask, jnp.exp(z - gmax), 0.0))

    elo = float(ROUTE_E0) + float(EXPERTS_PER_GROUP) * gsel
    emask = (lane_f >= elo) & (lane_f < elo + float(EXPERTS_PER_GROUP))
    emax = rmax(jnp.where(emask, z, -jnp.inf))
    ee = jnp.where(emask, jnp.exp(z - emax), 0.0)
    eprob = ee / rsum(ee)
    p1 = rmax(jnp.where(emask, eprob, -1.0))
    i1 = rmin(jnp.where(emask & (eprob == p1), lane_f, float(LANES)))
    m2 = emask & (lane_f != i1)
    p2 = rmax(jnp.where(m2, eprob, -1.0))
    i2 = rmin(jnp.where(m2 & (eprob == p2), lane_f, float(LANES)))
    den = p1 + p2
    g1 = gp * p1 / den
    g2 = gp * p2 / den

    oh1 = (lane_f == i1).astype(F32)
    oh2 = (lane_f == i2).astype(F32)
    oh = oh1 + oh2
    tri = (lax.broadcasted_iota(jnp.int32, (tm, tm), 1) < lax.broadcasted_iota(jnp.int32, (tm, tm), 0)).astype(BF16)
    tot = carry_ref[...] + jnp.dot(tri, oh.astype(BF16), preferred_element_type=F32)
    r1 = rsum(oh1 * tot)
    r2 = rsum(oh2 * tot)
    carry = carry_ref[...] + jnp.sum(oh, axis=0, keepdims=True)
    carry_ref[...] = carry
    cnt_ref[...] = carry

    two = lambda a, b: jnp.where(lane == 0, a, jnp.where(lane == 1, b, 0.0))
    gate_ref[...] = two(g1, g2)
    eid_ref[...] = two(i1 - float(ROUTE_E0), i2 - float(ROUTE_E0)).astype(jnp.int32)
    rank_ref[...] = two(r1, r2).astype(jnp.int32)


def _route(logits, rbias, tm):
    n = logits.shape[0]
    row = pl.BlockSpec((tm, LANES), lambda i: (i, 0))
    one = pl.BlockSpec((1, LANES), lambda i: (0, 0))
    return pl.pallas_call(
        _route_kernel,
        grid=(n // tm,),
        in_specs=[row, one],
        out_specs=[row, row, row, one],
        out_shape=[jax.ShapeDtypeStruct((n, LANES), F32), jax.ShapeDtypeStruct((n, LANES), jnp.int32),
                   jax.ShapeDtypeStruct((n, LANES), jnp.int32), jax.ShapeDtypeStruct((1, LANES), F32)],
        scratch_shapes=[pltpu.VMEM((1, LANES), F32)],
        compiler_params=_cparams(("arbitrary",)),
        name="route",
    )(logits, rbias)


TOP_K = 2


ROW_SUB = 8
HALF_D = D_MODEL // 2
assert ROW_SUB * LANES == HALF_D


def _pack_pair(lo, hi):
    lo_b = lax.bitcast_convert_type(lo.astype(BF16).astype(F32), jnp.uint32)
    hi_b = lax.bitcast_convert_type(hi.astype(BF16).astype(F32), jnp.uint32)
    return (lo_b >> 16) | hi_b


def _unpack_pair(w):
    lo = lax.bitcast_convert_type(w << 16, F32)
    hi = lax.bitcast_convert_type(w & jnp.uint32(0xFFFF0000), F32)
    return lo, hi


def _store_packed_rows(dst, x):
    for c in range(ROW_SUB):
        dst(c, _pack_pair(x[:, c * LANES:(c + 1) * LANES], x[:, HALF_D + c * LANES:HALF_D + (c + 1) * LANES]))


def _dispatch_kernel(pos_ref, x1_ref, g2_ref, xs_ref, hp_ref, sem):
    i = pl.program_id(0)
    last = pl.num_programs(0) - 1
    slot = i % 2
    x1 = x1_ref[...]
    tm = x1.shape[0]
    ms = jnp.mean(x1 * x1, axis=-1, keepdims=True)
    h = x1 * lax.rsqrt(ms + EPS) * g2_ref[...]

    def put(c, words):
        hp_ref[slot, pl.ds(c, tm, stride=ROW_SUB), :] = words

    _store_packed_rows(put, h)

    def row_copy(sl, r, p, k):
        src = hp_ref.at[sl, pl.ds(pl.multiple_of(r * ROW_SUB, ROW_SUB), ROW_SUB)]
        dst = xs_ref.at[pl.ds(pl.multiple_of(p * ROW_SUB, ROW_SUB), ROW_SUB)]
        return pltpu.make_async_copy(src, dst, sem.at[sl])

    def start(r, c):
        for k in range(TOP_K):
            row_copy(slot, r, pos_ref[0, TOP_K * r + k], k).start()
        return c

    def wait_all(sl):
        def wait(r, c):
            for k in range(TOP_K):
                row_copy(sl, 0, 0, k).wait()
            return c
        lax.fori_loop(0, tm, wait, 0)

    lax.fori_loop(0, tm, start, 0, unroll=4)

    @pl.when(i > 0)
    def _():
        wait_all(1 - slot)

    @pl.when(i == last)
    def _():
        wait_all(slot)


def _dispatch(pos3, x1, g2, tm):
    n = x1.shape[0]
    return pl.pallas_call(
        _dispatch_kernel,
        grid=(n // tm,),
        in_specs=[pl.BlockSpec((None, 1, TOP_K * tm), lambda i: (i, 0, 0), memory_space=pltpu.SMEM),
                  pl.BlockSpec((tm, D_MODEL), lambda i: (i, 0)),
                  pl.BlockSpec((1, D_MODEL), lambda i: (0, 0))],
        out_specs=pl.BlockSpec(memory_space=pl.ANY),
        out_shape=jax.ShapeDtypeStruct((n * TOP_K * ROW_SUB, LANES), jnp.uint32),
        scratch_shapes=[pltpu.VMEM((2, tm * ROW_SUB, LANES), jnp.uint32), pltpu.SemaphoreType.DMA((2,))],
        compiler_params=_cparams(("arbitrary",)),
        name="dispatch",
    )(pos3, x1, g2)


def _combine_kernel(pos_ref, posn_ref, gate_ref, x1_ref, ys_ref, y_ref, buf_ref, sem):
    i = pl.program_id(0)
    last = pl.num_programs(0) - 1
    slot = i % 2
    tm = x1_ref.shape[0]

    def row_copy(sl, k, r, p):
        src = ys_ref.at[pl.ds(pl.multiple_of(p * ROW_SUB, ROW_SUB), ROW_SUB)]
        dst = buf_ref.at[sl, k, pl.ds(pl.multiple_of(r * ROW_SUB, ROW_SUB), ROW_SUB)]
        return pltpu.make_async_copy(src, dst, sem.at[sl])

    def start_all(sl, p_ref):
        def start(r, c):
            for k in range(TOP_K):
                row_copy(sl, k, r, p_ref[0, TOP_K * r + k]).start()
            return c
        lax.fori_loop(0, tm, start, 0, unroll=4)

    @pl.when(i == 0)
    def _():
        start_all(0, pos_ref)

    @pl.when(i < last)
    def _():
        start_all(1 - slot, posn_ref)

    def wait(r, c):
        for k in range(TOP_K):
            row_copy(slot, k, 0, 0).wait()
        return c

    lax.fori_loop(0, tm, wait, 0)
    gates = gate_ref[...]
    g0 = jnp.broadcast_to(gates[:, 0:1], (tm, LANES))
    g1 = jnp.broadcast_to(gates[:, 1:2], (tm, LANES))
    for c in range(ROW_SUB):
        lo0, hi0 = _unpack_pair(buf_ref[slot, 0, pl.ds(c, tm, stride=ROW_SUB), :])
        lo1, hi1 = _unpack_pair(buf_ref[slot, 1, pl.ds(c, tm, stride=ROW_SUB), :])
        cl = slice(c * LANES, (c + 1) * LANES)
        ch = slice(HALF_D + c * LANES, HALF_D + (c + 1) * LANES)
        y_ref[:, cl] = x1_ref[:, cl] + (g0 * lo0 + g1 * lo1)
        y_ref[:, ch] = x1_ref[:, ch] + (g0 * hi0 + g1 * hi1)


def _combine(pos3, gates, x1, ys, tm):
    n = x1.shape[0]
    nt = n // tm
    pos_blk = lambda fn: pl.BlockSpec((None, 1, TOP_K * tm), fn, memory_space=pltpu.SMEM)
    return pl.pallas_call(
        _combine_kernel,
        grid=(nt,),
        in_specs=[pos_blk(lambda i: (i, 0, 0)),
                  pos_blk(lambda i: (jnp.minimum(i + 1, nt - 1), 0, 0)),
                  pl.BlockSpec((tm, LANES), lambda i: (i, 0)),
                  pl.BlockSpec((tm, D_MODEL), lambda i: (i, 0)),
                  pl.BlockSpec(memory_space=pl.ANY)],
        out_specs=pl.BlockSpec((tm, D_MODEL), lambda i: (i, 0)),
        out_shape=jax.ShapeDtypeStruct((n, D_MODEL), F32),
        scratch_shapes=[pltpu.VMEM((2, TOP_K, tm * ROW_SUB, LANES), jnp.uint32), pltpu.SemaphoreType.DMA((2,))],
        compiler_params=_cparams(("arbitrary",)),
        name="combine",
    )(pos3, pos3, gates, x1, ys)


def _experts_kernel(tile_s, exp_s, lo_s, hi_s, first_s, x_ref, wg_ref, wu_ref, wd_ref, y_ref, xb_ref):
    w = pl.program_id(0)
    lo, hi = lo_s[w], hi_s[w]
    bm = xb_ref.shape[0]

    @pl.when(hi > lo)
    def _():
        for c in range(ROW_SUB):
            xl, xh = _unpack_pair(x_ref[pl.ds(c, bm, stride=ROW_SUB), :])
            xb_ref[:, c * LANES:(c + 1) * LANES] = xl.astype(BF16)
            xb_ref[:, HALF_D + c * LANES:HALF_D + (c + 1) * LANES] = xh.astype(BF16)
        x = xb_ref[...]
        a = jnp.dot(x, wg_ref[...], preferred_element_type=F32)
        u = jnp.dot(x, wu_ref[...], preferred_element_type=F32)
        hm = (a * jax.nn.sigmoid(a) * u).astype(BF16)
        y = jnp.dot(hm, wd_ref[...], preferred_element_type=F32)
        rows = lax.broadcasted_iota(jnp.int32, (bm, LANES), 0)
        mine = (rows >= lo) & (rows < hi)

        @pl.when(first_s[w] == 1)
        def _():
            def put(c, words):
                y_ref[pl.ds(c, bm, stride=ROW_SUB), :] = jnp.where(mine, words, jnp.uint32(0))
            _store_packed_rows(put, y)

        @pl.when(first_s[w] == 0)
        def _():
            def put(c, words):
                rws = pl.ds(c, bm, stride=ROW_SUB)
                y_ref[rws, :] = jnp.where(mine, words, y_ref[rws, :])
            _store_packed_rows(put, y)


def _experts(meta, xs, wg, wu, wd, bm):
    p = xs.shape[0] // ROW_SUB
    nw = p // bm + N_EXPERTS
    grid_spec = pltpu.PrefetchScalarGridSpec(
        num_scalar_prefetch=5,
        grid=(nw,),
        in_specs=[
            pl.BlockSpec((bm * ROW_SUB, LANES), lambda w, t, e, lo, hi, f: (t[w], 0)),
            pl.BlockSpec((None, D_MODEL, D_EXPERT), lambda w, t, e, lo, hi, f: (e[w], 0, 0)),
            pl.BlockSpec((None, D_MODEL, D_EXPERT), lambda w, t, e, lo, hi, f: (e[w], 0, 0)),
            pl.BlockSpec((None, D_EXPERT, D_MODEL), lambda w, t, e, lo, hi, f: (e[w], 0, 0)),
        ],
        out_specs=pl.BlockSpec((bm * ROW_SUB, LANES), lambda w, t, e, lo, hi, f: (t[w], 0)),
        scratch_shapes=[pltpu.VMEM((bm, D_MODEL), BF16)],
    )
    return pl.pallas_call(
        _experts_kernel,
        grid_spec=grid_spec,
        out_shape=jax.ShapeDtypeStruct((p * ROW_SUB, LANES), jnp.uint32),
        compiler_params=_cparams(("arbitrary",)),
        name="experts",
    )(*meta, xs, wg, wu, wd)


def _expert_work_items(counts, p, bm):
    ntiles = p // bm
    nw = ntiles + N_EXPERTS
    ends = jnp.cumsum(counts)
    starts = ends - counts
    ft = starts // bm
    nt = jnp.where(counts > 0, (ends - 1) // bm - ft + 1, 0)
    wend = jnp.cumsum(nt)
    wstart = wend - nt
    w = jnp.arange(nw, dtype=jnp.int32)
    used = w < wend[-1]
    e = jnp.minimum(jnp.sum((w[:, None] >= wend[None, :]).astype(jnp.int32), axis=1), N_EXPERTS - 1)
    e_last = jnp.max(jnp.where(counts > 0, jnp.arange(N_EXPERTS, dtype=jnp.int32), 0))
    e = jnp.where(used, e, e_last)
    tile = jnp.where(used, ft[e] + (w - wstart[e]), ntiles - 1).astype(jnp.int32)
    lo = jnp.where(used, jnp.maximum(starts[e], tile * bm) - tile * bm, 0).astype(jnp.int32)
    hi = jnp.where(used, jnp.minimum(ends[e], (tile + 1) * bm) - tile * bm, 0).astype(jnp.int32)
    first = jnp.concatenate([jnp.ones((1,), jnp.int32), (tile[1:] != tile[:-1]).astype(jnp.int32)])
    return (tile, e, lo, hi, first), starts


def _rope_tables(T):
    half = ROT_DIM // 2
    inv = 1.0 / (ROPE_THETA ** (np.arange(half, dtype=np.float64) * (2.0 / ROT_DIM)))
    ang = np.arange(T, dtype=np.float64)[:, None] * inv[None, :]
    cos, sin = np.cos(ang), np.sin(ang)
    rest = HEAD_DIM - ROT_DIM
    z_half, z_rest = np.zeros((T, half)), np.zeros((T, rest))
    cos_t = np.concatenate([cos, cos, np.ones((T, rest))], axis=1)
    sin_t = np.concatenate([-sin, sin, z_rest], axis=1)
    return tuple(jnp.asarray(t, dtype=F32) for t in (cos_t, sin_t))


def _layer(x, p):
    B, T, D = x.shape
    n = B * T
    assert D == D_MODEL and T % NA_TQ == 0 and T // GRID_W >= NA_ROWS + NA_HALO_ROWS
    x2 = x.reshape(n, D)
    tm_in = 1024
    tm_out = 256
    tm_route = 512
    tm_move = 512
    bm = 256

    main, qkv1, qkv2, sig = _inproj(x2, p["g1"], p["w_in"], p["gains"], *_rope_tables(T), B, T, tm_in)
    main3 = main.reshape(B, T, JB_G1 * COL_BLK)
    att_a = _natten(main3, p["na_bias"], B, T).reshape(n, W_NA)
    dil = [_dilattn(main3[:, None], JB_G0, B, T, 0), _dilattn(qkv1, 0, B, T, 1), _dilattn(qkv2, 0, B, T, 2)]
    x1, logits = _outproj(att_a, [d[0] for d in dil], [d[1] for d in dil], sig, x2,
                          p["wa"], p["wb"], p["wo"], p["g2"], p["wr"], tm_out)

    gates, eid, rank, cnt = _route(logits, p["rbias"], tm_route)
    counts = cnt[0, ROUTE_E0:ROUTE_E0 + N_EXPERTS].astype(jnp.int32)
    meta, starts = _expert_work_items(counts, n * TOP_K, bm)
    is_e = eid[:, :TOP_K, None] == jnp.arange(N_EXPERTS, dtype=jnp.int32)
    pos = jnp.sum(jnp.where(is_e, starts, 0), axis=-1) + rank[:, :TOP_K]
    pos3 = pos.astype(jnp.int32).reshape(n // tm_move, 1, TOP_K * tm_move)

    xs = _dispatch(pos3, x1, p["g2"], tm_move)
    ys = _experts(meta, xs, p["wg"], p["wu"], p["wd"], bm)
    y = _combine(pos3, gates, x1, ys, tm_move)
    return y.reshape(B, T, D)


def _prepare_params(norm1_g, w_in, qn_a, kn_a, rpb_a, qn_b, kn_b, w_branch_a, w_branch_b, w_out,
                    norm2_g, router_group_w, router_group_b, router_expert_w, router_expert_b, w_gate, w_up, w_down):
    assert norm1_g.shape[0] == 1
    l = 0
    pad_r = LANES - N_GROUPS - N_EXPERTS
    zrow = jnp.zeros((4, HEAD_DIM), F32)
    col = lambda start, width: w_in[l][:, start:start + width]
    qb0, kb0, vb0 = 3 * W_NA, 3 * W_NA + W_DIL, 3 * W_NA + 2 * W_DIL
    pieces = [col(0, 3 * W_NA)]
    for g in range(len(DIL_CONFIGS)):
        pieces += [col(qb0 + g * W_GRP, W_GRP), col(kb0 + g * W_GRP, W_GRP), col(vb0 + g * W_GRP, W_GRP)]
    pieces.append(col(3 * W_NA + 3 * W_DIL, 2 * D_MODEL))
    return {
        "g1": norm1_g[l][None, :],
        "w_in": jnp.concatenate(pieces, axis=1).astype(BF16),
        "gains": jnp.concatenate([qn_a[l][None], kn_a[l][None], qn_b[l][None], kn_b[l][None], zrow], axis=0),
        "na_bias": _na_bias_table(rpb_a[l]),
        "wa": w_branch_a[l].astype(BF16),
        "wb": w_branch_b[l].astype(BF16),
        "wo": w_out[l].astype(BF16),
        "g2": norm2_g[l][None, :],
        "wr": jnp.concatenate([router_group_w[l], router_expert_w[l], jnp.zeros((D_MODEL, pad_r), F32)],
                              axis=1).astype(BF16),
        "rbias": jnp.concatenate([router_group_b[l], router_expert_b[l], jnp.zeros((pad_r,), F32)])[None, :],
        "wg": w_gate[l].astype(BF16),
        "wu": w_up[l].astype(BF16),
        "wd": w_down[l].astype(BF16),
    }


def kernel(x_prompt, x_sample, norm1_g, w_in, qn_a, kn_a, rpb_a, qn_b, kn_b, w_branch_a, w_branch_b, w_out,
           norm2_g, router_group_w, router_group_b, router_expert_w, router_expert_b, w_gate, w_up, w_down):
    p = _prepare_params(norm1_g, w_in, qn_a, kn_a, rpb_a, qn_b, kn_b, w_branch_a, w_branch_b, w_out, norm2_g,
                        router_group_w, router_group_b, router_expert_w, router_expert_b, w_gate, w_up, w_down)
    return (_layer(x_prompt, p), _layer(x_sample, p))
```

```python
import functools

import jax
import jax.numpy as jnp
import numpy as np
from jax import lax
from jax.experimental import pallas as pl
from jax.experimental.pallas import tpu as pltpu

F32 = jnp.float32
BF16 = jnp.bfloat16

D_MODEL = 2048
HEAD_DIM = 128
N_HEADS_NA = 4
N_HEADS_PER_DIL = 4
DIL_CONFIGS = ((128, 1), (512, 4), (2048, 16))
W_NA = N_HEADS_NA * HEAD_DIM
W_DIL = N_HEADS_PER_DIL * len(DIL_CONFIGS) * HEAD_DIM
W_GRP = N_HEADS_PER_DIL * HEAD_DIM
GRID_W = 64
NA_ROWS = 8
NA_COLS = 16
ROT_DIM = HEAD_DIM // 4
ROPE_THETA = 500000.0
BAND_BLOCK = 128
N_GROUPS = 4
EXPERTS_PER_GROUP = 8
N_EXPERTS = N_GROUPS * EXPERTS_PER_GROUP
D_EXPERT = 1024
EPS = 1e-6
NEG = -1e30
IN_WIDTH = 3 * W_NA + 3 * W_DIL + 2 * D_MODEL
SCALE = HEAD_DIM ** -0.5

COL_BLK = 512
JB_G0, JB_G1, JB_G2, JB_GATE = 3, 6, 9, 12
N_COL_BLK = IN_WIDTH // COL_BLK

LANES = 128
VMEM_LIMIT = 56 * 1024 * 1024


def _cparams(sem, vmem=VMEM_LIMIT):
    return pltpu.CompilerParams(dimension_semantics=sem, vmem_limit_bytes=vmem)


def _inproj_kernel(x_ref, g1_ref, w_ref, gains_ref, cos_ref, sin_ref, ones_ref, perm_ref,
                   main_ref, d1_ref, d2_ref, gate_ref, h_ref, y_ref):
    j = pl.program_id(1)
    nh = COL_BLK // HEAD_DIM

    @pl.when(j == 0)
    def _():
        x = x_ref[...]
        ms = jnp.mean(x * x, axis=-1, keepdims=True)
        h_ref[...] = (x * lax.rsqrt(ms + EPS) * g1_ref[...]).astype(BF16)

    acc = jnp.dot(h_ref[...], w_ref[...], preferred_element_type=F32)
    kind = j % 3
    is_qk = (j < JB_GATE) & (kind != 2)
    gain_row = kind + jnp.where(j < JB_G0, 0, 2)

    def head_norm(hh, gain):
        xh = acc[:, hh * HEAD_DIM:(hh + 1) * HEAD_DIM]
        ss = jnp.dot((xh * xh).astype(BF16), ones_ref[...], preferred_element_type=F32)
        return xh * lax.rsqrt(ss * (1.0 / HEAD_DIM) + EPS) * gain

    def head_norm_rope(hh, gain, cos, sin):
        y = head_norm(hh, gain)
        return y * cos + jnp.dot(y.astype(BF16), perm_ref[...], preferred_element_type=F32) * sin

    @pl.when(is_qk & (j < JB_G0))
    def _():
        gain = gains_ref[pl.ds(gain_row, 1), :]
        for hh in range(nh):
            main_ref[:, hh * HEAD_DIM:(hh + 1) * HEAD_DIM] = head_norm(hh, gain).astype(BF16)

    @pl.when(is_qk & (j >= JB_G0) & (j < JB_G1))
    def _():
        gain = gains_ref[pl.ds(gain_row, 1), :]
        cos, sin = cos_ref[...], sin_ref[...]
        for hh in range(nh):
            main_ref[:, hh * HEAD_DIM:(hh + 1) * HEAD_DIM] = head_norm_rope(hh, gain, cos, sin).astype(BF16)

    @pl.when((kind == 2) & (j < JB_G1))
    def _():
        main_ref[...] = acc.astype(BF16)

    @pl.when(is_qk & (j >= JB_G1))
    def _():
        gain = gains_ref[pl.ds(gain_row, 1), :]
        cos, sin = cos_ref[...], sin_ref[...]
        for hh in range(nh):
            y_ref[hh] = head_norm_rope(hh, gain, cos, sin)

    @pl.when((kind == 2) & (j >= JB_G1) & (j < JB_GATE))
    def _():
        for hh in range(nh):
            y_ref[hh] = acc[:, hh * HEAD_DIM:(hh + 1) * HEAD_DIM]

    def deinterleave(dst_ref, dil):
        rows = y_ref.shape[1] // dil
        for ph in range(dil):
            for hh in range(nh):
                dst_ref[ph, :, hh * HEAD_DIM:(hh + 1) * HEAD_DIM] = (
                    y_ref[hh, pl.ds(ph, rows, stride=dil), :].astype(BF16))

    @pl.when((j >= JB_G1) & (j < JB_G2))
    def _():
        deinterleave(d1_ref, DIL_CONFIGS[1][1])

    @pl.when((j >= JB_G2) & (j < JB_GATE))
    def _():
        deinterleave(d2_ref, DIL_CONFIGS[2][1])

    @pl.when(j >= JB_GATE)
    def _():
        gate_ref[...] = (0.5 * jnp.tanh(0.5 * acc) + 0.5).astype(BF16)


def _inproj(x2, g1, w_in_bf, gains, cos_t, sin_t, B, T, tm):
    n = x2.shape[0]
    tpos = T // tm
    d1, d2 = DIL_CONFIGS[1][1], DIL_CONFIGS[2][1]
    tab = pl.BlockSpec((tm, HEAD_DIM), lambda i, j: (i % tpos, 0))
    sq = pl.BlockSpec((HEAD_DIM, HEAD_DIM), lambda i, j: (0, 0))
    half = ROT_DIM // 2
    lane = np.arange(HEAD_DIM)
    partner = np.where(lane < half, lane + half, lane - half)
    perm = ((lane[:, None] == partner[None, :]) & (lane[None, :] < ROT_DIM)).astype(np.float32)
    ones = np.ones((HEAD_DIM, HEAD_DIM), np.float32)
    return pl.pallas_call(
        _inproj_kernel,
        grid=(n // tm, N_COL_BLK),
        in_specs=[
            pl.BlockSpec((tm, D_MODEL), lambda i, j: (i, 0)),
            pl.BlockSpec((1, D_MODEL), lambda i, j: (0, 0)),
            pl.BlockSpec((D_MODEL, COL_BLK), lambda i, j: (0, j)),
            pl.BlockSpec((8, HEAD_DIM), lambda i, j: (0, 0)),
            tab, tab, sq, sq,
        ],
        out_specs=[
            pl.BlockSpec((tm, COL_BLK), lambda i, j: (i, jnp.minimum(j, JB_G1 - 1))),
            pl.BlockSpec((None, d1, tm // d1, COL_BLK),
                         lambda i, j: (i // tpos, 0, i % tpos, jnp.clip(j - JB_G1, 0, 2))),
            pl.BlockSpec((None, d2, tm // d2, COL_BLK),
                         lambda i, j: (i // tpos, 0, i % tpos, jnp.clip(j - JB_G2, 0, 2))),
            pl.BlockSpec((tm, COL_BLK), lambda i, j: (i, jnp.clip(j - JB_GATE, 0, N_COL_BLK - JB_GATE - 1))),
        ],
        out_shape=[
            jax.ShapeDtypeStruct((n, JB_G1 * COL_BLK), BF16),
            jax.ShapeDtypeStruct((B, d1, T // d1, 3 * COL_BLK), BF16),
            jax.ShapeDtypeStruct((B, d2, T // d2, 3 * COL_BLK), BF16),
            jax.ShapeDtypeStruct((n, 2 * D_MODEL), BF16),
        ],
        scratch_shapes=[pltpu.VMEM((tm, D_MODEL), BF16), pltpu.VMEM((COL_BLK // HEAD_DIM, tm, HEAD_DIM), F32)],
        compiler_params=_cparams(("arbitrary", "arbitrary")),
        name="inproj",
    )(x2, g1, w_in_bf, gains, cos_t, sin_t, jnp.asarray(ones, BF16), jnp.asarray(perm, BF16))


NA_CHUNK_ROWS = 16
NA_HALO_ROWS = 4
NA_TQ = NA_CHUNK_ROWS * GRID_W
NA_TH = NA_HALO_ROWS * GRID_W
NA_GROUP = 4
NA_SPAN_ROWS = 12
NA_GQ = NA_GROUP * GRID_W
NA_SPAN = NA_SPAN_ROWS * GRID_W
NA_CASES = 3
assert NA_CHUNK_ROWS % NA_GROUP == 0 and NA_HALO_ROWS == NA_ROWS // 2
assert NA_ROWS + NA_GROUP - 1 <= NA_SPAN_ROWS <= NA_GROUP + 2 * NA_HALO_ROWS


def _natten_kernel(q_ref, kp_ref, km_ref, kn_ref, vp_ref, vm_ref, vn_ref, bias_ref, o_ref, kw_ref, vw_ref, *, rows):
    i = pl.program_id(1)
    kw_ref[0:NA_TH] = kp_ref[...]
    kw_ref[NA_TH:NA_TH + NA_TQ] = km_ref[...]
    kw_ref[NA_TH + NA_TQ:] = kn_ref[...]
    vw_ref[0:NA_TH] = vp_ref[...]
    vw_ref[NA_TH:NA_TH + NA_TQ] = vm_ref[...]
    vw_ref[NA_TH + NA_TQ:] = vn_ref[...]

    for gg in range(NA_CHUNK_ROWS // NA_GROUP):
        r0 = i * NA_CHUNK_ROWS + gg * NA_GROUP
        rs0 = jnp.clip(r0 - NA_ROWS // 2, 0, rows - NA_ROWS)
        case = jnp.where(r0 == 0, 0, jnp.where(r0 == rows - NA_GROUP, 2, 1))
        off = pl.multiple_of((rs0 - (i * NA_CHUNK_ROWS - NA_HALO_ROWS)) * GRID_W, GRID_W)
        for h in range(N_HEADS_NA):
            cols = slice(h * HEAD_DIM, (h + 1) * HEAD_DIM)
            q = q_ref[gg * NA_GQ:(gg + 1) * NA_GQ, cols]
            k = kw_ref[pl.ds(off, NA_SPAN), cols]
            v = vw_ref[pl.ds(off, NA_SPAN), cols]
            s = lax.dot_general(q, k, (((1,), (1,)), ((), ())), preferred_element_type=F32)
            s = s * SCALE + bias_ref[case, h]
            m = jnp.max(s, axis=-1, keepdims=True)
            p = jnp.exp(s - m)
            l = jnp.sum(p, axis=-1, keepdims=True)
            o = jnp.dot(p.astype(BF16), v, preferred_element_type=F32) / l
            o_ref[gg * NA_GQ:(gg + 1) * NA_GQ, cols] = o.astype(BF16)


def _natten(proj3, bias_tab, B, T):
    rows = T // GRID_W
    nh = T // NA_TH
    ratio = NA_TQ // NA_TH
    blk = lambda rws, fn: pl.BlockSpec((None, rws, COL_BLK), fn)
    return pl.pallas_call(
        functools.partial(_natten_kernel, rows=rows),
        grid=(B, T // NA_TQ),
        in_specs=[
            blk(NA_TQ, lambda b, i: (b, i, 0)),
            blk(NA_TH, lambda b, i: (b, jnp.maximum(i * ratio - 1, 0), 1)),
            blk(NA_TQ, lambda b, i: (b, i, 1)),
            blk(NA_TH, lambda b, i: (b, jnp.minimum((i + 1) * ratio, nh - 1), 1)),
            blk(NA_TH, lambda b, i: (b, jnp.maximum(i * ratio - 1, 0), 2)),
            blk(NA_TQ, lambda b, i: (b, i, 2)),
            blk(NA_TH, lambda b, i: (b, jnp.minimum((i + 1) * ratio, nh - 1), 2)),
            pl.BlockSpec((NA_CASES, N_HEADS_NA, NA_GQ, NA_SPAN), lambda b, i: (0, 0, 0, 0)),
        ],
        out_specs=pl.BlockSpec((None, NA_TQ, W_NA), lambda b, i: (b, i, 0)),
        out_shape=jax.ShapeDtypeStruct((B, T, W_NA), BF16),
        scratch_shapes=[pltpu.VMEM((NA_TQ + 2 * NA_TH, COL_BLK), BF16)] * 2,
        compiler_params=_cparams(("arbitrary", "arbitrary")),
        name="natten",
    )(proj3, proj3, proj3, proj3, proj3, proj3, proj3, bias_tab)


def _na_bias_table(rpb):
    c = np.arange(GRID_W)
    cs = np.clip(c - NA_COLS // 2, 0, GRID_W - NA_COLS)
    cc = np.arange(GRID_W)
    valid = (cc[None, :] >= cs[:, None]) & (cc[None, :] < cs[:, None] + NA_COLS)
    dc_idx = np.clip(cc[None, :] - c[:, None] + NA_COLS - 1, 0, 2 * NA_COLS - 2)
    pick = (dc_idx[..., None] == np.arange(2 * NA_COLS - 1)).astype(np.float32)
    t = jnp.einsum("qkd,hrd->hrqk", pick, rpb.astype(F32), precision=lax.Precision.HIGHEST)
    t = jnp.where(valid[None, None], t, NEG)
    half = NA_ROWS // 2
    cases = ([(m, 0) for m in range(NA_GROUP)],
             [(half, m) for m in range(NA_GROUP)],
             [(half + m, 0) for m in range(NA_GROUP)])
    tiles = []
    for case in cases:
        per_row = []
        for var, woff in case:
            win = t[:, NA_ROWS - 1 - var:2 * NA_ROWS - 1 - var]
            pad = ((0, 0), (woff, NA_SPAN_ROWS - NA_ROWS - woff), (0, 0), (0, 0))
            per_row.append(jnp.pad(win, pad, constant_values=NEG))
        tile = jnp.stack(per_row, axis=1)
        tiles.append(tile.transpose(0, 1, 3, 2, 4).reshape(N_HEADS_NA, NA_GQ, NA_SPAN))
    return jnp.stack(tiles, axis=0)


DIL_HALF = 64
DIL_KB = BAND_BLOCK + 2 * DIL_HALF
DIL_TOKENS_PER_STEP = (1024, 1024, 2048)


def _dilattn_kernel(q_ref, kp_ref, km_ref, kn_ref, vp_ref, vm_ref, vn_ref, o_ref, lse_ref,
                    kw_ref, vw_ref, os_ref, ls_ref, *, L, tq, dil):
    i = pl.program_id(1)
    kw_ref[:, 0:DIL_HALF] = kp_ref[...]
    kw_ref[:, DIL_HALF:DIL_HALF + tq] = km_ref[...]
    kw_ref[:, DIL_HALF + tq:] = kn_ref[...]
    vw_ref[:, 0:DIL_HALF] = vp_ref[...]
    vw_ref[:, DIL_HALF:DIL_HALF + tq] = vm_ref[...]
    vw_ref[:, DIL_HALF + tq:] = vn_ref[...]

    def phase(j):
        r_io = lax.broadcasted_iota(jnp.int32, (BAND_BLOCK, DIL_KB), 0)
        c_io = lax.broadcasted_iota(jnp.int32, (BAND_BLOCK, DIL_KB), 1)
        lane = lax.broadcasted_iota(jnp.int32, (BAND_BLOCK, LANES), 1)
        for s in range(tq // BAND_BLOCK):
            base = i * tq + s * BAND_BLOCK - DIL_HALF
            lo = jnp.maximum(r_io, -base)
            hi = jnp.minimum(r_io + 2 * DIL_HALF, L - 1 - base)
            valid = (c_io >= lo) & (c_io <= hi)
            out_rows = pl.ds(s * BAND_BLOCK * dil + j, BAND_BLOCK, stride=dil)
            lse_tile = jnp.zeros((BAND_BLOCK, LANES), F32)
            for h in range(N_HEADS_PER_DIL):
                cols = slice(h * HEAD_DIM, (h + 1) * HEAD_DIM)
                q = q_ref[j, s * BAND_BLOCK:(s + 1) * BAND_BLOCK, cols]
                k = kw_ref[j, s * BAND_BLOCK:s * BAND_BLOCK + DIL_KB, cols]
                v = vw_ref[j, s * BAND_BLOCK:s * BAND_BLOCK + DIL_KB, cols]
                sc = lax.dot_general(q, k, (((1,), (1,)), ((), ())), preferred_element_type=F32) * SCALE
                sc = jnp.where(valid, sc, NEG)
                m = jnp.max(sc, axis=-1, keepdims=True)
                p = jnp.exp(sc - m)
                l = jnp.sum(p, axis=-1, keepdims=True)
                os_ref[h, out_rows, :] = jnp.dot(p.astype(BF16), v, preferred_element_type=F32) / l
                lse_tile = jnp.where(lane == h, m + jnp.log(l), lse_tile)
            ls_ref[out_rows, :] = lse_tile

    if dil == 1:
        phase(0)
    else:
        def body(j, c):
            phase(j)
            return c
        lax.fori_loop(0, dil, body, 0, unroll=2)
    for h in range(N_HEADS_PER_DIL):
        o_ref[:, h * HEAD_DIM:(h + 1) * HEAD_DIM] = os_ref[h].astype(BF16)
    lse_ref[...] = ls_ref[...]


def _dilattn(qkv, col0, B, T, g):
    dil = DIL_CONFIGS[g][1]
    assert DIL_CONFIGS[g][0] // 2 // dil == DIL_HALF
    L = T // dil
    tq = min(DIL_TOKENS_PER_STEP[g] // dil, L)
    nh = L // DIL_HALF
    ratio = tq // DIL_HALF
    blk = lambda rws, fn: pl.BlockSpec((None, dil, rws, COL_BLK), fn)
    prev = lambda i: jnp.maximum(i * ratio - 1, 0)
    nxt = lambda i: jnp.minimum((i + 1) * ratio, nh - 1)
    o, lse = pl.pallas_call(
        functools.partial(_dilattn_kernel, L=L, tq=tq, dil=dil),
        grid=(B, L // tq),
        in_specs=[
            blk(tq, lambda b, i: (b, 0, i, col0)),
            blk(DIL_HALF, lambda b, i: (b, 0, prev(i), col0 + 1)),
            blk(tq, lambda b, i: (b, 0, i, col0 + 1)),
            blk(DIL_HALF, lambda b, i: (b, 0, nxt(i), col0 + 1)),
            blk(DIL_HALF, lambda b, i: (b, 0, prev(i), col0 + 2)),
            blk(tq, lambda b, i: (b, 0, i, col0 + 2)),
            blk(DIL_HALF, lambda b, i: (b, 0, nxt(i), col0 + 2)),
        ],
        out_specs=[
            pl.BlockSpec((None, tq * dil, W_GRP), lambda b, i: (b, i, 0)),
            pl.BlockSpec((None, tq * dil, LANES), lambda b, i: (b, i, 0)),
        ],
        out_shape=[
            jax.ShapeDtypeStruct((B, T, W_GRP), BF16),
            jax.ShapeDtypeStruct((B, T, LANES), F32),
        ],
        scratch_shapes=[pltpu.VMEM((dil, tq + 2 * DIL_HALF, COL_BLK), BF16)] * 2
        + [pltpu.VMEM((N_HEADS_PER_DIL, tq * dil, HEAD_DIM), F32), pltpu.VMEM((tq * dil, LANES), F32)],
        compiler_params=_cparams(("arbitrary", "arbitrary")),
        name=f"dilattn{g}",
    )(qkv, qkv, qkv, qkv, qkv, qkv, qkv)
    return o.reshape(B * T, W_GRP), lse.reshape(B * T, LANES)


def _outproj_kernel(a_ref, o0_ref, o1_ref, o2_ref, l0_ref, l1_ref, l2_ref, ga_ref, gb_ref, x_ref,
                    wa_ref, wb_ref, wo_ref, g2_ref, wr_ref, x1_ref, lg_ref):
    o_refs = (o0_ref, o1_ref, o2_ref)
    lses = [r[...] for r in (l0_ref, l1_ref, l2_ref)]
    parts = []
    for h in range(N_HEADS_PER_DIL):
        cols = slice(h * HEAD_DIM, (h + 1) * HEAD_DIM)
        lh = [l[:, h:h + 1] for l in lses]
        mx = jnp.maximum(jnp.maximum(lh[0], lh[1]), lh[2])
        e = [jnp.exp(v - mx) for v in lh]
        den = e[0] + e[1] + e[2]
        acc = None
        for g in range(len(DIL_CONFIGS)):
            t = (e[g] / den) * o_refs[g][:, cols].astype(F32)
            acc = t if acc is None else acc + t
        parts.append(acc.astype(BF16))
    att_b = jnp.concatenate(parts, axis=-1)
    ya = jnp.dot(a_ref[...], wa_ref[...], preferred_element_type=F32)
    yb = jnp.dot(att_b, wb_ref[...], preferred_element_type=F32)
    mix = ga_ref[...].astype(F32) * ya + gb_ref[...].astype(F32) * yb
    x1 = x_ref[...] + jnp.dot(mix.astype(BF16), wo_ref[...], preferred_element_type=F32)
    x1_ref[...] = x1
    ms = jnp.mean(x1 * x1, axis=-1, keepdims=True)
    h2 = x1 * lax.rsqrt(ms + EPS) * g2_ref[...]
    lg_ref[...] = jnp.dot(h2.astype(BF16), wr_ref[...], preferred_element_type=F32)


def _outproj(att_a, os_, lses, gates, x2, wa, wb, wo, g2, wr, tm):
    n = x2.shape[0]
    row = lambda w: pl.BlockSpec((tm, w), lambda i: (i, 0))
    full = lambda a: pl.BlockSpec(a.shape, lambda i: (0,) * a.ndim)
    return pl.pallas_call(
        _outproj_kernel,
        grid=(n // tm,),
        in_specs=[row(W_NA), row(W_GRP), row(W_GRP), row(W_GRP), row(LANES), row(LANES), row(LANES),
                  pl.BlockSpec((tm, D_MODEL), lambda i: (i, 0)),
                  pl.BlockSpec((tm, D_MODEL), lambda i: (i, 1)),
                  row(D_MODEL), full(wa), full(wb), full(wo), full(g2), full(wr)],
        out_specs=[row(D_MODEL), row(LANES)],
        out_shape=[jax.ShapeDtypeStruct((n, D_MODEL), F32), jax.ShapeDtypeStruct((n, LANES), F32)],
        compiler_params=_cparams(("arbitrary",)),
        name="outproj",
    )(att_a, *os_, *lses, gates, gates, x2, wa, wb, wo, g2, wr)


ROUTE_E0 = N_GROUPS


def _route_kernel(lg_ref, b_ref, gate_ref, eid_ref, rank_ref, cnt_ref, carry_ref):
    i = pl.program_id(0)

    @pl.when(i == 0)
    def _():
        carry_ref[...] = jnp.zeros_like(carry_ref)

    z = lg_ref[...] + b_ref[...]
    tm = z.shape[0]
    lane = lax.broadcasted_iota(jnp.int32, z.shape, 1)
    lane_f = lane.astype(F32)
    rmax = lambda v: jnp.max(v, axis=-1, keepdims=True)
    rmin = lambda v: jnp.min(v, axis=-1, keepdims=True)
    rsum = lambda v: jnp.sum(v, axis=-1, keepdims=True)

    gmask = lane < N_GROUPS
    gmax = rmax(jnp.where(gmask, z, -jnp.inf))
    gsel = rmin(jnp.where(gmask & (z == gmax), lane_f, float(LANES)))
    gp = 1.0 / rsum(jnp.where(gmask, jnp.exp(z - gmax), 0.0))

    elo = float(ROUTE_E0) + float(EXPERTS_PER_GROUP) * gsel
    emask = (lane_f >= elo) & (lane_f < elo + float(EXPERTS_PER_GROUP))
    emax = rmax(jnp.where(emask, z, -jnp.inf))
    ee = jnp.where(emask, jnp.exp(z - emax), 0.0)
    eprob = ee / rsum(ee)
    p1 = rmax(jnp.where(emask, eprob, -1.0))
    i1 = rmin(jnp.where(emask & (eprob == p1), lane_f, float(LANES)))
    m2 = emask & (lane_f != i1)
    p2 = rmax(jnp.where(m2, eprob, -1.0))
    i2 = rmin(jnp.where(m2 & (eprob == p2), lane_f, float(LANES)))
    den = p1 + p2
    g1 = gp * p1 / den
    g2 = gp * p2 / den

    oh1 = (lane_f == i1).astype(F32)
    oh2 = (lane_f == i2).astype(F32)
    oh = oh1 + oh2
    tri = (lax.broadcasted_iota(jnp.int32, (tm, tm), 1) < lax.broadcasted_iota(jnp.int32, (tm, tm), 0)).astype(BF16)
    tot = carry_ref[...] + jnp.dot(tri, oh.astype(BF16), preferred_element_type=F32)
    r1 = rsum(oh1 * tot)
    r2 = rsum(oh2 * tot)
    carry = carry_ref[...] + jnp.sum(oh, axis=0, keepdims=True)
    carry_ref[...] = carry
    cnt_ref[...] = carry

    two = lambda a, b: jnp.where(lane == 0, a, jnp.where(lane == 1, b, 0.0))
    gate_ref[...] = two(g1, g2)
    eid_ref[...] = two(i1 - float(ROUTE_E0), i2 - float(ROUTE_E0)).astype(jnp.int32)
    rank_ref[...] = two(r1, r2).astype(jnp.int32)


def _route(logits, rbias, tm):
    n = logits.shape[0]
    row = pl.BlockSpec((tm, LANES), lambda i: (i, 0))
    one = pl.BlockSpec((1, LANES), lambda i: (0, 0))
    return pl.pallas_call(
        _route_kernel,
        grid=(n // tm,),
        in_specs=[row, one],
        out_specs=[row, row, row, one],
        out_shape=[jax.ShapeDtypeStruct((n, LANES), F32), jax.ShapeDtypeStruct((n, LANES), jnp.int32),
                   jax.ShapeDtypeStruct((n, LANES), jnp.int32), jax.ShapeDtypeStruct((1, LANES), F32)],
        scratch_shapes=[pltpu.VMEM((1, LANES), F32)],
        compiler_params=_cparams(("arbitrary",)),
        name="route",
    )(logits, rbias)


TOP_K = 2


ROW_SUB = 8
HALF_D = D_MODEL // 2
assert ROW_SUB * LANES == HALF_D


def _pack_pair(lo, hi):
    lo_b = lax.bitcast_convert_type(lo.astype(BF16).astype(F32), jnp.uint32)
    hi_b = lax.bitcast_convert_type(hi.astype(BF16).astype(F32), jnp.uint32)
    return (lo_b >> 16) | hi_b


def _unpack_pair(w):
    lo = lax.bitcast_convert_type(w << 16, F32)
    hi = lax.bitcast_convert_type(w & jnp.uint32(0xFFFF0000), F32)
    return lo, hi


def _store_packed_rows(dst, x):
    for c in range(ROW_SUB):
        dst(c, _pack_pair(x[:, c * LANES:(c + 1) * LANES], x[:, HALF_D + c * LANES:HALF_D + (c + 1) * LANES]))


def _dispatch_kernel(pos_ref, x1_ref, g2_ref, xs_ref, hp_ref, sem):
    i = pl.program_id(0)
    last = pl.num_programs(0) - 1
    slot = i % 2
    x1 = x1_ref[...]
    tm = x1.shape[0]
    ms = jnp.mean(x1 * x1, axis=-1, keepdims=True)
    h = x1 * lax.rsqrt(ms + EPS) * g2_ref[...]

    def put(c, words):
        hp_ref[slot, pl.ds(c, tm, stride=ROW_SUB), :] = words

    _store_packed_rows(put, h)

    def row_copy(sl, r, p):
        src = hp_ref.at[sl, pl.ds(pl.multiple_of(r * ROW_SUB, ROW_SUB), ROW_SUB)]
        dst = xs_ref.at[pl.ds(pl.multiple_of(p * ROW_SUB, ROW_SUB), ROW_SUB)]
        return pltpu.make_async_copy(src, dst, sem.at[sl])

    def start(r, c):
        for k in range(TOP_K):
            row_copy(slot, r, pos_ref[0, TOP_K * r + k]).start(priority=k)
        return c

    def wait_all(sl):
        def wait(r, c):
            for k in range(TOP_K):
                row_copy(sl, 0, 0).wait()
            return c
        lax.fori_loop(0, tm, wait, 0)

    lax.fori_loop(0, tm, start, 0, unroll=4)

    @pl.when(i > 0)
    def _():
        wait_all(1 - slot)

    @pl.when(i == last)
    def _():
        wait_all(slot)


def _dispatch(pos3, x1, g2, tm):
    n = x1.shape[0]
    return pl.pallas_call(
        _dispatch_kernel,
        grid=(n // tm,),
        in_specs=[pl.BlockSpec((None, 1, TOP_K * tm), lambda i: (i, 0, 0), memory_space=pltpu.SMEM),
                  pl.BlockSpec((tm, D_MODEL), lambda i: (i, 0)),
                  pl.BlockSpec((1, D_MODEL), lambda i: (0, 0))],
        out_specs=pl.BlockSpec(memory_space=pl.ANY),
        out_shape=jax.ShapeDtypeStruct((n * TOP_K * ROW_SUB, LANES), jnp.uint32),
        scratch_shapes=[pltpu.VMEM((2, tm * ROW_SUB, LANES), jnp.uint32), pltpu.SemaphoreType.DMA((2,))],
        compiler_params=_cparams(("arbitrary",)),
        name="dispatch",
    )(pos3, x1, g2)


def _combine_kernel(pos_ref, posn_ref, gate_ref, x1_ref, ys_ref, y_ref, buf_ref, sem):
    i = pl.program_id(0)
    last = pl.num_programs(0) - 1
    slot = i % 2
    tm = x1_ref.shape[0]

    def row_copy(sl, k, r, p):
        src = ys_ref.at[pl.ds(pl.multiple_of(p * ROW_SUB, ROW_SUB), ROW_SUB)]
        dst = buf_ref.at[sl, k, pl.ds(pl.multiple_of(r * ROW_SUB, ROW_SUB), ROW_SUB)]
        return pltpu.make_async_copy(src, dst, sem.at[sl])

    def start_all(sl, p_ref):
        def start(r, c):
            for k in range(TOP_K):
                row_copy(sl, k, r, p_ref[0, TOP_K * r + k]).start(priority=k)
            return c
        lax.fori_loop(0, tm, start, 0, unroll=4)

    @pl.when(i == 0)
    def _():
        start_all(0, pos_ref)

    @pl.when(i < last)
    def _():
        start_all(1 - slot, posn_ref)

    def wait(r, c):
        for k in range(TOP_K):
            row_copy(slot, k, 0, 0).wait()
        return c

    lax.fori_loop(0, tm, wait, 0)
    gates = gate_ref[...]
    g0 = jnp.broadcast_to(gates[:, 0:1], (tm, LANES))
    g1 = jnp.broadcast_to(gates[:, 1:2], (tm, LANES))
    for c in range(ROW_SUB):
        lo0, hi0 = _unpack_pair(buf_ref[slot, 0, pl.ds(c, tm, stride=ROW_SUB), :])
        lo1, hi1 = _unpack_pair(buf_ref[slot, 1, pl.ds(c, tm, stride=ROW_SUB), :])
        cl = slice(c * LANES, (c + 1) * LANES)
        ch = slice(HALF_D + c * LANES, HALF_D + (c + 1) * LANES)
        y_ref[:, cl] = x1_ref[:, cl] + (g0 * lo0 + g1 * lo1)
        y_ref[:, ch] = x1_ref[:, ch] + (g0 * hi0 + g1 * hi1)


def _combine(pos3, gates, x1, ys, tm):
    n = x1.shape[0]
    nt = n // tm
    pos_blk = lambda fn: pl.BlockSpec((None, 1, TOP_K * tm), fn, memory_space=pltpu.SMEM)
    return pl.pallas_call(
        _combine_kernel,
        grid=(nt,),
        in_specs=[pos_blk(lambda i: (i, 0, 0)),
                  pos_blk(lambda i: (jnp.minimum(i + 1, nt - 1), 0, 0)),
                  pl.BlockSpec((tm, LANES), lambda i: (i, 0)),
                  pl.BlockSpec((tm, D_MODEL), lambda i: (i, 0)),
                  pl.BlockSpec(memory_space=pl.ANY)],
        out_specs=pl.BlockSpec((tm, D_MODEL), lambda i: (i, 0)),
        out_shape=jax.ShapeDtypeStruct((n, D_MODEL), F32),
        scratch_shapes=[pltpu.VMEM((2, TOP_K, tm * ROW_SUB, LANES), jnp.uint32), pltpu.SemaphoreType.DMA((2,))],
        compiler_params=_cparams(("arbitrary",)),
        name="combine",
    )(pos3, pos3, gates, x1, ys)


def _experts_kernel(tile_s, exp_s, lo_s, hi_s, first_s, x_ref, wg_ref, wu_ref, wd_ref, y_ref, xb_ref):
    w = pl.program_id(0)
    lo, hi = lo_s[w], hi_s[w]
    bm = xb_ref.shape[0]

    @pl.when(hi > lo)
    def _():
        for c in range(ROW_SUB):
            xl, xh = _unpack_pair(x_ref[pl.ds(c, bm, stride=ROW_SUB), :])
            xb_ref[:, c * LANES:(c + 1) * LANES] = xl.astype(BF16)
            xb_ref[:, HALF_D + c * LANES:HALF_D + (c + 1) * LANES] = xh.astype(BF16)
        x = xb_ref[...]
        a = jnp.dot(x, wg_ref[...], preferred_element_type=F32)
        u = jnp.dot(x, wu_ref[...], preferred_element_type=F32)
        hm = (a * jax.nn.sigmoid(a) * u).astype(BF16)
        y = jnp.dot(hm, wd_ref[...], preferred_element_type=F32)
        rows = lax.broadcasted_iota(jnp.int32, (bm, LANES), 0)
        mine = (rows >= lo) & (rows < hi)

        @pl.when(first_s[w] == 1)
        def _():
            def put(c, words):
                y_ref[pl.ds(c, bm, stride=ROW_SUB), :] = jnp.where(mine, words, jnp.uint32(0))
            _store_packed_rows(put, y)

        @pl.when(first_s[w] == 0)
        def _():
            def put(c, words):
                rws = pl.ds(c, bm, stride=ROW_SUB)
                y_ref[rws, :] = jnp.where(mine, words, y_ref[rws, :])
            _store_packed_rows(put, y)


def _experts(meta, xs, wg, wu, wd, bm):
    p = xs.shape[0] // ROW_SUB
    nw = p // bm + N_EXPERTS
    grid_spec = pltpu.PrefetchScalarGridSpec(
        num_scalar_prefetch=5,
        grid=(nw,),
        in_specs=[
            pl.BlockSpec((bm * ROW_SUB, LANES), lambda w, t, e, lo, hi, f: (t[w], 0)),
            pl.BlockSpec((None, D_MODEL, D_EXPERT), lambda w, t, e, lo, hi, f: (e[w], 0, 0)),
            pl.BlockSpec((None, D_MODEL, D_EXPERT), lambda w, t, e, lo, hi, f: (e[w], 0, 0)),
            pl.BlockSpec((None, D_EXPERT, D_MODEL), lambda w, t, e, lo, hi, f: (e[w], 0, 0)),
        ],
        out_specs=pl.BlockSpec((bm * ROW_SUB, LANES), lambda w, t, e, lo, hi, f: (t[w], 0)),
        scratch_shapes=[pltpu.VMEM((bm, D_MODEL), BF16)],
    )
    return pl.pallas_call(
        _experts_kernel,
        grid_spec=grid_spec,
        out_shape=jax.ShapeDtypeStruct((p * ROW_SUB, LANES), jnp.uint32),
        compiler_params=_cparams(("arbitrary",)),
        name="experts",
    )(*meta, xs, wg, wu, wd)


def _expert_work_items(counts, p, bm):
    ntiles = p // bm
    nw = ntiles + N_EXPERTS
    ends = jnp.cumsum(counts)
    starts = ends - counts
    ft = starts // bm
    nt = jnp.where(counts > 0, (ends - 1) // bm - ft + 1, 0)
    wend = jnp.cumsum(nt)
    wstart = wend - nt
    w = jnp.arange(nw, dtype=jnp.int32)
    used = w < wend[-1]
    e = jnp.minimum(jnp.sum((w[:, None] >= wend[None, :]).astype(jnp.int32), axis=1), N_EXPERTS - 1)
    e_last = jnp.max(jnp.where(counts > 0, jnp.arange(N_EXPERTS, dtype=jnp.int32), 0))
    e = jnp.where(used, e, e_last)
    tile = jnp.where(used, ft[e] + (w - wstart[e]), ntiles - 1).astype(jnp.int32)
    lo = jnp.where(used, jnp.maximum(starts[e], tile * bm) - tile * bm, 0).astype(jnp.int32)
    hi = jnp.where(used, jnp.minimum(ends[e], (tile + 1) * bm) - tile * bm, 0).astype(jnp.int32)
    first = jnp.concatenate([jnp.ones((1,), jnp.int32), (tile[1:] != tile[:-1]).astype(jnp.int32)])
    return (tile, e, lo, hi, first), starts


def _rope_tables(T):
    half = ROT_DIM // 2
    inv = 1.0 / (ROPE_THETA ** (np.arange(half, dtype=np.float64) * (2.0 / ROT_DIM)))
    ang = np.arange(T, dtype=np.float64)[:, None] * inv[None, :]
    cos, sin = np.cos(ang), np.sin(ang)
    rest = HEAD_DIM - ROT_DIM
    z_half, z_rest = np.zeros((T, half)), np.zeros((T, rest))
    cos_t = np.concatenate([cos, cos, np.ones((T, rest))], axis=1)
    sin_t = np.concatenate([-sin, sin, z_rest], axis=1)
    return tuple(jnp.asarray(t, dtype=F32) for t in (cos_t, sin_t))


def _layer(x, p):
    B, T, D = x.shape
    n = B * T
    assert D == D_MODEL and T % NA_TQ == 0 and T // GRID_W >= NA_ROWS + NA_HALO_ROWS
    x2 = x.reshape(n, D)
    tm_in = 1024
    tm_out = 256
    tm_route = 512
    tm_move = 512
    bm = 256

    main, qkv1, qkv2, sig = _inproj(x2, p["g1"], p["w_in"], p["gains"], *_rope_tables(T), B, T, tm_in)
    main3 = main.reshape(B, T, JB_G1 * COL_BLK)
    att_a = _natten(main3, p["na_bias"], B, T).reshape(n, W_NA)
    dil = [_dilattn(main3[:, None], JB_G0, B, T, 0), _dilattn(qkv1, 0, B, T, 1), _dilattn(qkv2, 0, B, T, 2)]
    x1, logits = _outproj(att_a, [d[0] for d in dil], [d[1] for d in dil], sig, x2,
                          p["wa"], p["wb"], p["wo"], p["g2"], p["wr"], tm_out)

    gates, eid, rank, cnt = _route(logits, p["rbias"], tm_route)
    counts = cnt[0, ROUTE_E0:ROUTE_E0 + N_EXPERTS].astype(jnp.int32)
    meta, starts = _expert_work_items(counts, n * TOP_K, bm)
    is_e = eid[:, :TOP_K, None] == jnp.arange(N_EXPERTS, dtype=jnp.int32)
    pos = jnp.sum(jnp.where(is_e, starts, 0), axis=-1) + rank[:, :TOP_K]
    pos3 = pos.astype(jnp.int32).reshape(n // tm_move, 1, TOP_K * tm_move)

    xs = _dispatch(pos3, x1, p["g2"], tm_move)
    ys = _experts(meta, xs, p["wg"], p["wu"], p["wd"], bm)
    y = _combine(pos3, gates, x1, ys, tm_move)
    return y.reshape(B, T, D)


def _prepare_params(norm1_g, w_in, qn_a, kn_a, rpb_a, qn_b, kn_b, w_branch_a, w_branch_b, w_out,
                    norm2_g, router_group_w, router_group_b, router_expert_w, router_expert_b, w_gate, w_up, w_down):
    assert norm1_g.shape[0] == 1
    l = 0
    pad_r = LANES - N_GROUPS - N_EXPERTS
    zrow = jnp.zeros((4, HEAD_DIM), F32)
    col = lambda start, width: w_in[l][:, start:start + width]
    qb0, kb0, vb0 = 3 * W_NA, 3 * W_NA + W_DIL, 3 * W_NA + 2 * W_DIL
    pieces = [col(0, 3 * W_NA)]
    for g in range(len(DIL_CONFIGS)):
        pieces += [col(qb0 + g * W_GRP, W_GRP), col(kb0 + g * W_GRP, W_GRP), col(vb0 + g * W_GRP, W_GRP)]
    pieces.append(col(3 * W_NA + 3 * W_DIL, 2 * D_MODEL))
    return {
        "g1": norm1_g[l][None, :],
        "w_in": jnp.concatenate(pieces, axis=1).astype(BF16),
        "gains": jnp.concatenate([qn_a[l][None], kn_a[l][None], qn_b[l][None], kn_b[l][None], zrow], axis=0),
        "na_bias": _na_bias_table(rpb_a[l]),
        "wa": w_branch_a[l].astype(BF16),
        "wb": w_branch_b[l].astype(BF16),
        "wo": w_out[l].astype(BF16),
        "g2": norm2_g[l][None, :],
        "wr": jnp.concatenate([router_group_w[l], router_expert_w[l], jnp.zeros((D_MODEL, pad_r), F32)],
                              axis=1).astype(BF16),
        "rbias": jnp.concatenate([router_group_b[l], router_expert_b[l], jnp.zeros((pad_r,), F32)])[None, :],
        "wg": w_gate[l].astype(BF16),
        "wu": w_up[l].astype(BF16),
        "wd": w_down[l].astype(BF16),
    }


def kernel(x_prompt, x_sample, norm1_g, w_in, qn_a, kn_a, rpb_a, qn_b, kn_b, w_branch_a, w_branch_b, w_out,
           norm2_g, router_group_w, router_group_b, router_expert_w, router_expert_b, w_gate, w_up, w_down):
    p = _prepare_params(norm1_g, w_in, qn_a, kn_a, rpb_a, qn_b, kn_b, w_branch_a, w_branch_b, w_out, norm2_g,
                        router_group_w, router_group_b, router_expert_w, router_expert_b, w_gate, w_up, w_down)
    return (_layer(x_prompt, p), _layer(x_sample, p))
```

```python
import functools

import jax
import jax.numpy as jnp
import numpy as np
from jax import lax
from jax.experimental import pallas as pl
from jax.experimental.pallas import tpu as pltpu

F32 = jnp.float32
BF16 = jnp.bfloat16

D_MODEL = 2048
HEAD_DIM = 128
N_HEADS_NA = 4
N_HEADS_PER_DIL = 4
DIL_CONFIGS = ((128, 1), (512, 4), (2048, 16))
W_NA = N_HEADS_NA * HEAD_DIM
W_DIL = N_HEADS_PER_DIL * len(DIL_CONFIGS) * HEAD_DIM
W_GRP = N_HEADS_PER_DIL * HEAD_DIM
GRID_W = 64
NA_ROWS = 8
NA_COLS = 16
ROT_DIM = HEAD_DIM // 4
ROPE_THETA = 500000.0
BAND_BLOCK = 128
N_GROUPS = 4
EXPERTS_PER_GROUP = 8
N_EXPERTS = N_GROUPS * EXPERTS_PER_GROUP
D_EXPERT = 1024
EPS = 1e-6
NEG = -1e30
IN_WIDTH = 3 * W_NA + 3 * W_DIL + 2 * D_MODEL
SCALE = HEAD_DIM ** -0.5

COL_BLK = 512
JB_G0, JB_G1, JB_G2, JB_GATE = 3, 6, 9, 12
N_COL_BLK = IN_WIDTH // COL_BLK

LANES = 128
VMEM_LIMIT = 56 * 1024 * 1024


def _cparams(sem, vmem=VMEM_LIMIT):
    return pltpu.CompilerParams(dimension_semantics=sem, vmem_limit_bytes=vmem)


def _inproj_kernel(x_ref, g1_ref, w_ref, gains_ref, cos_ref, sin_ref,
                   main_ref, d1_ref, d2_ref, gate_ref, h_ref, acc_ref, y_ref):
    j = pl.program_id(1)
    jb = j - 1
    nh = COL_BLK // HEAD_DIM
    kind = jb % 3
    attn = (jb >= 0) & (jb < JB_GATE)
    gain_row = kind + jnp.where(jb < JB_G0, 0, 2)

    def matmul():
        acc_ref[...] = jnp.dot(h_ref[...], w_ref[...], preferred_element_type=F32)

    @pl.when(j == 0)
    def _():
        x = x_ref[...]
        ms = jnp.mean(x * x, axis=-1, keepdims=True)
        h_ref[...] = (x * lax.rsqrt(ms + EPS) * g1_ref[...]).astype(BF16)
        matmul()

    @pl.when(attn & (kind != 2))
    def _():
        gain = gains_ref[pl.ds(gain_row, 1), :]
        cos, sin = cos_ref[...], sin_ref[...]
        low = lax.broadcasted_iota(jnp.int32, cos.shape, 1) < ROT_DIM // 2
        for hh in range(nh):
            xh = acc_ref[:, hh * HEAD_DIM:(hh + 1) * HEAD_DIM]
            ms = jnp.mean(xh * xh, axis=-1, keepdims=True)
            y = xh * lax.rsqrt(ms + EPS) * gain
            partner = jnp.where(low, pltpu.roll(y, HEAD_DIM - ROT_DIM // 2, 1), pltpu.roll(y, ROT_DIM // 2, 1))
            y_ref[hh] = y * cos + partner * sin
        matmul()

    @pl.when(attn & (kind == 2))
    def _():
        for hh in range(nh):
            y_ref[hh] = acc_ref[:, hh * HEAD_DIM:(hh + 1) * HEAD_DIM]
        matmul()

    def gates():
        gate_ref[...] = (0.5 * jnp.tanh(0.5 * acc_ref[...]) + 0.5).astype(BF16)

    @pl.when((jb >= JB_GATE) & (j < N_COL_BLK))
    def _():
        gates()
        matmul()

    @pl.when(j == N_COL_BLK)
    def _():
        gates()

    @pl.when(attn & (jb < JB_G1))
    def _():
        for hh in range(nh):
            main_ref[:, hh * HEAD_DIM:(hh + 1) * HEAD_DIM] = y_ref[hh].astype(BF16)

    def deinterleave(dst_ref, dil):
        rows = y_ref.shape[1] // dil
        for ph in range(dil):
            for hh in range(nh):
                dst_ref[ph, :, hh * HEAD_DIM:(hh + 1) * HEAD_DIM] = (
                    y_ref[hh, pl.ds(ph, rows, stride=dil), :].astype(BF16))

    @pl.when((jb >= JB_G1) & (jb < JB_G2))
    def _():
        deinterleave(d1_ref, DIL_CONFIGS[1][1])

    @pl.when((jb >= JB_G2) & (jb < JB_GATE))
    def _():
        deinterleave(d2_ref, DIL_CONFIGS[2][1])


def _inproj(x2, g1, w_in_bf, gains, cos_t, sin_t, B, T, tm):
    n = x2.shape[0]
    tpos = T // tm
    d1, d2 = DIL_CONFIGS[1][1], DIL_CONFIGS[2][1]
    tab = pl.BlockSpec((None, tm, HEAD_DIM), lambda i, j: (jnp.where(j - 1 >= JB_G0, 1, 0), i % tpos, 0))
    jb = lambda j: jnp.maximum(j - 1, 0)
    return pl.pallas_call(
        _inproj_kernel,
        grid=(n // tm, N_COL_BLK + 1),
        in_specs=[
            pl.BlockSpec((tm, D_MODEL), lambda i, j: (i, 0)),
            pl.BlockSpec((1, D_MODEL), lambda i, j: (0, 0)),
            pl.BlockSpec((D_MODEL, COL_BLK), lambda i, j: (0, jnp.minimum(j, N_COL_BLK - 1))),
            pl.BlockSpec((8, HEAD_DIM), lambda i, j: (0, 0)),
            tab, tab,
        ],
        out_specs=[
            pl.BlockSpec((tm, COL_BLK), lambda i, j: (i, jnp.minimum(jb(j), JB_G1 - 1))),
            pl.BlockSpec((None, d1, tm // d1, COL_BLK),
                         lambda i, j: (i // tpos, 0, i % tpos, jnp.clip(jb(j) - JB_G1, 0, 2))),
            pl.BlockSpec((None, d2, tm // d2, COL_BLK),
                         lambda i, j: (i // tpos, 0, i % tpos, jnp.clip(jb(j) - JB_G2, 0, 2))),
            pl.BlockSpec((tm, COL_BLK), lambda i, j: (i, jnp.clip(jb(j) - JB_GATE, 0, N_COL_BLK - JB_GATE - 1))),
        ],
        out_shape=[
            jax.ShapeDtypeStruct((n, JB_G1 * COL_BLK), BF16),
            jax.ShapeDtypeStruct((B, d1, T // d1, 3 * COL_BLK), BF16),
            jax.ShapeDtypeStruct((B, d2, T // d2, 3 * COL_BLK), BF16),
            jax.ShapeDtypeStruct((n, 2 * D_MODEL), BF16),
        ],
        scratch_shapes=[pltpu.VMEM((tm, D_MODEL), BF16), pltpu.VMEM((tm, COL_BLK), F32),
                        pltpu.VMEM((COL_BLK // HEAD_DIM, tm, HEAD_DIM), F32)],
        compiler_params=_cparams(("arbitrary", "arbitrary")),
        name="inproj",
    )(x2, g1, w_in_bf, gains, cos_t, sin_t)


NA_CHUNK_ROWS = 16
NA_HALO_ROWS = 4
NA_TQ = NA_CHUNK_ROWS * GRID_W
NA_TH = NA_HALO_ROWS * GRID_W
NA_GROUP = 4
NA_SPAN_ROWS = 12
NA_GQ = NA_GROUP * GRID_W
NA_SPAN = NA_SPAN_ROWS * GRID_W
NA_CASES = 3
assert NA_CHUNK_ROWS % NA_GROUP == 0 and NA_HALO_ROWS == NA_ROWS // 2
assert NA_ROWS + NA_GROUP - 1 <= NA_SPAN_ROWS <= NA_GROUP + 2 * NA_HALO_ROWS


def _natten_kernel(q_ref, kp_ref, km_ref, kn_ref, vp_ref, vm_ref, vn_ref, bias_ref, o_ref, kw_ref, vw_ref, *, rows):
    i = pl.program_id(1)
    kw_ref[0:NA_TH] = kp_ref[...]
    kw_ref[NA_TH:NA_TH + NA_TQ] = km_ref[...]
    kw_ref[NA_TH + NA_TQ:] = kn_ref[...]
    vw_ref[0:NA_TH] = vp_ref[...]
    vw_ref[NA_TH:NA_TH + NA_TQ] = vm_ref[...]
    vw_ref[NA_TH + NA_TQ:] = vn_ref[...]

    for gg in range(NA_CHUNK_ROWS // NA_GROUP):
        r0 = i * NA_CHUNK_ROWS + gg * NA_GROUP
        rs0 = jnp.clip(r0 - NA_ROWS // 2, 0, rows - NA_ROWS)
        case = jnp.where(r0 == 0, 0, jnp.where(r0 == rows - NA_GROUP, 2, 1))
        off = pl.multiple_of((rs0 - (i * NA_CHUNK_ROWS - NA_HALO_ROWS)) * GRID_W, GRID_W)
        for h in range(N_HEADS_NA):
            cols = slice(h * HEAD_DIM, (h + 1) * HEAD_DIM)
            q = q_ref[gg * NA_GQ:(gg + 1) * NA_GQ, cols]
            k = kw_ref[pl.ds(off, NA_SPAN), cols]
            v = vw_ref[pl.ds(off, NA_SPAN), cols]
            s = lax.dot_general(q, k, (((1,), (1,)), ((), ())), preferred_element_type=F32)
            s = s * SCALE + bias_ref[case, h]
            m = jnp.max(s, axis=-1, keepdims=True)
            p = jnp.exp(s - m)
            l = jnp.sum(p, axis=-1, keepdims=True)
            o = jnp.dot(p.astype(BF16), v, preferred_element_type=F32) / l
            o_ref[gg * NA_GQ:(gg + 1) * NA_GQ, cols] = o.astype(BF16)


def _natten(proj3, bias_tab, B, T):
    rows = T // GRID_W
    nh = T // NA_TH
    ratio = NA_TQ // NA_TH
    blk = lambda rws, fn: pl.BlockSpec((None, rws, COL_BLK), fn)
    return pl.pallas_call(
        functools.partial(_natten_kernel, rows=rows),
        grid=(B, T // NA_TQ),
        in_specs=[
            blk(NA_TQ, lambda b, i: (b, i, 0)),
            blk(NA_TH, lambda b, i: (b, jnp.maximum(i * ratio - 1, 0), 1)),
            blk(NA_TQ, lambda b, i: (b, i, 1)),
            blk(NA_TH, lambda b, i: (b, jnp.minimum((i + 1) * ratio, nh - 1), 1)),
            blk(NA_TH, lambda b, i: (b, jnp.maximum(i * ratio - 1, 0), 2)),
            blk(NA_TQ, lambda b, i: (b, i, 2)),
            blk(NA_TH, lambda b, i: (b, jnp.minimum((i + 1) * ratio, nh - 1), 2)),
            pl.BlockSpec((NA_CASES, N_HEADS_NA, NA_GQ, NA_SPAN), lambda b, i: (0, 0, 0, 0)),
        ],
        out_specs=pl.BlockSpec((None, NA_TQ, W_NA), lambda b, i: (b, i, 0)),
        out_shape=jax.ShapeDtypeStruct((B, T, W_NA), BF16),
        scratch_shapes=[pltpu.VMEM((NA_TQ + 2 * NA_TH, COL_BLK), BF16)] * 2,
        compiler_params=_cparams(("arbitrary", "arbitrary")),
        name="natten",
    )(proj3, proj3, proj3, proj3, proj3, proj3, proj3, bias_tab)


def _na_bias_table(rpb):
    c = np.arange(GRID_W)
    cs = np.clip(c - NA_COLS // 2, 0, GRID_W - NA_COLS)
    cc = np.arange(GRID_W)
    valid = (cc[None, :] >= cs[:, None]) & (cc[None, :] < cs[:, None] + NA_COLS)
    dc_idx = np.clip(cc[None, :] - c[:, None] + NA_COLS - 1, 0, 2 * NA_COLS - 2)
    pick = (dc_idx[..., None] == np.arange(2 * NA_COLS - 1)).astype(np.float32)
    t = jnp.einsum("qkd,hrd->hrqk", pick, rpb.astype(F32), precision=lax.Precision.HIGHEST)
    t = jnp.where(valid[None, None], t, NEG)
    half = NA_ROWS // 2
    cases = ([(m, 0) for m in range(NA_GROUP)],
             [(half, m) for m in range(NA_GROUP)],
             [(half + m, 0) for m in range(NA_GROUP)])
    tiles = []
    for case in cases:
        per_row = []
        for var, woff in case:
            win = t[:, NA_ROWS - 1 - var:2 * NA_ROWS - 1 - var]
            pad = ((0, 0), (woff, NA_SPAN_ROWS - NA_ROWS - woff), (0, 0), (0, 0))
            per_row.append(jnp.pad(win, pad, constant_values=NEG))
        tile = jnp.stack(per_row, axis=1)
        tiles.append(tile.transpose(0, 1, 3, 2, 4).reshape(N_HEADS_NA, NA_GQ, NA_SPAN))
    return jnp.stack(tiles, axis=0)


DIL_HALF = 64
DIL_KB = BAND_BLOCK + 2 * DIL_HALF
DIL_TOKENS_PER_STEP = (1024, 1024, 2048)


def _dilattn_kernel(q_ref, kp_ref, km_ref, kn_ref, vp_ref, vm_ref, vn_ref, o_ref, lse_ref,
                    kw_ref, vw_ref, os_ref, ls_ref, *, L, tq, dil):
    i = pl.program_id(1)
    kw_ref[:, 0:DIL_HALF] = kp_ref[...]
    kw_ref[:, DIL_HALF:DIL_HALF + tq] = km_ref[...]
    kw_ref[:, DIL_HALF + tq:] = kn_ref[...]
    vw_ref[:, 0:DIL_HALF] = vp_ref[...]
    vw_ref[:, DIL_HALF:DIL_HALF + tq] = vm_ref[...]
    vw_ref[:, DIL_HALF + tq:] = vn_ref[...]

    def phase(j):
        r_io = lax.broadcasted_iota(jnp.int32, (BAND_BLOCK, DIL_KB), 0)
        c_io = lax.broadcasted_iota(jnp.int32, (BAND_BLOCK, DIL_KB), 1)
        lane = lax.broadcasted_iota(jnp.int32, (BAND_BLOCK, LANES), 1)
        for s in range(tq // BAND_BLOCK):
            base = i * tq + s * BAND_BLOCK - DIL_HALF
            lo = jnp.maximum(r_io, -base)
            hi = jnp.minimum(r_io + 2 * DIL_HALF, L - 1 - base)
            valid = (c_io >= lo) & (c_io <= hi)
            out_rows = pl.ds(s * BAND_BLOCK * dil + j, BAND_BLOCK, stride=dil)
            lse_tile = jnp.zeros((BAND_BLOCK, LANES), F32)
            for h in range(N_HEADS_PER_DIL):
                cols = slice(h * HEAD_DIM, (h + 1) * HEAD_DIM)
                q = q_ref[j, s * BAND_BLOCK:(s + 1) * BAND_BLOCK, cols]
                k = kw_ref[j, s * BAND_BLOCK:s * BAND_BLOCK + DIL_KB, cols]
                v = vw_ref[j, s * BAND_BLOCK:s * BAND_BLOCK + DIL_KB, cols]
                sc = lax.dot_general(q, k, (((1,), (1,)), ((), ())), preferred_element_type=F32) * SCALE
                sc = jnp.where(valid, sc, NEG)
                m = jnp.max(sc, axis=-1, keepdims=True)
                p = jnp.exp(sc - m)
                l = jnp.sum(p, axis=-1, keepdims=True)
                os_ref[h, out_rows, :] = jnp.dot(p.astype(BF16), v, preferred_element_type=F32) / l
                lse_tile = jnp.where(lane == h, m + jnp.log(l), lse_tile)
            ls_ref[out_rows, :] = lse_tile

    if dil == 1:
        phase(0)
    else:
        def body(j, c):
            phase(j)
            return c
        lax.fori_loop(0, dil, body, 0, unroll=2)
    for h in range(N_HEADS_PER_DIL):
        o_ref[:, h * HEAD_DIM:(h + 1) * HEAD_DIM] = os_ref[h].astype(BF16)
    lse_ref[...] = ls_ref[...]


def _dilattn(qkv, col0, B, T, g):
    dil = DIL_CONFIGS[g][1]
    assert DIL_CONFIGS[g][0] // 2 // dil == DIL_HALF
    L = T // dil
    tq = min(DIL_TOKENS_PER_STEP[g] // dil, L)
    nh = L // DIL_HALF
    ratio = tq // DIL_HALF
    blk = lambda rws, fn: pl.BlockSpec((None, dil, rws, COL_BLK), fn)
    prev = lambda i: jnp.maximum(i * ratio - 1, 0)
    nxt = lambda i: jnp.minimum((i + 1) * ratio, nh - 1)
    o, lse = pl.pallas_call(
        functools.partial(_dilattn_kernel, L=L, tq=tq, dil=dil),
        grid=(B, L // tq),
        in_specs=[
            blk(tq, lambda b, i: (b, 0, i, col0)),
            blk(DIL_HALF, lambda b, i: (b, 0, prev(i), col0 + 1)),
            blk(tq, lambda b, i: (b, 0, i, col0 + 1)),
            blk(DIL_HALF, lambda b, i: (b, 0, nxt(i), col0 + 1)),
            blk(DIL_HALF, lambda b, i: (b, 0, prev(i), col0 + 2)),
            blk(tq, lambda b, i: (b, 0, i, col0 + 2)),
            blk(DIL_HALF, lambda b, i: (b, 0, nxt(i), col0 + 2)),
        ],
        out_specs=[
            pl.BlockSpec((None, tq * dil, W_GRP), lambda b, i: (b, i, 0)),
            pl.BlockSpec((None, tq * dil, LANES), lambda b, i: (b, i, 0)),
        ],
        out_shape=[
            jax.ShapeDtypeStruct((B, T, W_GRP), BF16),
            jax.ShapeDtypeStruct((B, T, LANES), F32),
        ],
        scratch_shapes=[pltpu.VMEM((dil, tq + 2 * DIL_HALF, COL_BLK), BF16)] * 2
        + [pltpu.VMEM((N_HEADS_PER_DIL, tq * dil, HEAD_DIM), F32), pltpu.VMEM((tq * dil, LANES), F32)],
        compiler_params=_cparams(("arbitrary", "arbitrary")),
        name=f"dilattn{g}",
    )(qkv, qkv, qkv, qkv, qkv, qkv, qkv)
    return o.reshape(B * T, W_GRP), lse.reshape(B * T, LANES)


def _outproj_kernel(a_ref, o0_ref, o1_ref, o2_ref, l0_ref, l1_ref, l2_ref, ga_ref, gb_ref, x_ref,
                    wa_ref, wb_ref, wo_ref, g2_ref, wr_ref, x1_ref, lg_ref):
    o_refs = (o0_ref, o1_ref, o2_ref)
    lses = [r[...] for r in (l0_ref, l1_ref, l2_ref)]
    parts = []
    for h in range(N_HEADS_PER_DIL):
        cols = slice(h * HEAD_DIM, (h + 1) * HEAD_DIM)
        lh = [l[:, h:h + 1] for l in lses]
        mx = jnp.maximum(jnp.maximum(lh[0], lh[1]), lh[2])
        e = [jnp.exp(v - mx) for v in lh]
        den = e[0] + e[1] + e[2]
        acc = None
        for g in range(len(DIL_CONFIGS)):
            t = (e[g] / den) * o_refs[g][:, cols].astype(F32)
            acc = t if acc is None else acc + t
        parts.append(acc.astype(BF16))
    att_b = jnp.concatenate(parts, axis=-1)
    ya = jnp.dot(a_ref[...], wa_ref[...], preferred_element_type=F32)
    yb = jnp.dot(att_b, wb_ref[...], preferred_element_type=F32)
    mix = ga_ref[...].astype(F32) * ya + gb_ref[...].astype(F32) * yb
    x1 = x_ref[...] + jnp.dot(mix.astype(BF16), wo_ref[...], preferred_element_type=F32)
    x1_ref[...] = x1
    ms = jnp.mean(x1 * x1, axis=-1, keepdims=True)
    h2 = x1 * lax.rsqrt(ms + EPS) * g2_ref[...]
    lg_ref[...] = jnp.dot(h2.astype(BF16), wr_ref[...], preferred_element_type=F32)


def _outproj(att_a, os_, lses, gates, x2, wa, wb, wo, g2, wr, tm):
    n = x2.shape[0]
    row = lambda w: pl.BlockSpec((tm, w), lambda i: (i, 0))
    full = lambda a: pl.BlockSpec(a.shape, lambda i: (0,) * a.ndim)
    return pl.pallas_call(
        _outproj_kernel,
        grid=(n // tm,),
        in_specs=[row(W_NA), row(W_GRP), row(W_GRP), row(W_GRP), row(LANES), row(LANES), row(LANES),
                  pl.BlockSpec((tm, D_MODEL), lambda i: (i, 0)),
                  pl.BlockSpec((tm, D_MODEL), lambda i: (i, 1)),
                  row(D_MODEL), full(wa), full(wb), full(wo), full(g2), full(wr)],
        out_specs=[row(D_MODEL), row(LANES)],
        out_shape=[jax.ShapeDtypeStruct((n, D_MODEL), F32), jax.ShapeDtypeStruct((n, LANES), F32)],
        compiler_params=_cparams(("arbitrary",)),
        name="outproj",
    )(att_a, *os_, *lses, gates, gates, x2, wa, wb, wo, g2, wr)


ROUTE_E0 = N_GROUPS


def _route_kernel(lg_ref, b_ref, gate_ref, eid_ref, rank_ref, cnt_ref, carry_ref):
    i = pl.program_id(0)

    @pl.when(i == 0)
    def _():
        carry_ref[...] = jnp.zeros_like(carry_ref)

    z = lg_ref[...] + b_ref[...]
    tm = z.shape[0]
    lane = lax.broadcasted_iota(jnp.int32, z.shape, 1)
    lane_f = lane.astype(F32)
    rmax = lambda v: jnp.max(v, axis=-1, keepdims=True)
    rmin = lambda v: jnp.min(v, axis=-1, keepdims=True)
    rsum = lambda v: jnp.sum(v, axis=-1, keepdims=True)

    gmask = lane < N_GROUPS
    gmax = rmax(jnp.where(gmask, z, -jnp.inf))
    gsel = rmin(jnp.where(gmask & (z == gmax), lane_f, float(LANES)))
    gp = 1.0 / rsum(jnp.where(gmask, jnp.exp(z - gmax), 0.0))

    elo = float(ROUTE_E0) + float(EXPERTS_PER_GROUP) * gsel
    emask = (lane_f >= elo) & (lane_f < elo + float(EXPERTS_PER_GROUP))
    emax = rmax(jnp.where(emask, z, -jnp.inf))
    ee = jnp.where(emask, jnp.exp(z - emax), 0.0)
    eprob = ee / rsum(ee)
    p1 = rmax(jnp.where(emask, eprob, -1.0))
    i1 = rmin(jnp.where(emask & (eprob == p1), lane_f, float(LANES)))
    m2 = emask & (lane_f != i1)
    p2 = rmax(jnp.where(m2, eprob, -1.0))
    i2 = rmin(jnp.where(m2 & (eprob == p2), lane_f, float(LANES)))
    den = p1 + p2
    g1 = gp * p1 / den
    g2 = gp * p2 / den

    oh1 = (lane_f == i1).astype(F32)
    oh2 = (lane_f == i2).astype(F32)
    oh = oh1 + oh2
    tri = (lax.broadcasted_iota(jnp.int32, (tm, tm), 1) < lax.broadcasted_iota(jnp.int32, (tm, tm), 0)).astype(BF16)
    tot = carry_ref[...] + jnp.dot(tri, oh.astype(BF16), preferred_element_type=F32)
    r1 = rsum(oh1 * tot)
    r2 = rsum(oh2 * tot)
    carry = carry_ref[...] + jnp.sum(oh, axis=0, keepdims=True)
    carry_ref[...] = carry
    cnt_ref[...] = carry

    two = lambda a, b: jnp.where(lane == 0, a, jnp.where(lane == 1, b, 0.0))
    gate_ref[...] = two(g1, g2)
    eid_ref[...] = two(i1 - float(ROUTE_E0), i2 - float(ROUTE_E0)).astype(jnp.int32)
    rank_ref[...] = two(r1, r2).astype(jnp.int32)


def _route(logits, rbias, tm):
    n = logits.shape[0]
    row = pl.BlockSpec((tm, LANES), lambda i: (i, 0))
    one = pl.BlockSpec((1, LANES), lambda i: (0, 0))
    return pl.pallas_call(
        _route_kernel,
        grid=(n // tm,),
        in_specs=[row, one],
        out_specs=[row, row, row, one],
        out_shape=[jax.ShapeDtypeStruct((n, LANES), F32), jax.ShapeDtypeStruct((n, LANES), jnp.int32),
                   jax.ShapeDtypeStruct((n, LANES), jnp.int32), jax.ShapeDtypeStruct((1, LANES), F32)],
        scratch_shapes=[pltpu.VMEM((1, LANES), F32)],
        compiler_params=_cparams(("arbitrary",)),
        name="route",
    )(logits, rbias)


TOP_K = 2


ROW_SUB = 8
HALF_D = D_MODEL // 2
assert ROW_SUB * LANES == HALF_D


def _pack_pair(lo, hi):
    lo_b = lax.bitcast_convert_type(lo.astype(BF16).astype(F32), jnp.uint32)
    hi_b = lax.bitcast_convert_type(hi.astype(BF16).astype(F32), jnp.uint32)
    return (lo_b >> 16) | hi_b


def _unpack_pair(w):
    lo = lax.bitcast_convert_type(w << 16, F32)
    hi = lax.bitcast_convert_type(w & jnp.uint32(0xFFFF0000), F32)
    return lo, hi


def _store_packed_rows(dst, x):
    for c in range(ROW_SUB):
        dst(c, _pack_pair(x[:, c * LANES:(c + 1) * LANES], x[:, HALF_D + c * LANES:HALF_D + (c + 1) * LANES]))


def _dispatch_kernel(pos_ref, x1_ref, g2_ref, xs_ref, hp_ref, sem):
    i = pl.program_id(0)
    last = pl.num_programs(0) - 1
    slot = i % 2
    x1 = x1_ref[...]
    tm = x1.shape[0]
    ms = jnp.mean(x1 * x1, axis=-1, keepdims=True)
    h = x1 * lax.rsqrt(ms + EPS) * g2_ref[...]

    def put(c, words):
        hp_ref[slot, pl.ds(c, tm, stride=ROW_SUB), :] = words

    _store_packed_rows(put, h)

    def row_copy(sl, r, p):
        src = hp_ref.at[sl, pl.ds(pl.multiple_of(r * ROW_SUB, ROW_SUB), ROW_SUB)]
        dst = xs_ref.at[pl.ds(pl.multiple_of(p * ROW_SUB, ROW_SUB), ROW_SUB)]
        return pltpu.make_async_copy(src, dst, sem.at[sl])

    def start(r, c):
        for k in range(TOP_K):
            row_copy(slot, r, pos_ref[0, TOP_K * r + k]).start(priority=k)
        return c

    def wait_all(sl):
        def wait(r, c):
            for k in range(TOP_K):
                row_copy(sl, 0, 0).wait()
            return c
        lax.fori_loop(0, tm, wait, 0)

    lax.fori_loop(0, tm, start, 0, unroll=4)

    @pl.when(i > 0)
    def _():
        wait_all(1 - slot)

    @pl.when(i == last)
    def _():
        wait_all(slot)


def _dispatch(pos3, x1, g2, tm):
    n = x1.shape[0]
    return pl.pallas_call(
        _dispatch_kernel,
        grid=(n // tm,),
        in_specs=[pl.BlockSpec((None, 1, TOP_K * tm), lambda i: (i, 0, 0), memory_space=pltpu.SMEM),
                  pl.BlockSpec((tm, D_MODEL), lambda i: (i, 0)),
                  pl.BlockSpec((1, D_MODEL), lambda i: (0, 0))],
        out_specs=pl.BlockSpec(memory_space=pl.ANY),
        out_shape=jax.ShapeDtypeStruct((n * TOP_K * ROW_SUB, LANES), jnp.uint32),
        scratch_shapes=[pltpu.VMEM((2, tm * ROW_SUB, LANES), jnp.uint32), pltpu.SemaphoreType.DMA((2,))],
        compiler_params=_cparams(("arbitrary",)),
        name="dispatch",
    )(pos3, x1, g2)


def _combine_kernel(pos_ref, posn_ref, gate_ref, x1_ref, ys_ref, y_ref, buf_ref, sem):
    i = pl.program_id(0)
    last = pl.num_programs(0) - 1
    slot = i % 2
    tm = x1_ref.shape[0]

    def row_copy(sl, k, r, p):
        src = ys_ref.at[pl.ds(pl.multiple_of(p * ROW_SUB, ROW_SUB), ROW_SUB)]
        dst = buf_ref.at[sl, k, pl.ds(pl.multiple_of(r * ROW_SUB, ROW_SUB), ROW_SUB)]
        return pltpu.make_async_copy(src, dst, sem.at[sl])

    def start_all(sl, p_ref):
        def start(r, c):
            for k in range(TOP_K):
                row_copy(sl, k, r, p_ref[0, TOP_K * r + k]).start(priority=k)
            return c
        lax.fori_loop(0, tm, start, 0, unroll=4)

    @pl.when(i == 0)
    def _():
        start_all(0, pos_ref)

    @pl.when(i < last)
    def _():
        start_all(1 - slot, posn_ref)

    def wait(r, c):
        for k in range(TOP_K):
            row_copy(slot, k, 0, 0).wait()
        return c

    lax.fori_loop(0, tm, wait, 0)
    gates = gate_ref[...]
    g0 = jnp.broadcast_to(gates[:, 0:1], (tm, LANES))
    g1 = jnp.broadcast_to(gates[:, 1:2], (tm, LANES))
    for c in range(ROW_SUB):
        lo0, hi0 = _unpack_pair(buf_ref[slot, 0, pl.ds(c, tm, stride=ROW_SUB), :])
        lo1, hi1 = _unpack_pair(buf_ref[slot, 1, pl.ds(c, tm, stride=ROW_SUB), :])
        cl = slice(c * LANES, (c + 1) * LANES)
        ch = slice(HALF_D + c * LANES, HALF_D + (c + 1) * LANES)
        y_ref[:, cl] = x1_ref[:, cl] + (g0 * lo0 + g1 * lo1)
        y_ref[:, ch] = x1_ref[:, ch] + (g0 * hi0 + g1 * hi1)


def _combine(pos3, gates, x1, ys, tm):
    n = x1.shape[0]
    nt = n // tm
    pos_blk = lambda fn: pl.BlockSpec((None, 1, TOP_K * tm), fn, memory_space=pltpu.SMEM)
    return pl.pallas_call(
        _combine_kernel,
        grid=(nt,),
        in_specs=[pos_blk(lambda i: (i, 0, 0)),
                  pos_blk(lambda i: (jnp.minimum(i + 1, nt - 1), 0, 0)),
                  pl.BlockSpec((tm, LANES), lambda i: (i, 0)),
                  pl.BlockSpec((tm, D_MODEL), lambda i: (i, 0)),
                  pl.BlockSpec(memory_space=pl.ANY)],
        out_specs=pl.BlockSpec((tm, D_MODEL), lambda i: (i, 0)),
        out_shape=jax.ShapeDtypeStruct((n, D_MODEL), F32),
        scratch_shapes=[pltpu.VMEM((2, TOP_K, tm * ROW_SUB, LANES), jnp.uint32), pltpu.SemaphoreType.DMA((2,))],
        compiler_params=_cparams(("arbitrary",)),
        name="combine",
    )(pos3, pos3, gates, x1, ys)


def _experts_kernel(tile_s, exp_s, lo_s, hi_s, first_s, x_ref, wg_ref, wu_ref, wd_ref, y_ref, xb_ref):
    w = pl.program_id(0)
    lo, hi = lo_s[w], hi_s[w]
    bm = xb_ref.shape[0]

    @pl.when(hi > lo)
    def _():
        for c in range(ROW_SUB):
            xl, xh = _unpack_pair(x_ref[pl.ds(c, bm, stride=ROW_SUB), :])
            xb_ref[:, c * LANES:(c + 1) * LANES] = xl.astype(BF16)
            xb_ref[:, HALF_D + c * LANES:HALF_D + (c + 1) * LANES] = xh.astype(BF16)
        x = xb_ref[...]
        a = jnp.dot(x, wg_ref[...], preferred_element_type=F32)
        u = jnp.dot(x, wu_ref[...], preferred_element_type=F32)
        hm = (a * jax.nn.sigmoid(a) * u).astype(BF16)
        y = jnp.dot(hm, wd_ref[...], preferred_element_type=F32)
        rows = lax.broadcasted_iota(jnp.int32, (bm, LANES), 0)
        mine = (rows >= lo) & (rows < hi)

        @pl.when(first_s[w] == 1)
        def _():
            def put(c, words):
                y_ref[pl.ds(c, bm, stride=ROW_SUB), :] = jnp.where(mine, words, jnp.uint32(0))
            _store_packed_rows(put, y)

        @pl.when(first_s[w] == 0)
        def _():
            def put(c, words):
                rws = pl.ds(c, bm, stride=ROW_SUB)
                y_ref[rws, :] = jnp.where(mine, words, y_ref[rws, :])
            _store_packed_rows(put, y)


def _experts(meta, xs, wg, wu, wd, bm):
    p = xs.shape[0] // ROW_SUB
    nw = p // bm + N_EXPERTS
    grid_spec = pltpu.PrefetchScalarGridSpec(
        num_scalar_prefetch=5,
        grid=(nw,),
        in_specs=[
            pl.BlockSpec((bm * ROW_SUB, LANES), lambda w, t, e, lo, hi, f: (t[w], 0)),
            pl.BlockSpec((None, D_MODEL, D_EXPERT), lambda w, t, e, lo, hi, f: (e[w], 0, 0)),
            pl.BlockSpec((None, D_MODEL, D_EXPERT), lambda w, t, e, lo, hi, f: (e[w], 0, 0)),
            pl.BlockSpec((None, D_EXPERT, D_MODEL), lambda w, t, e, lo, hi, f: (e[w], 0, 0)),
        ],
        out_specs=pl.BlockSpec((bm * ROW_SUB, LANES), lambda w, t, e, lo, hi, f: (t[w], 0)),
        scratch_shapes=[pltpu.VMEM((bm, D_MODEL), BF16)],
    )
    return pl.pallas_call(
        _experts_kernel,
        grid_spec=grid_spec,
        out_shape=jax.ShapeDtypeStruct((p * ROW_SUB, LANES), jnp.uint32),
        compiler_params=_cparams(("arbitrary",)),
        name="experts",
    )(*meta, xs, wg, wu, wd)


def _expert_work_items(counts, p, bm):
    ntiles = p // bm
    nw = ntiles + N_EXPERTS
    ends = jnp.cumsum(counts)
    starts = ends - counts
    ft = starts // bm
    nt = jnp.where(counts > 0, (ends - 1) // bm - ft + 1, 0)
    wend = jnp.cumsum(nt)
    wstart = wend - nt
    w = jnp.arange(nw, dtype=jnp.int32)
    used = w < wend[-1]
    e = jnp.minimum(jnp.sum((w[:, None] >= wend[None, :]).astype(jnp.int32), axis=1), N_EXPERTS - 1)
    e_last = jnp.max(jnp.where(counts > 0, jnp.arange(N_EXPERTS, dtype=jnp.int32), 0))
    e = jnp.where(used, e, e_last)
    tile = jnp.where(used, ft[e] + (w - wstart[e]), ntiles - 1).astype(jnp.int32)
    lo = jnp.where(used, jnp.maximum(starts[e], tile * bm) - tile * bm, 0).astype(jnp.int32)
    hi = jnp.where(used, jnp.minimum(ends[e], (tile + 1) * bm) - tile * bm, 0).astype(jnp.int32)
    first = jnp.concatenate([jnp.ones((1,), jnp.int32), (tile[1:] != tile[:-1]).astype(jnp.int32)])
    return (tile, e, lo, hi, first), starts


def _rope_tables(T):
    half = ROT_DIM // 2
    inv = 1.0 / (ROPE_THETA ** (np.arange(half, dtype=np.float64) * (2.0 / ROT_DIM)))
    ang = np.arange(T, dtype=np.float64)[:, None] * inv[None, :]
    cos, sin = np.cos(ang), np.sin(ang)
    rest = HEAD_DIM - ROT_DIM
    z_half, z_rest = np.zeros((T, half)), np.zeros((T, rest))
    cos_t = np.concatenate([cos, cos, np.ones((T, rest))], axis=1)
    sin_t = np.concatenate([-sin, sin, z_rest], axis=1)
    cos_t = np.stack([np.ones_like(cos_t), cos_t])
    sin_t = np.stack([np.zeros_like(sin_t), sin_t])
    return tuple(jnp.asarray(t, dtype=F32) for t in (cos_t, sin_t))


def _layer(x, p):
    B, T, D = x.shape
    n = B * T
    assert D == D_MODEL and T % NA_TQ == 0 and T // GRID_W >= NA_ROWS + NA_HALO_ROWS
    x2 = x.reshape(n, D)
    tm_in = 1024
    tm_out = 256
    tm_route = 512
    tm_move = 512
    bm = 256

    main, qkv1, qkv2, sig = _inproj(x2, p["g1"], p["w_in"], p["gains"], *_rope_tables(T), B, T, tm_in)
    main3 = main.reshape(B, T, JB_G1 * COL_BLK)
    att_a = _natten(main3, p["na_bias"], B, T).reshape(n, W_NA)
    dil = [_dilattn(main3[:, None], JB_G0, B, T, 0), _dilattn(qkv1, 0, B, T, 1), _dilattn(qkv2, 0, B, T, 2)]
    x1, logits = _outproj(att_a, [d[0] for d in dil], [d[1] for d in dil], sig, x2,
                          p["wa"], p["wb"], p["wo"], p["g2"], p["wr"], tm_out)

    gates, eid, rank, cnt = _route(logits, p["rbias"], tm_route)
    counts = cnt[0, ROUTE_E0:ROUTE_E0 + N_EXPERTS].astype(jnp.int32)
    meta, starts = _expert_work_items(counts, n * TOP_K, bm)
    is_e = eid[:, :TOP_K, None] == jnp.arange(N_EXPERTS, dtype=jnp.int32)
    pos = jnp.sum(jnp.where(is_e, starts, 0), axis=-1) + rank[:, :TOP_K]
    pos3 = pos.astype(jnp.int32).reshape(n // tm_move, 1, TOP_K * tm_move)

    xs = _dispatch(pos3, x1, p["g2"], tm_move)
    ys = _experts(meta, xs, p["wg"], p["wu"], p["wd"], bm)
    y = _combine(pos3, gates, x1, ys, tm_move)
    return y.reshape(B, T, D)


def _prepare_params(norm1_g, w_in, qn_a, kn_a, rpb_a, qn_b, kn_b, w_branch_a, w_branch_b, w_out,
                    norm2_g, router_group_w, router_group_b, router_expert_w, router_expert_b, w_gate, w_up, w_down):
    assert norm1_g.shape[0] == 1
    l = 0
    pad_r = LANES - N_GROUPS - N_EXPERTS
    zrow = jnp.zeros((4, HEAD_DIM), F32)
    col = lambda start, width: w_in[l][:, start:start + width]
    qb0, kb0, vb0 = 3 * W_NA, 3 * W_NA + W_DIL, 3 * W_NA + 2 * W_DIL
    pieces = [col(0, 3 * W_NA)]
    for g in range(len(DIL_CONFIGS)):
        pieces += [col(qb0 + g * W_GRP, W_GRP), col(kb0 + g * W_GRP, W_GRP), col(vb0 + g * W_GRP, W_GRP)]
    pieces.append(col(3 * W_NA + 3 * W_DIL, 2 * D_MODEL))
    return {
        "g1": norm1_g[l][None, :],
        "w_in": jnp.concatenate(pieces, axis=1).astype(BF16),
        "gains": jnp.concatenate([qn_a[l][None], kn_a[l][None], qn_b[l][None], kn_b[l][None], zrow], axis=0),
        "na_bias": _na_bias_table(rpb_a[l]),
        "wa": w_branch_a[l].astype(BF16),
        "wb": w_branch_b[l].astype(BF16),
        "wo": w_out[l].astype(BF16),
        "g2": norm2_g[l][None, :],
        "wr": jnp.concatenate([router_group_w[l], router_expert_w[l], jnp.zeros((D_MODEL, pad_r), F32)],
                              axis=1).astype(BF16),
        "rbias": jnp.concatenate([router_group_b[l], router_expert_b[l], jnp.zeros((pad_r,), F32)])[None, :],
        "wg": w_gate[l].astype(BF16),
        "wu": w_up[l].astype(BF16),
        "wd": w_down[l].astype(BF16),
    }


def kernel(x_prompt, x_sample, norm1_g, w_in, qn_a, kn_a, rpb_a, qn_b, kn_b, w_branch_a, w_branch_b, w_out,
           norm2_g, router_group_w, router_group_b, router_expert_w, router_expert_b, w_gate, w_up, w_down):
    p = _prepare_params(norm1_g, w_in, qn_a, kn_a, rpb_a, qn_b, kn_b, w_branch_a, w_branch_b, w_out, norm2_g,
                        router_group_w, router_group_b, router_expert_w, router_expert_b, w_gate, w_up, w_down)
    return (_layer(x_prompt, p), _layer(x_sample, p))
```

```python
import functools

import jax
import jax.numpy as jnp
import numpy as np
from jax import lax
from jax.experimental import pallas as pl
from jax.experimental.pallas import tpu as pltpu

F32 = jnp.float32
BF16 = jnp.bfloat16

D_MODEL = 2048
HEAD_DIM = 128
N_HEADS_NA = 4
N_HEADS_PER_DIL = 4
DIL_CONFIGS = ((128, 1), (512, 4), (2048, 16))
W_NA = N_HEADS_NA * HEAD_DIM
W_DIL = N_HEADS_PER_DIL * len(DIL_CONFIGS) * HEAD_DIM
W_GRP = N_HEADS_PER_DIL * HEAD_DIM
GRID_W = 64
NA_ROWS = 8
NA_COLS = 16
ROT_DIM = HEAD_DIM // 4
ROPE_THETA = 500000.0
BAND_BLOCK = 128
N_GROUPS = 4
EXPERTS_PER_GROUP = 8
N_EXPERTS = N_GROUPS * EXPERTS_PER_GROUP
D_EXPERT = 1024
EPS = 1e-6
NEG = -1e30
IN_WIDTH = 3 * W_NA + 3 * W_DIL + 2 * D_MODEL
SCALE = HEAD_DIM ** -0.5

COL_BLK = 512
JB_G0, JB_G1, JB_G2, JB_GATE = 3, 6, 9, 12
N_COL_BLK = IN_WIDTH // COL_BLK

LANES = 128
VMEM_LIMIT = 56 * 1024 * 1024


def _cparams(sem, vmem=VMEM_LIMIT):
    return pltpu.CompilerParams(dimension_semantics=sem, vmem_limit_bytes=vmem)


def _inproj_kernel(x_ref, g1_ref, w_ref, gains_ref, cos_ref, sin_ref,
                   main_ref, d1_ref, d2_ref, gate_ref, h_ref, acc_ref, y_ref):
    j = pl.program_id(1)
    jb = j - 1
    nh = COL_BLK // HEAD_DIM
    kind = jb % 3
    is_v = kind == 2
    gain_row = jnp.where(is_v, 4, kind + jnp.where(jb < JB_G0, 0, 2))

    def matmul():
        acc_ref[...] = jnp.dot(h_ref[...], w_ref[...], preferred_element_type=F32)

    @pl.when(j == 0)
    def _():
        x = x_ref[...]
        ms = jnp.mean(x * x, axis=-1, keepdims=True)
        h_ref[...] = (x * lax.rsqrt(ms + EPS) * g1_ref[...]).astype(BF16)
        matmul()

    def finished_head(hh, gain, cos, sin, low):
        xh = acc_ref[:, hh * HEAD_DIM:(hh + 1) * HEAD_DIM]
        ms = jnp.mean(xh * xh, axis=-1, keepdims=True)
        y = xh * jnp.where(is_v, 1.0, lax.rsqrt(ms + EPS)) * gain
        partner = jnp.where(low, pltpu.roll(y, HEAD_DIM - ROT_DIM // 2, 1), pltpu.roll(y, ROT_DIM // 2, 1))
        return y * cos + partner * sin

    def attention_block(store_head, after=None):
        gain = gains_ref[pl.ds(gain_row, 1), :]
        cos, sin = cos_ref[...], sin_ref[...]
        low = lax.broadcasted_iota(jnp.int32, cos.shape, 1) < ROT_DIM // 2
        for hh in range(nh):
            store_head(hh, finished_head(hh, gain, cos, sin, low))
        if after is not None:
            after()
        matmul()

    def to_main(hh, y):
        main_ref[:, hh * HEAD_DIM:(hh + 1) * HEAD_DIM] = y.astype(BF16)

    def to_scratch(hh, y):
        y_ref[hh] = y

    def deinterleave(dst_ref, dil):
        rows = y_ref.shape[1] // dil
        for ph in range(dil):
            for hh in range(nh):
                dst_ref[ph, :, hh * HEAD_DIM:(hh + 1) * HEAD_DIM] = (
                    y_ref[hh, pl.ds(ph, rows, stride=dil), :].astype(BF16))

    @pl.when((jb >= 0) & (jb < JB_G1))
    def _():
        attention_block(to_main)

    @pl.when((jb >= JB_G1) & (jb < JB_G2))
    def _():
        attention_block(to_scratch, lambda: deinterleave(d1_ref, DIL_CONFIGS[1][1]))

    @pl.when((jb >= JB_G2) & (jb < JB_GATE))
    def _():
        attention_block(to_scratch, lambda: deinterleave(d2_ref, DIL_CONFIGS[2][1]))

    def gates():
        gate_ref[...] = (0.5 * jnp.tanh(0.5 * acc_ref[...]) + 0.5).astype(BF16)

    @pl.when((jb >= JB_GATE) & (j < N_COL_BLK))
    def _():
        gates()
        matmul()

    @pl.when(j == N_COL_BLK)
    def _():
        gates()


def _inproj(x2, g1, w_in_bf, gains, cos_t, sin_t, B, T, tm):
    n = x2.shape[0]
    tpos = T // tm
    d1, d2 = DIL_CONFIGS[1][1], DIL_CONFIGS[2][1]
    rot = lambda j: jnp.where((j - 1 >= JB_G0) & (j - 1 < JB_GATE) & ((j - 1) % 3 != 2), 1, 0)
    tab = pl.BlockSpec((None, tm, HEAD_DIM), lambda i, j: (rot(j), i % tpos, 0))
    jb = lambda j: jnp.maximum(j - 1, 0)
    return pl.pallas_call(
        _inproj_kernel,
        grid=(n // tm, N_COL_BLK + 1),
        in_specs=[
            pl.BlockSpec((tm, D_MODEL), lambda i, j: (i, 0)),
            pl.BlockSpec((1, D_MODEL), lambda i, j: (0, 0)),
            pl.BlockSpec((D_MODEL, COL_BLK), lambda i, j: (0, jnp.minimum(j, N_COL_BLK - 1))),
            pl.BlockSpec((8, HEAD_DIM), lambda i, j: (0, 0)),
            tab, tab,
        ],
        out_specs=[
            pl.BlockSpec((tm, COL_BLK), lambda i, j: (i, jnp.minimum(jb(j), JB_G1 - 1))),
            pl.BlockSpec((None, d1, tm // d1, COL_BLK),
                         lambda i, j: (i // tpos, 0, i % tpos, jnp.clip(jb(j) - JB_G1, 0, 2))),
            pl.BlockSpec((None, d2, tm // d2, COL_BLK),
                         lambda i, j: (i // tpos, 0, i % tpos, jnp.clip(jb(j) - JB_G2, 0, 2))),
            pl.BlockSpec((tm, COL_BLK), lambda i, j: (i, jnp.clip(jb(j) - JB_GATE, 0, N_COL_BLK - JB_GATE - 1))),
        ],
        out_shape=[
            jax.ShapeDtypeStruct((n, JB_G1 * COL_BLK), BF16),
            jax.ShapeDtypeStruct((B, d1, T // d1, 3 * COL_BLK), BF16),
            jax.ShapeDtypeStruct((B, d2, T // d2, 3 * COL_BLK), BF16),
            jax.ShapeDtypeStruct((n, 2 * D_MODEL), BF16),
        ],
        scratch_shapes=[pltpu.VMEM((tm, D_MODEL), BF16), pltpu.VMEM((tm, COL_BLK), F32),
                        pltpu.VMEM((COL_BLK // HEAD_DIM, tm, HEAD_DIM), F32)],
        compiler_params=_cparams(("arbitrary", "arbitrary")),
        name="inproj",
    )(x2, g1, w_in_bf, gains, cos_t, sin_t)


NA_CHUNK_ROWS = 16
NA_HALO_ROWS = 4
NA_TQ = NA_CHUNK_ROWS * GRID_W
NA_TH = NA_HALO_ROWS * GRID_W
NA_GROUP = 4
NA_SPAN_ROWS = 12
NA_GQ = NA_GROUP * GRID_W
NA_SPAN = NA_SPAN_ROWS * GRID_W
NA_CASES = 3
assert NA_CHUNK_ROWS % NA_GROUP == 0 and NA_HALO_ROWS == NA_ROWS // 2
assert NA_ROWS + NA_GROUP - 1 <= NA_SPAN_ROWS <= NA_GROUP + 2 * NA_HALO_ROWS


def _natten_kernel(q_ref, kp_ref, km_ref, kn_ref, vp_ref, vm_ref, vn_ref, bias_ref, o_ref, kw_ref, vw_ref, *, rows):
    i = pl.program_id(1)
    kw_ref[0:NA_TH] = kp_ref[...]
    kw_ref[NA_TH:NA_TH + NA_TQ] = km_ref[...]
    kw_ref[NA_TH + NA_TQ:] = kn_ref[...]
    vw_ref[0:NA_TH] = vp_ref[...]
    vw_ref[NA_TH:NA_TH + NA_TQ] = vm_ref[...]
    vw_ref[NA_TH + NA_TQ:] = vn_ref[...]

    for gg in range(NA_CHUNK_ROWS // NA_GROUP):
        r0 = i * NA_CHUNK_ROWS + gg * NA_GROUP
        rs0 = jnp.clip(r0 - NA_ROWS // 2, 0, rows - NA_ROWS)
        case = jnp.where(r0 == 0, 0, jnp.where(r0 == rows - NA_GROUP, 2, 1))
        off = pl.multiple_of((rs0 - (i * NA_CHUNK_ROWS - NA_HALO_ROWS)) * GRID_W, GRID_W)
        for h in range(N_HEADS_NA):
            cols = slice(h * HEAD_DIM, (h + 1) * HEAD_DIM)
            q = q_ref[gg * NA_GQ:(gg + 1) * NA_GQ, cols]
            k = kw_ref[pl.ds(off, NA_SPAN), cols]
            v = vw_ref[pl.ds(off, NA_SPAN), cols]
            s = lax.dot_general(q, k, (((1,), (1,)), ((), ())), preferred_element_type=F32)
            s = s * SCALE + bias_ref[case, h]
            m = jnp.max(s, axis=-1, keepdims=True)
            p = jnp.exp(s - m)
            l = jnp.sum(p, axis=-1, keepdims=True)
            o = jnp.dot(p.astype(BF16), v, preferred_element_type=F32) / l
            o_ref[gg * NA_GQ:(gg + 1) * NA_GQ, cols] = o.astype(BF16)


def _natten(proj3, bias_tab, B, T):
    rows = T // GRID_W
    nh = T // NA_TH
    ratio = NA_TQ // NA_TH
    blk = lambda rws, fn: pl.BlockSpec((None, rws, COL_BLK), fn)
    return pl.pallas_call(
        functools.partial(_natten_kernel, rows=rows),
        grid=(B, T // NA_TQ),
        in_specs=[
            blk(NA_TQ, lambda b, i: (b, i, 0)),
            blk(NA_TH, lambda b, i: (b, jnp.maximum(i * ratio - 1, 0), 1)),
            blk(NA_TQ, lambda b, i: (b, i, 1)),
            blk(NA_TH, lambda b, i: (b, jnp.minimum((i + 1) * ratio, nh - 1), 1)),
            blk(NA_TH, lambda b, i: (b, jnp.maximum(i * ratio - 1, 0), 2)),
            blk(NA_TQ, lambda b, i: (b, i, 2)),
            blk(NA_TH, lambda b, i: (b, jnp.minimum((i + 1) * ratio, nh - 1), 2)),
            pl.BlockSpec((NA_CASES, N_HEADS_NA, NA_GQ, NA_SPAN), lambda b, i: (0, 0, 0, 0)),
        ],
        out_specs=pl.BlockSpec((None, NA_TQ, W_NA), lambda b, i: (b, i, 0)),
        out_shape=jax.ShapeDtypeStruct((B, T, W_NA), BF16),
        scratch_shapes=[pltpu.VMEM((NA_TQ + 2 * NA_TH, COL_BLK), BF16)] * 2,
        compiler_params=_cparams(("arbitrary", "arbitrary")),
        name="natten",
    )(proj3, proj3, proj3, proj3, proj3, proj3, proj3, bias_tab)


def _na_bias_table(rpb):
    c = np.arange(GRID_W)
    cs = np.clip(c - NA_COLS // 2, 0, GRID_W - NA_COLS)
    cc = np.arange(GRID_W)
    valid = (cc[None, :] >= cs[:, None]) & (cc[None, :] < cs[:, None] + NA_COLS)
    dc_idx = np.clip(cc[None, :] - c[:, None] + NA_COLS - 1, 0, 2 * NA_COLS - 2)
    pick = (dc_idx[..., None] == np.arange(2 * NA_COLS - 1)).astype(np.float32)
    t = jnp.einsum("qkd,hrd->hrqk", pick, rpb.astype(F32), precision=lax.Precision.HIGHEST)
    t = jnp.where(valid[None, None], t, NEG)
    half = NA_ROWS // 2
    cases = ([(m, 0) for m in range(NA_GROUP)],
             [(half, m) for m in range(NA_GROUP)],
             [(half + m, 0) for m in range(NA_GROUP)])
    tiles = []
    for case in cases:
        per_row = []
        for var, woff in case:
            win = t[:, NA_ROWS - 1 - var:2 * NA_ROWS - 1 - var]
            pad = ((0, 0), (woff, NA_SPAN_ROWS - NA_ROWS - woff), (0, 0), (0, 0))
            per_row.append(jnp.pad(win, pad, constant_values=NEG))
        tile = jnp.stack(per_row, axis=1)
        tiles.append(tile.transpose(0, 1, 3, 2, 4).reshape(N_HEADS_NA, NA_GQ, NA_SPAN))
    return jnp.stack(tiles, axis=0)


DIL_HALF = 64
DIL_KB = BAND_BLOCK + 2 * DIL_HALF
DIL_TOKENS_PER_STEP = (1024, 1024, 2048)


def _dilattn_kernel(q_ref, kp_ref, km_ref, kn_ref, vp_ref, vm_ref, vn_ref, o_ref, lse_ref,
                    kw_ref, vw_ref, os_ref, ls_ref, *, L, tq, dil):
    i = pl.program_id(1)
    kw_ref[:, 0:DIL_HALF] = kp_ref[...]
    kw_ref[:, DIL_HALF:DIL_HALF + tq] = km_ref[...]
    kw_ref[:, DIL_HALF + tq:] = kn_ref[...]
    vw_ref[:, 0:DIL_HALF] = vp_ref[...]
    vw_ref[:, DIL_HALF:DIL_HALF + tq] = vm_ref[...]
    vw_ref[:, DIL_HALF + tq:] = vn_ref[...]

    def phase(j):
        r_io = lax.broadcasted_iota(jnp.int32, (BAND_BLOCK, DIL_KB), 0)
        c_io = lax.broadcasted_iota(jnp.int32, (BAND_BLOCK, DIL_KB), 1)
        lane = lax.broadcasted_iota(jnp.int32, (BAND_BLOCK, LANES), 1)
        for s in range(tq // BAND_BLOCK):
            base = i * tq + s * BAND_BLOCK - DIL_HALF
            lo = jnp.maximum(r_io, -base)
            hi = jnp.minimum(r_io + 2 * DIL_HALF, L - 1 - base)
            valid = (c_io >= lo) & (c_io <= hi)
            out_rows = pl.ds(s * BAND_BLOCK * dil + j, BAND_BLOCK, stride=dil)
            lse_tile = jnp.zeros((BAND_BLOCK, LANES), F32)
            for h in range(N_HEADS_PER_DIL):
                cols = slice(h * HEAD_DIM, (h + 1) * HEAD_DIM)
                q = q_ref[j, s * BAND_BLOCK:(s + 1) * BAND_BLOCK, cols]
                k = kw_ref[j, s * BAND_BLOCK:s * BAND_BLOCK + DIL_KB, cols]
                v = vw_ref[j, s * BAND_BLOCK:s * BAND_BLOCK + DIL_KB, cols]
                sc = lax.dot_general(q, k, (((1,), (1,)), ((), ())), preferred_element_type=F32) * SCALE
                sc = jnp.where(valid, sc, NEG)
                m = jnp.max(sc, axis=-1, keepdims=True)
                p = jnp.exp(sc - m)
                l = jnp.sum(p, axis=-1, keepdims=True)
                os_ref[h, out_rows, :] = jnp.dot(p.astype(BF16), v, preferred_element_type=F32) / l
                lse_tile = jnp.where(lane == h, m + jnp.log(l), lse_tile)
            ls_ref[out_rows, :] = lse_tile

    if dil == 1:
        phase(0)
    else:
        def body(j, c):
            phase(j)
            return c
        lax.fori_loop(0, dil, body, 0, unroll=2)
    for h in range(N_HEADS_PER_DIL):
        o_ref[:, h * HEAD_DIM:(h + 1) * HEAD_DIM] = os_ref[h].astype(BF16)
    lse_ref[...] = ls_ref[...]


def _dilattn(qkv, col0, B, T, g):
    dil = DIL_CONFIGS[g][1]
    assert DIL_CONFIGS[g][0] // 2 // dil == DIL_HALF
    L = T // dil
    tq = min(DIL_TOKENS_PER_STEP[g] // dil, L)
    nh = L // DIL_HALF
    ratio = tq // DIL_HALF
    blk = lambda rws, fn: pl.BlockSpec((None, dil, rws, COL_BLK), fn)
    prev = lambda i: jnp.maximum(i * ratio - 1, 0)
    nxt = lambda i: jnp.minimum((i + 1) * ratio, nh - 1)
    o, lse = pl.pallas_call(
        functools.partial(_dilattn_kernel, L=L, tq=tq, dil=dil),
        grid=(B, L // tq),
        in_specs=[
            blk(tq, lambda b, i: (b, 0, i, col0)),
            blk(DIL_HALF, lambda b, i: (b, 0, prev(i), col0 + 1)),
            blk(tq, lambda b, i: (b, 0, i, col0 + 1)),
            blk(DIL_HALF, lambda b, i: (b, 0, nxt(i), col0 + 1)),
            blk(DIL_HALF, lambda b, i: (b, 0, prev(i), col0 + 2)),
            blk(tq, lambda b, i: (b, 0, i, col0 + 2)),
            blk(DIL_HALF, lambda b, i: (b, 0, nxt(i), col0 + 2)),
        ],
        out_specs=[
            pl.BlockSpec((None, tq * dil, W_GRP), lambda b, i: (b, i, 0)),
            pl.BlockSpec((None, tq * dil, LANES), lambda b, i: (b, i, 0)),
        ],
        out_shape=[
            jax.ShapeDtypeStruct((B, T, W_GRP), BF16),
            jax.ShapeDtypeStruct((B, T, LANES), F32),
        ],
        scratch_shapes=[pltpu.VMEM((dil, tq + 2 * DIL_HALF, COL_BLK), BF16)] * 2
        + [pltpu.VMEM((N_HEADS_PER_DIL, tq * dil, HEAD_DIM), F32), pltpu.VMEM((tq * dil, LANES), F32)],
        compiler_params=_cparams(("arbitrary", "arbitrary")),
        name=f"dilattn{g}",
    )(qkv, qkv, qkv, qkv, qkv, qkv, qkv)
    return o.reshape(B * T, W_GRP), lse.reshape(B * T, LANES)


def _outproj_kernel(a_ref, o0_ref, o1_ref, o2_ref, l0_ref, l1_ref, l2_ref, ga_ref, gb_ref, x_ref,
                    wa_ref, wb_ref, wo_ref, g2_ref, wr_ref, x1_ref, lg_ref):
    o_refs = (o0_ref, o1_ref, o2_ref)
    lses = [r[...] for r in (l0_ref, l1_ref, l2_ref)]
    parts = []
    for h in range(N_HEADS_PER_DIL):
        cols = slice(h * HEAD_DIM, (h + 1) * HEAD_DIM)
        lh = [l[:, h:h + 1] for l in lses]
        mx = jnp.maximum(jnp.maximum(lh[0], lh[1]), lh[2])
        e = [jnp.exp(v - mx) for v in lh]
        den = e[0] + e[1] + e[2]
        acc = None
        for g in range(len(DIL_CONFIGS)):
            t = (e[g] / den) * o_refs[g][:, cols].astype(F32)
            acc = t if acc is None else acc + t
        parts.append(acc.astype(BF16))
    att_b = jnp.concatenate(parts, axis=-1)
    ya = jnp.dot(a_ref[...], wa_ref[...], preferred_element_type=F32)
    yb = jnp.dot(att_b, wb_ref[...], preferred_element_type=F32)
    mix = ga_ref[...].astype(F32) * ya + gb_ref[...].astype(F32) * yb
    x1 = x_ref[...] + jnp.dot(mix.astype(BF16), wo_ref[...], preferred_element_type=F32)
    x1_ref[...] = x1
    ms = jnp.mean(x1 * x1, axis=-1, keepdims=True)
    h2 = x1 * lax.rsqrt(ms + EPS) * g2_ref[...]
    lg_ref[...] = jnp.dot(h2.astype(BF16), wr_ref[...], preferred_element_type=F32)


def _outproj(att_a, os_, lses, gates, x2, wa, wb, wo, g2, wr, tm):
    n = x2.shape[0]
    row = lambda w: pl.BlockSpec((tm, w), lambda i: (i, 0))
    full = lambda a: pl.BlockSpec(a.shape, lambda i: (0,) * a.ndim)
    return pl.pallas_call(
        _outproj_kernel,
        grid=(n // tm,),
        in_specs=[row(W_NA), row(W_GRP), row(W_GRP), row(W_GRP), row(LANES), row(LANES), row(LANES),
                  pl.BlockSpec((tm, D_MODEL), lambda i: (i, 0)),
                  pl.BlockSpec((tm, D_MODEL), lambda i: (i, 1)),
                  row(D_MODEL), full(wa), full(wb), full(wo), full(g2), full(wr)],
        out_specs=[row(D_MODEL), row(LANES)],
        out_shape=[jax.ShapeDtypeStruct((n, D_MODEL), F32), jax.ShapeDtypeStruct((n, LANES), F32)],
        compiler_params=_cparams(("arbitrary",)),
        name="outproj",
    )(att_a, *os_, *lses, gates, gates, x2, wa, wb, wo, g2, wr)


ROUTE_E0 = N_GROUPS


def _route_kernel(lg_ref, b_ref, gate_ref, eid_ref, rank_ref, cnt_ref, carry_ref):
    i = pl.program_id(0)

    @pl.when(i == 0)
    def _():
        carry_ref[...] = jnp.zeros_like(carry_ref)

    z = lg_ref[...] + b_ref[...]
    tm = z.shape[0]
    lane = lax.broadcasted_iota(jnp.int32, z.shape, 1)
    lane_f = lane.astype(F32)
    rmax = lambda v: jnp.max(v, axis=-1, keepdims=True)
    rmin = lambda v: jnp.min(v, axis=-1, keepdims=True)
    rsum = lambda v: jnp.sum(v, axis=-1, keepdims=True)

    gmask = lane < N_GROUPS
    gmax = rmax(jnp.where(gmask, z, -jnp.inf))
    gsel = rmin(jnp.where(gmask & (z == gmax), lane_f, float(LANES)))
    gp = 1.0 / rsum(jnp.where(gmask, jnp.exp(z - gmax), 0.0))

    elo = float(ROUTE_E0) + float(EXPERTS_PER_GROUP) * gsel
    emask = (lane_f >= elo) & (lane_f < elo + float(EXPERTS_PER_GROUP))
    emax = rmax(jnp.where(emask, z, -jnp.inf))
    ee = jnp.where(emask, jnp.exp(z - emax), 0.0)
    eprob = ee / rsum(ee)
    p1 = rmax(jnp.where(emask, eprob, -1.0))
    i1 = rmin(jnp.where(emask & (eprob == p1), lane_f, float(LANES)))
    m2 = emask & (lane_f != i1)
    p2 = rmax(jnp.where(m2, eprob, -1.0))
    i2 = rmin(jnp.where(m2 & (eprob == p2), lane_f, float(LANES)))
    den = p1 + p2
    g1 = gp * p1 / den
    g2 = gp * p2 / den

    oh1 = (lane_f == i1).astype(F32)
    oh2 = (lane_f == i2).astype(F32)
    oh = oh1 + oh2
    tri = (lax.broadcasted_iota(jnp.int32, (tm, tm), 1) < lax.broadcasted_iota(jnp.int32, (tm, tm), 0)).astype(BF16)
    tot = carry_ref[...] + jnp.dot(tri, oh.astype(BF16), preferred_element_type=F32)
    r1 = rsum(oh1 * tot)
    r2 = rsum(oh2 * tot)
    carry = carry_ref[...] + jnp.sum(oh, axis=0, keepdims=True)
    carry_ref[...] = carry
    cnt_ref[...] = carry

    two = lambda a, b: jnp.where(lane == 0, a, jnp.where(lane == 1, b, 0.0))
    gate_ref[...] = two(g1, g2)
    eid_ref[...] = two(i1 - float(ROUTE_E0), i2 - float(ROUTE_E0)).astype(jnp.int32)
    rank_ref[...] = two(r1, r2).astype(jnp.int32)


def _route(logits, rbias, tm):
    n = logits.shape[0]
    row = pl.BlockSpec((tm, LANES), lambda i: (i, 0))
    one = pl.BlockSpec((1, LANES), lambda i: (0, 0))
    return pl.pallas_call(
        _route_kernel,
        grid=(n // tm,),
        in_specs=[row, one],
        out_specs=[row, row, row, one],
        out_shape=[jax.ShapeDtypeStruct((n, LANES), F32), jax.ShapeDtypeStruct((n, LANES), jnp.int32),
                   jax.ShapeDtypeStruct((n, LANES), jnp.int32), jax.ShapeDtypeStruct((1, LANES), F32)],
        scratch_shapes=[pltpu.VMEM((1, LANES), F32)],
        compiler_params=_cparams(("arbitrary",)),
        name="route",
    )(logits, rbias)


TOP_K = 2


ROW_SUB = 8
HALF_D = D_MODEL // 2
assert ROW_SUB * LANES == HALF_D


def _pack_pair(lo, hi):
    lo_b = lax.bitcast_convert_type(lo.astype(BF16).astype(F32), jnp.uint32)
    hi_b = lax.bitcast_convert_type(hi.astype(BF16).astype(F32), jnp.uint32)
    return (lo_b >> 16) | hi_b


def _unpack_pair(w):
    lo = lax.bitcast_convert_type(w << 16, F32)
    hi = lax.bitcast_convert_type(w & jnp.uint32(0xFFFF0000), F32)
    return lo, hi


def _store_packed_rows(dst, x):
    for c in range(ROW_SUB):
        dst(c, _pack_pair(x[:, c * LANES:(c + 1) * LANES], x[:, HALF_D + c * LANES:HALF_D + (c + 1) * LANES]))


def _dispatch_kernel(pos_ref, x1_ref, g2_ref, xs_ref, hp_ref, sem):
    i = pl.program_id(0)
    last = pl.num_programs(0) - 1
    slot = i % 2
    x1 = x1_ref[...]
    tm = x1.shape[0]
    ms = jnp.mean(x1 * x1, axis=-1, keepdims=True)
    h = x1 * lax.rsqrt(ms + EPS) * g2_ref[...]

    def put(c, words):
        hp_ref[slot, pl.ds(c, tm, stride=ROW_SUB), :] = words

    _store_packed_rows(put, h)

    def row_copy(sl, r, p):
        src = hp_ref.at[sl, pl.ds(pl.multiple_of(r * ROW_SUB, ROW_SUB), ROW_SUB)]
        dst = xs_ref.at[pl.ds(pl.multiple_of(p * ROW_SUB, ROW_SUB), ROW_SUB)]
        return pltpu.make_async_copy(src, dst, sem.at[sl])

    def start(r, c):
        for k in range(TOP_K):
            row_copy(slot, r, pos_ref[0, TOP_K * r + k]).start(priority=k)
        return c

    def wait_all(sl):
        def wait(r, c):
            for k in range(TOP_K):
                row_copy(sl, 0, 0).wait()
            return c
        lax.fori_loop(0, tm, wait, 0)

    lax.fori_loop(0, tm, start, 0, unroll=4)

    @pl.when(i > 0)
    def _():
        wait_all(1 - slot)

    @pl.when(i == last)
    def _():
        wait_all(slot)


def _dispatch(pos3, x1, g2, tm):
    n = x1.shape[0]
    return pl.pallas_call(
        _dispatch_kernel,
        grid=(n // tm,),
        in_specs=[pl.BlockSpec((None, 1, TOP_K * tm), lambda i: (i, 0, 0), memory_space=pltpu.SMEM),
                  pl.BlockSpec((tm, D_MODEL), lambda i: (i, 0)),
                  pl.BlockSpec((1, D_MODEL), lambda i: (0, 0))],
        out_specs=pl.BlockSpec(memory_space=pl.ANY),
        out_shape=jax.ShapeDtypeStruct((n * TOP_K * ROW_SUB, LANES), jnp.uint32),
        scratch_shapes=[pltpu.VMEM((2, tm * ROW_SUB, LANES), jnp.uint32), pltpu.SemaphoreType.DMA((2,))],
        compiler_params=_cparams(("arbitrary",)),
        name="dispatch",
    )(pos3, x1, g2)


def _combine_kernel(pos_ref, posn_ref, gate_ref, x1_ref, ys_ref, y_ref, buf_ref, sem):
    i = pl.program_id(0)
    last = pl.num_programs(0) - 1
    slot = i % 2
    tm = x1_ref.shape[0]

    def row_copy(sl, k, r, p):
        src = ys_ref.at[pl.ds(pl.multiple_of(p * ROW_SUB, ROW_SUB), ROW_SUB)]
        dst = buf_ref.at[sl, k, pl.ds(pl.multiple_of(r * ROW_SUB, ROW_SUB), ROW_SUB)]
        return pltpu.make_async_copy(src, dst, sem.at[sl])

    def start_all(sl, p_ref):
        def start(r, c):
            for k in range(TOP_K):
                row_copy(sl, k, r, p_ref[0, TOP_K * r + k]).start(priority=k)
            return c
        lax.fori_loop(0, tm, start, 0, unroll=4)

    @pl.when(i == 0)
    def _():
        start_all(0, pos_ref)

    @pl.when(i < last)
    def _():
        start_all(1 - slot, posn_ref)

    def wait(r, c):
        for k in range(TOP_K):
            row_copy(slot, k, 0, 0).wait()
        return c

    lax.fori_loop(0, tm, wait, 0)
    gates = gate_ref[...]
    g0 = jnp.broadcast_to(gates[:, 0:1], (tm, LANES))
    g1 = jnp.broadcast_to(gates[:, 1:2], (tm, LANES))
    for c in range(ROW_SUB):
        lo0, hi0 = _unpack_pair(buf_ref[slot, 0, pl.ds(c, tm, stride=ROW_SUB), :])
        lo1, hi1 = _unpack_pair(buf_ref[slot, 1, pl.ds(c, tm, stride=ROW_SUB), :])
        cl = slice(c * LANES, (c + 1) * LANES)
        ch = slice(HALF_D + c * LANES, HALF_D + (c + 1) * LANES)
        y_ref[:, cl] = x1_ref[:, cl] + (g0 * lo0 + g1 * lo1)
        y_ref[:, ch] = x1_ref[:, ch] + (g0 * hi0 + g1 * hi1)


def _combine(pos3, gates, x1, ys, tm):
    n = x1.shape[0]
    nt = n // tm
    pos_blk = lambda fn: pl.BlockSpec((None, 1, TOP_K * tm), fn, memory_space=pltpu.SMEM)
    return pl.pallas_call(
        _combine_kernel,
        grid=(nt,),
        in_specs=[pos_blk(lambda i: (i, 0, 0)),
                  pos_blk(lambda i: (jnp.minimum(i + 1, nt - 1), 0, 0)),
                  pl.BlockSpec((tm, LANES), lambda i: (i, 0)),
                  pl.BlockSpec((tm, D_MODEL), lambda i: (i, 0)),
                  pl.BlockSpec(memory_space=pl.ANY)],
        out_specs=pl.BlockSpec((tm, D_MODEL), lambda i: (i, 0)),
        out_shape=jax.ShapeDtypeStruct((n, D_MODEL), F32),
        scratch_shapes=[pltpu.VMEM((2, TOP_K, tm * ROW_SUB, LANES), jnp.uint32), pltpu.SemaphoreType.DMA((2,))],
        compiler_params=_cparams(("arbitrary",)),
        name="combine",
    )(pos3, pos3, gates, x1, ys)


def _experts_kernel(tile_s, exp_s, lo_s, hi_s, first_s, x_ref, wg_ref, wu_ref, wd_ref, y_ref, xb_ref):
    w = pl.program_id(0)
    lo, hi = lo_s[w], hi_s[w]
    bm = xb_ref.shape[0]

    def work_item(first_visit):
        for c in range(ROW_SUB):
            xl, xh = _unpack_pair(x_ref[pl.ds(c, bm, stride=ROW_SUB), :])
            xb_ref[:, c * LANES:(c + 1) * LANES] = xl.astype(BF16)
            xb_ref[:, HALF_D + c * LANES:HALF_D + (c + 1) * LANES] = xh.astype(BF16)
        x = xb_ref[...]
        a = jnp.dot(x, wg_ref[...], preferred_element_type=F32)
        u = jnp.dot(x, wu_ref[...], preferred_element_type=F32)
        hm = (a * jax.nn.sigmoid(a) * u).astype(BF16)
        y = jnp.dot(hm, wd_ref[...], preferred_element_type=F32)
        rows = lax.broadcasted_iota(jnp.int32, (bm, LANES), 0)
        mine = (rows >= lo) & (rows < hi)

        def put(c, words):
            rws = pl.ds(c, bm, stride=ROW_SUB)
            y_ref[rws, :] = jnp.where(mine, words, jnp.uint32(0) if first_visit else y_ref[rws, :])
        _store_packed_rows(put, y)

    @pl.when((hi > lo) & (first_s[w] == 1))
    def _():
        work_item(True)

    @pl.when((hi > lo) & (first_s[w] == 0))
    def _():
        work_item(False)


def _experts(meta, xs, wg, wu, wd, bm):
    p = xs.shape[0] // ROW_SUB
    nw = p // bm + N_EXPERTS
    grid_spec = pltpu.PrefetchScalarGridSpec(
        num_scalar_prefetch=5,
        grid=(nw,),
        in_specs=[
            pl.BlockSpec((bm * ROW_SUB, LANES), lambda w, t, e, lo, hi, f: (t[w], 0)),
            pl.BlockSpec((None, D_MODEL, D_EXPERT), lambda w, t, e, lo, hi, f: (e[w], 0, 0)),
            pl.BlockSpec((None, D_MODEL, D_EXPERT), lambda w, t, e, lo, hi, f: (e[w], 0, 0)),
            pl.BlockSpec((None, D_EXPERT, D_MODEL), lambda w, t, e, lo, hi, f: (e[w], 0, 0)),
        ],
        out_specs=pl.BlockSpec((bm * ROW_SUB, LANES), lambda w, t, e, lo, hi, f: (t[w], 0)),
        scratch_shapes=[pltpu.VMEM((bm, D_MODEL), BF16)],
    )
    return pl.pallas_call(
        _experts_kernel,
        grid_spec=grid_spec,
        out_shape=jax.ShapeDtypeStruct((p * ROW_SUB, LANES), jnp.uint32),
        compiler_params=_cparams(("arbitrary",)),
        name="experts",
    )(*meta, xs, wg, wu, wd)


def _expert_work_items(counts, p, bm):
    ntiles = p // bm
    nw = ntiles + N_EXPERTS
    ends = jnp.cumsum(counts)
    starts = ends - counts
    ft = starts // bm
    nt = jnp.where(counts > 0, (ends - 1) // bm - ft + 1, 0)
    wend = jnp.cumsum(nt)
    wstart = wend - nt
    w = jnp.arange(nw, dtype=jnp.int32)
    used = w < wend[-1]
    e = jnp.minimum(jnp.sum((w[:, None] >= wend[None, :]).astype(jnp.int32), axis=1), N_EXPERTS - 1)
    e_last = jnp.max(jnp.where(counts > 0, jnp.arange(N_EXPERTS, dtype=jnp.int32), 0))
    e = jnp.where(used, e, e_last)
    tile = jnp.where(used, ft[e] + (w - wstart[e]), ntiles - 1).astype(jnp.int32)
    lo = jnp.where(used, jnp.maximum(starts[e], tile * bm) - tile * bm, 0).astype(jnp.int32)
    hi = jnp.where(used, jnp.minimum(ends[e], (tile + 1) * bm) - tile * bm, 0).astype(jnp.int32)
    first = jnp.concatenate([jnp.ones((1,), jnp.int32), (tile[1:] != tile[:-1]).astype(jnp.int32)])
    return (tile, e, lo, hi, first), starts


def _rope_tables(T):
    half = ROT_DIM // 2
    inv = 1.0 / (ROPE_THETA ** (np.arange(half, dtype=np.float64) * (2.0 / ROT_DIM)))
    ang = np.arange(T, dtype=np.float64)[:, None] * inv[None, :]
    cos, sin = np.cos(ang), np.sin(ang)
    rest = HEAD_DIM - ROT_DIM
    z_half, z_rest = np.zeros((T, half)), np.zeros((T, rest))
    cos_t = np.concatenate([cos, cos, np.ones((T, rest))], axis=1)
    sin_t = np.concatenate([-sin, sin, z_rest], axis=1)
    cos_t = np.stack([np.ones_like(cos_t), cos_t])
    sin_t = np.stack([np.zeros_like(sin_t), sin_t])
    return tuple(jnp.asarray(t, dtype=F32) for t in (cos_t, sin_t))


def _layer(x, p):
    B, T, D = x.shape
    n = B * T
    assert D == D_MODEL and T % NA_TQ == 0 and T // GRID_W >= NA_ROWS + NA_HALO_ROWS
    x2 = x.reshape(n, D)
    tm_in = 1024
    tm_out = 256
    tm_route = 512
    tm_move = 512
    bm = 256

    main, qkv1, qkv2, sig = _inproj(x2, p["g1"], p["w_in"], p["gains"], *_rope_tables(T), B, T, tm_in)
    main3 = main.reshape(B, T, JB_G1 * COL_BLK)
    att_a = _natten(main3, p["na_bias"], B, T).reshape(n, W_NA)
    dil = [_dilattn(main3[:, None], JB_G0, B, T, 0), _dilattn(qkv1, 0, B, T, 1), _dilattn(qkv2, 0, B, T, 2)]
    x1, logits = _outproj(att_a, [d[0] for d in dil], [d[1] for d in dil], sig, x2,
                          p["wa"], p["wb"], p["wo"], p["g2"], p["wr"], tm_out)

    gates, eid, rank, cnt = _route(logits, p["rbias"], tm_route)
    counts = cnt[0, ROUTE_E0:ROUTE_E0 + N_EXPERTS].astype(jnp.int32)
    meta, starts = _expert_work_items(counts, n * TOP_K, bm)
    is_e = eid[:, :TOP_K, None] == jnp.arange(N_EXPERTS, dtype=jnp.int32)
    pos = jnp.sum(jnp.where(is_e, starts, 0), axis=-1) + rank[:, :TOP_K]
    pos3 = pos.astype(jnp.int32).reshape(n // tm_move, 1, TOP_K * tm_move)

    xs = _dispatch(pos3, x1, p["g2"], tm_move)
    ys = _experts(meta, xs, p["wg"], p["wu"], p["wd"], bm)
    y = _combine(pos3, gates, x1, ys, tm_move)
    return y.reshape(B, T, D)


def _prepare_params(norm1_g, w_in, qn_a, kn_a, rpb_a, qn_b, kn_b, w_branch_a, w_branch_b, w_out,
                    norm2_g, router_group_w, router_group_b, router_expert_w, router_expert_b, w_gate, w_up, w_down):
    assert norm1_g.shape[0] == 1
    l = 0
    pad_r = LANES - N_GROUPS - N_EXPERTS
    zrow = jnp.concatenate([jnp.ones((1, HEAD_DIM), F32), jnp.zeros((3, HEAD_DIM), F32)], axis=0)
    col = lambda start, width: w_in[l][:, start:start + width]
    qb0, kb0, vb0 = 3 * W_NA, 3 * W_NA + W_DIL, 3 * W_NA + 2 * W_DIL
    pieces = [col(0, 3 * W_NA)]
    for g in range(len(DIL_CONFIGS)):
        pieces += [col(qb0 + g * W_GRP, W_GRP), col(kb0 + g * W_GRP, W_GRP), col(vb0 + g * W_GRP, W_GRP)]
    pieces.append(col(3 * W_NA + 3 * W_DIL, 2 * D_MODEL))
    return {
        "g1": norm1_g[l][None, :],
        "w_in": jnp.concatenate(pieces, axis=1).astype(BF16),
        "gains": jnp.concatenate([qn_a[l][None], kn_a[l][None], qn_b[l][None], kn_b[l][None], zrow], axis=0),
        "na_bias": _na_bias_table(rpb_a[l]),
        "wa": w_branch_a[l].astype(BF16),
        "wb": w_branch_b[l].astype(BF16),
        "wo": w_out[l].astype(BF16),
        "g2": norm2_g[l][None, :],
        "wr": jnp.concatenate([router_group_w[l], router_expert_w[l], jnp.zeros((D_MODEL, pad_r), F32)],
                              axis=1).astype(BF16),
        "rbias": jnp.concatenate([router_group_b[l], router_expert_b[l], jnp.zeros((pad_r,), F32)])[None, :],
        "wg": w_gate[l].astype(BF16),
        "wu": w_up[l].astype(BF16),
        "wd": w_down[l].astype(BF16),
    }


def kernel(x_prompt, x_sample, norm1_g, w_in, qn_a, kn_a, rpb_a, qn_b, kn_b, w_branch_a, w_branch_b, w_out,
           norm2_g, router_group_w, router_group_b, router_expert_w, router_expert_b, w_gate, w_up, w_down):
    p = _prepare_params(norm1_g, w_in, qn_a, kn_a, rpb_a, qn_b, kn_b, w_branch_a, w_branch_b, w_out, norm2_g,
                        router_group_w, router_group_b, router_expert_w, router_expert_b, w_gate, w_up, w_down)
    return (_layer(x_prompt, p), _layer(x_sample, p))
```

```python
import functools

import jax
import jax.numpy as jnp
import numpy as np
from jax import lax
from jax.experimental import pallas as pl
from jax.experimental.pallas import tpu as pltpu

F32 = jnp.float32
BF16 = jnp.bfloat16

D_MODEL = 2048
HEAD_DIM = 128
N_HEADS_NA = 4
N_HEADS_PER_DIL = 4
DIL_CONFIGS = ((128, 1), (512, 4), (2048, 16))
W_NA = N_HEADS_NA * HEAD_DIM
W_DIL = N_HEADS_PER_DIL * len(DIL_CONFIGS) * HEAD_DIM
W_GRP = N_HEADS_PER_DIL * HEAD_DIM
GRID_W = 64
NA_ROWS = 8
NA_COLS = 16
ROT_DIM = HEAD_DIM // 4
ROPE_THETA = 500000.0
BAND_BLOCK = 128
N_GROUPS = 4
EXPERTS_PER_GROUP = 8
N_EXPERTS = N_GROUPS * EXPERTS_PER_GROUP
D_EXPERT = 1024
EPS = 1e-6
NEG = -1e30
IN_WIDTH = 3 * W_NA + 3 * W_DIL + 2 * D_MODEL
SCALE = HEAD_DIM ** -0.5

COL_BLK = 512
JB_G0, JB_G1, JB_G2, JB_GATE = 3, 6, 9, 12
N_COL_BLK = IN_WIDTH // COL_BLK

LANES = 128
VMEM_LIMIT = 56 * 1024 * 1024


def _cparams(sem, vmem=VMEM_LIMIT):
    return pltpu.CompilerParams(dimension_semantics=sem, vmem_limit_bytes=vmem)


def _inproj_kernel(x_ref, g1_ref, w_ref, gains_ref, cos_ref, sin_ref,
                   main_ref, d1_ref, d2_ref, gate_ref, h_ref, acc_ref, y_ref):
    j = pl.program_id(1)
    jb = j - 1
    nh = COL_BLK // HEAD_DIM
    kind = jb % 3
    is_v = kind == 2
    gain_row = jnp.where(is_v, 4, kind + jnp.where(jb < JB_G0, 0, 2))

    def matmul():
        acc_ref[...] = jnp.dot(h_ref[...], w_ref[...], preferred_element_type=F32)

    @pl.when(j == 0)
    def _():
        x = x_ref[...]
        ms = jnp.mean(x * x, axis=-1, keepdims=True)
        h_ref[...] = (x * lax.rsqrt(ms + EPS) * g1_ref[...]).astype(BF16)
        matmul()

    def finished_head(hh, gain, cos, sin, low):
        xh = acc_ref[:, hh * HEAD_DIM:(hh + 1) * HEAD_DIM]
        ms = jnp.mean(xh * xh, axis=-1, keepdims=True)
        y = xh * jnp.where(is_v, 1.0, lax.rsqrt(ms + EPS)) * gain
        partner = jnp.where(low, pltpu.roll(y, HEAD_DIM - ROT_DIM // 2, 1), pltpu.roll(y, ROT_DIM // 2, 1))
        return y * cos + partner * sin

    def attention_block(store_head, after=None):
        gain = gains_ref[pl.ds(gain_row, 1), :]
        cos, sin = cos_ref[...], sin_ref[...]
        low = lax.broadcasted_iota(jnp.int32, cos.shape, 1) < ROT_DIM // 2
        for hh in range(nh):
            store_head(hh, finished_head(hh, gain, cos, sin, low))
        if after is not None:
            after()
        matmul()

    def to_main(hh, y):
        main_ref[:, hh * HEAD_DIM:(hh + 1) * HEAD_DIM] = y.astype(BF16)

    def to_scratch(hh, y):
        y_ref[hh] = y

    def deinterleave(dst_ref, dil):
        rows = y_ref.shape[1] // dil
        for ph in range(dil):
            for hh in range(nh):
                dst_ref[ph, :, hh * HEAD_DIM:(hh + 1) * HEAD_DIM] = (
                    y_ref[hh, pl.ds(ph, rows, stride=dil), :].astype(BF16))

    @pl.when((jb >= 0) & (jb < JB_G1))
    def _():
        attention_block(to_main)

    @pl.when((jb >= JB_G1) & (jb < JB_G2))
    def _():
        attention_block(to_scratch, lambda: deinterleave(d1_ref, DIL_CONFIGS[1][1]))

    @pl.when((jb >= JB_G2) & (jb < JB_GATE))
    def _():
        attention_block(to_scratch, lambda: deinterleave(d2_ref, DIL_CONFIGS[2][1]))

    def gates():
        gate_ref[...] = (0.5 * jnp.tanh(0.5 * acc_ref[...]) + 0.5).astype(BF16)

    @pl.when((jb >= JB_GATE) & (j < N_COL_BLK))
    def _():
        gates()
        matmul()

    @pl.when(j == N_COL_BLK)
    def _():
        gates()


def _inproj(x2, g1, w_in_bf, gains, cos_t, sin_t, B, T, tm):
    n = x2.shape[0]
    tpos = T // tm
    d1, d2 = DIL_CONFIGS[1][1], DIL_CONFIGS[2][1]
    rot = lambda j: jnp.where((j - 1 >= JB_G0) & (j - 1 < JB_GATE) & ((j - 1) % 3 != 2), 1, 0)
    tab = pl.BlockSpec((None, tm, HEAD_DIM), lambda i, j: (rot(j), i % tpos, 0))
    jb = lambda j: jnp.maximum(j - 1, 0)
    return pl.pallas_call(
        _inproj_kernel,
        grid=(n // tm, N_COL_BLK + 1),
        in_specs=[
            pl.BlockSpec((tm, D_MODEL), lambda i, j: (i, 0)),
            pl.BlockSpec((1, D_MODEL), lambda i, j: (0, 0)),
            pl.BlockSpec((D_MODEL, COL_BLK), lambda i, j: (0, jnp.minimum(j, N_COL_BLK - 1))),
            pl.BlockSpec((8, HEAD_DIM), lambda i, j: (0, 0)),
            tab, tab,
        ],
        out_specs=[
            pl.BlockSpec((tm, COL_BLK), lambda i, j: (i, jnp.minimum(jb(j), JB_G1 - 1))),
            pl.BlockSpec((None, d1, tm // d1, COL_BLK),
                         lambda i, j: (i // tpos, 0, i % tpos, jnp.clip(jb(j) - JB_G1, 0, 2))),
            pl.BlockSpec((None, d2, tm // d2, COL_BLK),
                         lambda i, j: (i // tpos, 0, i % tpos, jnp.clip(jb(j) - JB_G2, 0, 2))),
            pl.BlockSpec((tm, COL_BLK), lambda i, j: (i, jnp.clip(jb(j) - JB_GATE, 0, N_COL_BLK - JB_GATE - 1))),
        ],
        out_shape=[
            jax.ShapeDtypeStruct((n, JB_G1 * COL_BLK), BF16),
            jax.ShapeDtypeStruct((B, d1, T // d1, 3 * COL_BLK), BF16),
            jax.ShapeDtypeStruct((B, d2, T // d2, 3 * COL_BLK), BF16),
            jax.ShapeDtypeStruct((n, 2 * D_MODEL), BF16),
        ],
        scratch_shapes=[pltpu.VMEM((tm, D_MODEL), BF16), pltpu.VMEM((tm, COL_BLK), F32),
                        pltpu.VMEM((COL_BLK // HEAD_DIM, tm, HEAD_DIM), F32)],
        compiler_params=_cparams(("arbitrary", "arbitrary")),
        name="inproj",
    )(x2, g1, w_in_bf, gains, cos_t, sin_t)


NA_CHUNK_ROWS = 16
NA_HALO_ROWS = 4
NA_TQ = NA_CHUNK_ROWS * GRID_W
NA_TH = NA_HALO_ROWS * GRID_W
NA_GROUP = 4
NA_SPAN_ROWS = 12
NA_GQ = NA_GROUP * GRID_W
NA_SPAN = NA_SPAN_ROWS * GRID_W
NA_CASES = 3
assert NA_CHUNK_ROWS % NA_GROUP == 0 and NA_HALO_ROWS == NA_ROWS // 2
assert NA_ROWS + NA_GROUP - 1 <= NA_SPAN_ROWS <= NA_GROUP + 2 * NA_HALO_ROWS


def _natten_kernel(q_ref, kp_ref, km_ref, kn_ref, vp_ref, vm_ref, vn_ref, bias_ref, o_ref, kw_ref, vw_ref, *, rows):
    i = pl.program_id(1)
    kw_ref[0:NA_TH] = kp_ref[...]
    kw_ref[NA_TH:NA_TH + NA_TQ] = km_ref[...]
    kw_ref[NA_TH + NA_TQ:] = kn_ref[...]
    vw_ref[0:NA_TH] = vp_ref[...]
    vw_ref[NA_TH:NA_TH + NA_TQ] = vm_ref[...]
    vw_ref[NA_TH + NA_TQ:] = vn_ref[...]

    for gg in range(NA_CHUNK_ROWS // NA_GROUP):
        r0 = i * NA_CHUNK_ROWS + gg * NA_GROUP
        rs0 = jnp.clip(r0 - NA_ROWS // 2, 0, rows - NA_ROWS)
        case = jnp.where(r0 == 0, 0, jnp.where(r0 == rows - NA_GROUP, 2, 1))
        off = pl.multiple_of((rs0 - (i * NA_CHUNK_ROWS - NA_HALO_ROWS)) * GRID_W, GRID_W)
        for h in range(N_HEADS_NA):
            cols = slice(h * HEAD_DIM, (h + 1) * HEAD_DIM)
            q = q_ref[gg * NA_GQ:(gg + 1) * NA_GQ, cols]
            k = kw_ref[pl.ds(off, NA_SPAN), cols]
            v = vw_ref[pl.ds(off, NA_SPAN), cols]
            s = lax.dot_general(q, k, (((1,), (1,)), ((), ())), preferred_element_type=F32)
            s = s * SCALE + bias_ref[case, h]
            m = jnp.max(s, axis=-1, keepdims=True)
            p = jnp.exp(s - m)
            l = jnp.sum(p, axis=-1, keepdims=True)
            o = jnp.dot(p.astype(BF16), v, preferred_element_type=F32) / l
            o_ref[gg * NA_GQ:(gg + 1) * NA_GQ, cols] = o.astype(BF16)


def _natten(proj3, bias_tab, B, T):
    rows = T // GRID_W
    nh = T // NA_TH
    ratio = NA_TQ // NA_TH
    blk = lambda rws, fn: pl.BlockSpec((None, rws, COL_BLK), fn)
    return pl.pallas_call(
        functools.partial(_natten_kernel, rows=rows),
        grid=(B, T // NA_TQ),
        in_specs=[
            blk(NA_TQ, lambda b, i: (b, i, 0)),
            blk(NA_TH, lambda b, i: (b, jnp.maximum(i * ratio - 1, 0), 1)),
            blk(NA_TQ, lambda b, i: (b, i, 1)),
            blk(NA_TH, lambda b, i: (b, jnp.minimum((i + 1) * ratio, nh - 1), 1)),
            blk(NA_TH, lambda b, i: (b, jnp.maximum(i * ratio - 1, 0), 2)),
            blk(NA_TQ, lambda b, i: (b, i, 2)),
            blk(NA_TH, lambda b, i: (b, jnp.minimum((i + 1) * ratio, nh - 1), 2)),
            pl.BlockSpec((NA_CASES, N_HEADS_NA, NA_GQ, NA_SPAN), lambda b, i: (0, 0, 0, 0)),
        ],
        out_specs=pl.BlockSpec((None, NA_TQ, W_NA), lambda b, i: (b, i, 0)),
        out_shape=jax.ShapeDtypeStruct((B, T, W_NA), BF16),
        scratch_shapes=[pltpu.VMEM((NA_TQ + 2 * NA_TH, COL_BLK), BF16)] * 2,
        compiler_params=_cparams(("arbitrary", "arbitrary")),
        name="natten",
    )(proj3, proj3, proj3, proj3, proj3, proj3, proj3, bias_tab)


def _na_bias_table(rpb):
    c = np.arange(GRID_W)
    cs = np.clip(c - NA_COLS // 2, 0, GRID_W - NA_COLS)
    cc = np.arange(GRID_W)
    valid = (cc[None, :] >= cs[:, None]) & (cc[None, :] < cs[:, None] + NA_COLS)
    dc_idx = np.clip(cc[None, :] - c[:, None] + NA_COLS - 1, 0, 2 * NA_COLS - 2)
    pick = (dc_idx[..., None] == np.arange(2 * NA_COLS - 1)).astype(np.float32)
    t = jnp.einsum("qkd,hrd->hrqk", pick, rpb.astype(F32), precision=lax.Precision.HIGHEST)
    t = jnp.where(valid[None, None], t, NEG)
    half = NA_ROWS // 2
    cases = ([(m, 0) for m in range(NA_GROUP)],
             [(half, m) for m in range(NA_GROUP)],
             [(half + m, 0) for m in range(NA_GROUP)])
    tiles = []
    for case in cases:
        per_row = []
        for var, woff in case:
            win = t[:, NA_ROWS - 1 - var:2 * NA_ROWS - 1 - var]
            pad = ((0, 0), (woff, NA_SPAN_ROWS - NA_ROWS - woff), (0, 0), (0, 0))
            per_row.append(jnp.pad(win, pad, constant_values=NEG))
        tile = jnp.stack(per_row, axis=1)
        tiles.append(tile.transpose(0, 1, 3, 2, 4).reshape(N_HEADS_NA, NA_GQ, NA_SPAN))
    return jnp.stack(tiles, axis=0)


DIL_HALF = 64
DIL_KB = BAND_BLOCK + 2 * DIL_HALF
DIL_TOKENS_PER_STEP = (1024, 1024, 2048)


def _dilattn_kernel(q_ref, kp_ref, km_ref, kn_ref, vp_ref, vm_ref, vn_ref, o_ref, lse_ref,
                    kw_ref, vw_ref, os_ref, ls_ref, *, L, tq, dil):
    i = pl.program_id(1)
    kw_ref[:, 0:DIL_HALF] = kp_ref[...]
    kw_ref[:, DIL_HALF:DIL_HALF + tq] = km_ref[...]
    kw_ref[:, DIL_HALF + tq:] = kn_ref[...]
    vw_ref[:, 0:DIL_HALF] = vp_ref[...]
    vw_ref[:, DIL_HALF:DIL_HALF + tq] = vm_ref[...]
    vw_ref[:, DIL_HALF + tq:] = vn_ref[...]

    def phase(j):
        r_io = lax.broadcasted_iota(jnp.int32, (BAND_BLOCK, DIL_KB), 0)
        c_io = lax.broadcasted_iota(jnp.int32, (BAND_BLOCK, DIL_KB), 1)
        lane = lax.broadcasted_iota(jnp.int32, (BAND_BLOCK, LANES), 1)
        for s in range(tq // BAND_BLOCK):
            base = i * tq + s * BAND_BLOCK - DIL_HALF
            lo = jnp.maximum(r_io, -base)
            hi = jnp.minimum(r_io + 2 * DIL_HALF, L - 1 - base)
            valid = (c_io >= lo) & (c_io <= hi)
            out_rows = pl.ds(s * BAND_BLOCK * dil + j, BAND_BLOCK, stride=dil)
            lse_tile = jnp.zeros((BAND_BLOCK, LANES), F32)
            for h in range(N_HEADS_PER_DIL):
                cols = slice(h * HEAD_DIM, (h + 1) * HEAD_DIM)
                q = q_ref[j, s * BAND_BLOCK:(s + 1) * BAND_BLOCK, cols]
                k = kw_ref[j, s * BAND_BLOCK:s * BAND_BLOCK + DIL_KB, cols]
                v = vw_ref[j, s * BAND_BLOCK:s * BAND_BLOCK + DIL_KB, cols]
                sc = lax.dot_general(q, k, (((1,), (1,)), ((), ())), preferred_element_type=F32) * SCALE
                sc = jnp.where(valid, sc, NEG)
                m = jnp.max(sc, axis=-1, keepdims=True)
                p = jnp.exp(sc - m)
                l = jnp.sum(p, axis=-1, keepdims=True)
                os_ref[h, out_rows, :] = jnp.dot(p.astype(BF16), v, preferred_element_type=F32) / l
                lse_tile = jnp.where(lane == h, m + jnp.log(l), lse_tile)
            ls_ref[out_rows, :] = lse_tile

    if dil == 1:
        phase(0)
    else:
        def body(j, c):
            phase(j)
            return c
        lax.fori_loop(0, dil, body, 0, unroll=2)
    for h in range(N_HEADS_PER_DIL):
        o_ref[:, h * HEAD_DIM:(h + 1) * HEAD_DIM] = os_ref[h].astype(BF16)
    lse_ref[...] = ls_ref[...]


def _dilattn(qkv, col0, B, T, g):
    dil = DIL_CONFIGS[g][1]
    assert DIL_CONFIGS[g][0] // 2 // dil == DIL_HALF
    L = T // dil
    tq = min(DIL_TOKENS_PER_STEP[g] // dil, L)
    nh = L // DIL_HALF
    ratio = tq // DIL_HALF
    blk = lambda rws, fn: pl.BlockSpec((None, dil, rws, COL_BLK), fn)
    prev = lambda i: jnp.maximum(i * ratio - 1, 0)
    nxt = lambda i: jnp.minimum((i + 1) * ratio, nh - 1)
    o, lse = pl.pallas_call(
        functools.partial(_dilattn_kernel, L=L, tq=tq, dil=dil),
        grid=(B, L // tq),
        in_specs=[
            blk(tq, lambda b, i: (b, 0, i, col0)),
            blk(DIL_HALF, lambda b, i: (b, 0, prev(i), col0 + 1)),
            blk(tq, lambda b, i: (b, 0, i, col0 + 1)),
            blk(DIL_HALF, lambda b, i: (b, 0, nxt(i), col0 + 1)),
            blk(DIL_HALF, lambda b, i: (b, 0, prev(i), col0 + 2)),
            blk(tq, lambda b, i: (b, 0, i, col0 + 2)),
            blk(DIL_HALF, lambda b, i: (b, 0, nxt(i), col0 + 2)),
        ],
        out_specs=[
            pl.BlockSpec((None, tq * dil, W_GRP), lambda b, i: (b, i, 0)),
            pl.BlockSpec((None, tq * dil, LANES), lambda b, i: (b, i, 0)),
        ],
        out_shape=[
            jax.ShapeDtypeStruct((B, T, W_GRP), BF16),
            jax.ShapeDtypeStruct((B, T, LANES), F32),
        ],
        scratch_shapes=[pltpu.VMEM((dil, tq + 2 * DIL_HALF, COL_BLK), BF16)] * 2
        + [pltpu.VMEM((N_HEADS_PER_DIL, tq * dil, HEAD_DIM), F32), pltpu.VMEM((tq * dil, LANES), F32)],
        compiler_params=_cparams(("arbitrary", "arbitrary")),
        name=f"dilattn{g}",
    )(qkv, qkv, qkv, qkv, qkv, qkv, qkv)
    return o.reshape(B * T, W_GRP), lse.reshape(B * T, LANES)


def _outproj_kernel(a_ref, o0_ref, o1_ref, o2_ref, l0_ref, l1_ref, l2_ref, ga_ref, gb_ref, x_ref,
                    wa_ref, wb_ref, wo_ref, g2_ref, wr_ref, x1_ref, lg_ref):
    o_refs = (o0_ref, o1_ref, o2_ref)
    lses = [r[...] for r in (l0_ref, l1_ref, l2_ref)]
    parts = []
    for h in range(N_HEADS_PER_DIL):
        cols = slice(h * HEAD_DIM, (h + 1) * HEAD_DIM)
        lh = [l[:, h:h + 1] for l in lses]
        mx = jnp.maximum(jnp.maximum(lh[0], lh[1]), lh[2])
        e = [jnp.exp(v - mx) for v in lh]
        den = e[0] + e[1] + e[2]
        acc = None
        for g in range(len(DIL_CONFIGS)):
            t = (e[g] / den) * o_refs[g][:, cols].astype(F32)
            acc = t if acc is None else acc + t
        parts.append(acc.astype(BF16))
    att_b = jnp.concatenate(parts, axis=-1)
    ya = jnp.dot(a_ref[...], wa_ref[...], preferred_element_type=F32)
    yb = jnp.dot(att_b, wb_ref[...], preferred_element_type=F32)
    mix = ga_ref[...].astype(F32) * ya + gb_ref[...].astype(F32) * yb
    x1 = x_ref[...] + jnp.dot(mix.astype(BF16), wo_ref[...], preferred_element_type=F32)
    x1_ref[...] = x1
    ms = jnp.mean(x1 * x1, axis=-1, keepdims=True)
    h2 = x1 * lax.rsqrt(ms + EPS) * g2_ref[...]
    lg_ref[...] = jnp.dot(h2.astype(BF16), wr_ref[...], preferred_element_type=F32)


def _outproj(att_a, os_, lses, gates, x2, wa, wb, wo, g2, wr, tm):
    n = x2.shape[0]
    row = lambda w: pl.BlockSpec((tm, w), lambda i: (i, 0))
    full = lambda a: pl.BlockSpec(a.shape, lambda i: (0,) * a.ndim)
    return pl.pallas_call(
        _outproj_kernel,
        grid=(n // tm,),
        in_specs=[row(W_NA), row(W_GRP), row(W_GRP), row(W_GRP), row(LANES), row(LANES), row(LANES),
                  pl.BlockSpec((tm, D_MODEL), lambda i: (i, 0)),
                  pl.BlockSpec((tm, D_MODEL), lambda i: (i, 1)),
                  row(D_MODEL), full(wa), full(wb), full(wo), full(g2), full(wr)],
        out_specs=[row(D_MODEL), row(LANES)],
        out_shape=[jax.ShapeDtypeStruct((n, D_MODEL), F32), jax.ShapeDtypeStruct((n, LANES), F32)],
        compiler_params=_cparams(("arbitrary",)),
        name="outproj",
    )(att_a, *os_, *lses, gates, gates, x2, wa, wb, wo, g2, wr)


ROUTE_E0 = N_GROUPS


def _route_kernel(lg_ref, b_ref, gate_ref, eid_ref, rank_ref, cnt_ref, carry_ref):
    i = pl.program_id(0)

    @pl.when(i == 0)
    def _():
        carry_ref[...] = jnp.zeros_like(carry_ref)

    z = lg_ref[...] + b_ref[...]
    tm = z.shape[0]
    lane = lax.broadcasted_iota(jnp.int32, z.shape, 1)
    lane_f = lane.astype(F32)
    rmax = lambda v: jnp.max(v, axis=-1, keepdims=True)
    rmin = lambda v: jnp.min(v, axis=-1, keepdims=True)
    rsum = lambda v: jnp.sum(v, axis=-1, keepdims=True)

    gmask = lane < N_GROUPS
    gmax = rmax(jnp.where(gmask, z, -jnp.inf))
    gsel = rmin(jnp.where(gmask & (z == gmax), lane_f, float(LANES)))
    gp = 1.0 / rsum(jnp.where(gmask, jnp.exp(z - gmax), 0.0))

    elo = float(ROUTE_E0) + float(EXPERTS_PER_GROUP) * gsel
    emask = (lane_f >= elo) & (lane_f < elo + float(EXPERTS_PER_GROUP))
    emax = rmax(jnp.where(emask, z, -jnp.inf))
    ee = jnp.where(emask, jnp.exp(z - emax), 0.0)
    eprob = ee / rsum(ee)
    p1 = rmax(jnp.where(emask, eprob, -1.0))
    i1 = rmin(jnp.where(emask & (eprob == p1), lane_f, float(LANES)))
    m2 = emask & (lane_f != i1)
    p2 = rmax(jnp.where(m2, eprob, -1.0))
    i2 = rmin(jnp.where(m2 & (eprob == p2), lane_f, float(LANES)))
    den = p1 + p2
    g1 = gp * p1 / den
    g2 = gp * p2 / den

    oh1 = (lane_f == i1).astype(F32)
    oh2 = (lane_f == i2).astype(F32)
    oh = oh1 + oh2
    tri = (lax.broadcasted_iota(jnp.int32, (tm, tm), 1) < lax.broadcasted_iota(jnp.int32, (tm, tm), 0)).astype(BF16)
    tot = carry_ref[...] + jnp.dot(tri, oh.astype(BF16), preferred_element_type=F32)
    r1 = rsum(oh1 * tot)
    r2 = rsum(oh2 * tot)
    carry = carry_ref[...] + jnp.sum(oh, axis=0, keepdims=True)
    carry_ref[...] = carry
    cnt_ref[...] = carry

    two = lambda a, b: jnp.where(lane == 0, a, jnp.where(lane == 1, b, 0.0))
    gate_ref[...] = two(g1, g2)
    eid_ref[...] = two(i1 - float(ROUTE_E0), i2 - float(ROUTE_E0)).astype(jnp.int32)
    rank_ref[...] = two(r1, r2).astype(jnp.int32)


def _route(logits, rbias, tm):
    n = logits.shape[0]
    row = pl.BlockSpec((tm, LANES), lambda i: (i, 0))
    one = pl.BlockSpec((1, LANES), lambda i: (0, 0))
    return pl.pallas_call(
        _route_kernel,
        grid=(n // tm,),
        in_specs=[row, one],
        out_specs=[row, row, row, one],
        out_shape=[jax.ShapeDtypeStruct((n, LANES), F32), jax.ShapeDtypeStruct((n, LANES), jnp.int32),
                   jax.ShapeDtypeStruct((n, LANES), jnp.int32), jax.ShapeDtypeStruct((1, LANES), F32)],
        scratch_shapes=[pltpu.VMEM((1, LANES), F32)],
        compiler_params=_cparams(("arbitrary",)),
        name="route",
    )(logits, rbias)


TOP_K = 2


ROW_SUB = 8
HALF_D = D_MODEL // 2
assert ROW_SUB * LANES == HALF_D


def _pack_pair(lo, hi):
    lo_b = lax.bitcast_convert_type(lo.astype(BF16).astype(F32), jnp.uint32)
    hi_b = lax.bitcast_convert_type(hi.astype(BF16).astype(F32), jnp.uint32)
    return (lo_b >> 16) | hi_b


def _unpack_pair(w):
    lo = lax.bitcast_convert_type(w << 16, F32)
    hi = lax.bitcast_convert_type(w & jnp.uint32(0xFFFF0000), F32)
    return lo, hi


def _store_packed_rows(dst, x):
    for c in range(ROW_SUB):
        dst(c, _pack_pair(x[:, c * LANES:(c + 1) * LANES], x[:, HALF_D + c * LANES:HALF_D + (c + 1) * LANES]))


def _dispatch_kernel(pos_ref, *refs, bounds):
    nsrc = len(bounds) - 1
    x1_refs, (g2_ref, xs_ref, hp_ref, sem) = refs[:nsrc], refs[nsrc:]
    i = pl.program_id(0)
    last = pl.num_programs(0) - 1
    slot = i % 2
    tm = x1_refs[0].shape[0]

    def put(c, words):
        hp_ref[slot, pl.ds(c, tm, stride=ROW_SUB), :] = words

    for s in range(nsrc):
        @pl.when((i >= bounds[s]) & (i < bounds[s + 1]))
        def _(x1_ref=x1_refs[s]):
            x1 = x1_ref[...]
            ms = jnp.mean(x1 * x1, axis=-1, keepdims=True)
            _store_packed_rows(put, x1 * lax.rsqrt(ms + EPS) * g2_ref[...])

    def row_copy(sl, r, p):
        src = hp_ref.at[sl, pl.ds(pl.multiple_of(r * ROW_SUB, ROW_SUB), ROW_SUB)]
        dst = xs_ref.at[pl.ds(pl.multiple_of(p * ROW_SUB, ROW_SUB), ROW_SUB)]
        return pltpu.make_async_copy(src, dst, sem.at[sl])

    def start(r, c):
        for k in range(TOP_K):
            row_copy(slot, r, pos_ref[0, TOP_K * r + k]).start(priority=k)
        return c

    def wait_all(sl):
        def wait(r, c):
            for k in range(TOP_K):
                row_copy(sl, 0, 0).wait()
            return c
        lax.fori_loop(0, tm, wait, 0)

    lax.fori_loop(0, tm, start, 0, unroll=4)

    @pl.when(i > 0)
    def _():
        wait_all(1 - slot)

    @pl.when(i == last)
    def _():
        wait_all(slot)


def _dispatch(pos3, x1s, g2, tm):
    bounds = [0]
    for x1 in x1s:
        bounds.append(bounds[-1] + x1.shape[0] // tm)
    n = bounds[-1] * tm
    src = lambda s: pl.BlockSpec(
        (tm, D_MODEL), lambda i: (jnp.clip(i - bounds[s], 0, bounds[s + 1] - bounds[s] - 1), 0))
    return pl.pallas_call(
        functools.partial(_dispatch_kernel, bounds=tuple(bounds)),
        grid=(bounds[-1],),
        in_specs=[pl.BlockSpec((None, 1, TOP_K * tm), lambda i: (i, 0, 0), memory_space=pltpu.SMEM)]
        + [src(s) for s in range(len(x1s))]
        + [pl.BlockSpec((1, D_MODEL), lambda i: (0, 0))],
        out_specs=pl.BlockSpec(memory_space=pl.ANY),
        out_shape=jax.ShapeDtypeStruct((n * TOP_K * ROW_SUB, LANES), jnp.uint32),
        scratch_shapes=[pltpu.VMEM((2, tm * ROW_SUB, LANES), jnp.uint32), pltpu.SemaphoreType.DMA((2,))],
        compiler_params=_cparams(("arbitrary",)),
        name="dispatch",
    )(pos3, *x1s, g2)


def _combine_kernel(pos_ref, posn_ref, gate_ref, x1_ref, ys_ref, y_ref, buf_ref, sem):
    i = pl.program_id(0)
    last = pl.num_programs(0) - 1
    slot = i % 2
    tm = x1_ref.shape[0]

    def row_copy(sl, k, r, p):
        src = ys_ref.at[pl.ds(pl.multiple_of(p * ROW_SUB, ROW_SUB), ROW_SUB)]
        dst = buf_ref.at[sl, k, pl.ds(pl.multiple_of(r * ROW_SUB, ROW_SUB), ROW_SUB)]
        return pltpu.make_async_copy(src, dst, sem.at[sl])

    def start_all(sl, p_ref):
        def start(r, c):
            for k in range(TOP_K):
                row_copy(sl, k, r, p_ref[0, TOP_K * r + k]).start(priority=k)
            return c
        lax.fori_loop(0, tm, start, 0, unroll=4)

    @pl.when(i == 0)
    def _():
        start_all(0, pos_ref)

    @pl.when(i < last)
    def _():
        start_all(1 - slot, posn_ref)

    def wait(r, c):
        for k in range(TOP_K):
            row_copy(slot, k, 0, 0).wait()
        return c

    lax.fori_loop(0, tm, wait, 0)
    gates = gate_ref[...]
    g0 = jnp.broadcast_to(gates[:, 0:1], (tm, LANES))
    g1 = jnp.broadcast_to(gates[:, 1:2], (tm, LANES))
    for c in range(ROW_SUB):
        lo0, hi0 = _unpack_pair(buf_ref[slot, 0, pl.ds(c, tm, stride=ROW_SUB), :])
        lo1, hi1 = _unpack_pair(buf_ref[slot, 1, pl.ds(c, tm, stride=ROW_SUB), :])
        cl = slice(c * LANES, (c + 1) * LANES)
        ch = slice(HALF_D + c * LANES, HALF_D + (c + 1) * LANES)
        y_ref[:, cl] = x1_ref[:, cl] + (g0 * lo0 + g1 * lo1)
        y_ref[:, ch] = x1_ref[:, ch] + (g0 * hi0 + g1 * hi1)


def _combine(pos3, gates, x1, ys, tm, row0):
    n = x1.shape[0]
    nt = n // tm
    b0 = row0 // tm
    pos_blk = lambda fn: pl.BlockSpec((None, 1, TOP_K * tm), fn, memory_space=pltpu.SMEM)
    return pl.pallas_call(
        _combine_kernel,
        grid=(nt,),
        in_specs=[pos_blk(lambda i: (i + b0, 0, 0)),
                  pos_blk(lambda i: (jnp.minimum(i + 1, nt - 1) + b0, 0, 0)),
                  pl.BlockSpec((tm, LANES), lambda i: (i + b0, 0)),
                  pl.BlockSpec((tm, D_MODEL), lambda i: (i, 0)),
                  pl.BlockSpec(memory_space=pl.ANY)],
        out_specs=pl.BlockSpec((tm, D_MODEL), lambda i: (i, 0)),
        out_shape=jax.ShapeDtypeStruct((n, D_MODEL), F32),
        scratch_shapes=[pltpu.VMEM((2, TOP_K, tm * ROW_SUB, LANES), jnp.uint32), pltpu.SemaphoreType.DMA((2,))],
        compiler_params=_cparams(("arbitrary",)),
        name="combine",
    )(pos3, pos3, gates, x1, ys)


def _experts_kernel(tile_s, exp_s, lo_s, hi_s, first_s, x_ref, wg_ref, wu_ref, wd_ref, y_ref, xb_ref):
    w = pl.program_id(0)
    lo, hi = lo_s[w], hi_s[w]
    bm = xb_ref.shape[0]

    def work_item(first_visit):
        for c in range(ROW_SUB):
            xl, xh = _unpack_pair(x_ref[pl.ds(c, bm, stride=ROW_SUB), :])
            xb_ref[:, c * LANES:(c + 1) * LANES] = xl.astype(BF16)
            xb_ref[:, HALF_D + c * LANES:HALF_D + (c + 1) * LANES] = xh.astype(BF16)
        x = xb_ref[...]
        a = jnp.dot(x, wg_ref[...], preferred_element_type=F32)
        u = jnp.dot(x, wu_ref[...], preferred_element_type=F32)
        hm = (a * jax.nn.sigmoid(a) * u).astype(BF16)
        y = jnp.dot(hm, wd_ref[...], preferred_element_type=F32)
        rows = lax.broadcasted_iota(jnp.int32, (bm, LANES), 0)
        mine = (rows >= lo) & (rows < hi)

        def put(c, words):
            rws = pl.ds(c, bm, stride=ROW_SUB)
            y_ref[rws, :] = jnp.where(mine, words, jnp.uint32(0) if first_visit else y_ref[rws, :])
        _store_packed_rows(put, y)

    @pl.when((hi > lo) & (first_s[w] == 1))
    def _():
        work_item(True)

    @pl.when((hi > lo) & (first_s[w] == 0))
    def _():
        work_item(False)


def _experts(meta, xs, wg, wu, wd, bm):
    p = xs.shape[0] // ROW_SUB
    nw = p // bm + N_EXPERTS
    grid_spec = pltpu.PrefetchScalarGridSpec(
        num_scalar_prefetch=5,
        grid=(nw,),
        in_specs=[
            pl.BlockSpec((bm * ROW_SUB, LANES), lambda w, t, e, lo, hi, f: (t[w], 0)),
            pl.BlockSpec((None, D_MODEL, D_EXPERT), lambda w, t, e, lo, hi, f: (e[w], 0, 0)),
            pl.BlockSpec((None, D_MODEL, D_EXPERT), lambda w, t, e, lo, hi, f: (e[w], 0, 0)),
            pl.BlockSpec((None, D_EXPERT, D_MODEL), lambda w, t, e, lo, hi, f: (e[w], 0, 0)),
        ],
        out_specs=pl.BlockSpec((bm * ROW_SUB, LANES), lambda w, t, e, lo, hi, f: (t[w], 0)),
        scratch_shapes=[pltpu.VMEM((bm, D_MODEL), BF16)],
    )
    return pl.pallas_call(
        _experts_kernel,
        grid_spec=grid_spec,
        out_shape=jax.ShapeDtypeStruct((p * ROW_SUB, LANES), jnp.uint32),
        compiler_params=_cparams(("arbitrary",)),
        name="experts",
    )(*meta, xs, wg, wu, wd)


def _expert_work_items(counts, p, bm):
    ntiles = p // bm
    nw = ntiles + N_EXPERTS
    ends = jnp.cumsum(counts)
    starts = ends - counts
    ft = starts // bm
    nt = jnp.where(counts > 0, (ends - 1) // bm - ft + 1, 0)
    wend = jnp.cumsum(nt)
    wstart = wend - nt
    w = jnp.arange(nw, dtype=jnp.int32)
    used = w < wend[-1]
    e = jnp.minimum(jnp.sum((w[:, None] >= wend[None, :]).astype(jnp.int32), axis=1), N_EXPERTS - 1)
    e_last = jnp.max(jnp.where(counts > 0, jnp.arange(N_EXPERTS, dtype=jnp.int32), 0))
    e = jnp.where(used, e, e_last)
    tile = jnp.where(used, ft[e] + (w - wstart[e]), ntiles - 1).astype(jnp.int32)
    lo = jnp.where(used, jnp.maximum(starts[e], tile * bm) - tile * bm, 0).astype(jnp.int32)
    hi = jnp.where(used, jnp.minimum(ends[e], (tile + 1) * bm) - tile * bm, 0).astype(jnp.int32)
    first = jnp.concatenate([jnp.ones((1,), jnp.int32), (tile[1:] != tile[:-1]).astype(jnp.int32)])
    return (tile, e, lo, hi, first), starts


def _rope_tables(T):
    half = ROT_DIM // 2
    inv = 1.0 / (ROPE_THETA ** (np.arange(half, dtype=np.float64) * (2.0 / ROT_DIM)))
    ang = np.arange(T, dtype=np.float64)[:, None] * inv[None, :]
    cos, sin = np.cos(ang), np.sin(ang)
    rest = HEAD_DIM - ROT_DIM
    z_half, z_rest = np.zeros((T, half)), np.zeros((T, rest))
    cos_t = np.concatenate([cos, cos, np.ones((T, rest))], axis=1)
    sin_t = np.concatenate([-sin, sin, z_rest], axis=1)
    cos_t = np.stack([np.ones_like(cos_t), cos_t])
    sin_t = np.stack([np.zeros_like(sin_t), sin_t])
    return tuple(jnp.asarray(t, dtype=F32) for t in (cos_t, sin_t))


TM_IN = 1024
TM_OUT = 256
TM_ROUTE = 512
TM_MOVE = 512
BM_EXPERT = 256


def _attention_half(x, p):
    B, T, D = x.shape
    n = B * T
    assert D == D_MODEL and T % NA_TQ == 0 and T // GRID_W >= NA_ROWS + NA_HALO_ROWS
    x2 = x.reshape(n, D)
    main, qkv1, qkv2, sig = _inproj(x2, p["g1"], p["w_in"], p["gains"], *_rope_tables(T), B, T, TM_IN)
    main3 = main.reshape(B, T, JB_G1 * COL_BLK)
    att_a = _natten(main3, p["na_bias"], B, T).reshape(n, W_NA)
    dil = [_dilattn(main3[:, None], JB_G0, B, T, 0), _dilattn(qkv1, 0, B, T, 1), _dilattn(qkv2, 0, B, T, 2)]
    return _outproj(att_a, [d[0] for d in dil], [d[1] for d in dil], sig, x2,
                    p["wa"], p["wb"], p["wo"], p["g2"], p["wr"], TM_OUT)


def _layer(xs_in, p):
    halves = [_attention_half(x, p) for x in xs_in]
    x1s = [h[0] for h in halves]
    sizes = [x1.shape[0] for x1 in x1s]
    n = sum(sizes)
    logits = jnp.concatenate([h[1] for h in halves], axis=0)

    gates, eid, rank, cnt = _route(logits, p["rbias"], TM_ROUTE)
    counts = cnt[0, ROUTE_E0:ROUTE_E0 + N_EXPERTS].astype(jnp.int32)
    meta, starts = _expert_work_items(counts, n * TOP_K, BM_EXPERT)
    is_e = eid[:, :TOP_K, None] == jnp.arange(N_EXPERTS, dtype=jnp.int32)
    pos = jnp.sum(jnp.where(is_e, starts, 0), axis=-1) + rank[:, :TOP_K]
    pos3 = pos.astype(jnp.int32).reshape(n // TM_MOVE, 1, TOP_K * TM_MOVE)

    xs = _dispatch(pos3, x1s, p["g2"], TM_MOVE)
    ys = _experts(meta, xs, p["wg"], p["wu"], p["wd"], BM_EXPERT)
    outs, row0 = [], 0
    for x, x1, m in zip(xs_in, x1s, sizes):
        outs.append(_combine(pos3, gates, x1, ys, TM_MOVE, row0).reshape(x.shape))
        row0 += m
    return tuple(outs)


def _prepare_params(norm1_g, w_in, qn_a, kn_a, rpb_a, qn_b, kn_b, w_branch_a, w_branch_b, w_out,
                    norm2_g, router_group_w, router_group_b, router_expert_w, router_expert_b, w_gate, w_up, w_down):
    assert norm1_g.shape[0] == 1
    l = 0
    pad_r = LANES - N_GROUPS - N_EXPERTS
    zrow = jnp.concatenate([jnp.ones((1, HEAD_DIM), F32), jnp.zeros((3, HEAD_DIM), F32)], axis=0)
    col = lambda start, width: w_in[l][:, start:start + width]
    qb0, kb0, vb0 = 3 * W_NA, 3 * W_NA + W_DIL, 3 * W_NA + 2 * W_DIL
    pieces = [col(0, 3 * W_NA)]
    for g in range(len(DIL_CONFIGS)):
        pieces += [col(qb0 + g * W_GRP, W_GRP), col(kb0 + g * W_GRP, W_GRP), col(vb0 + g * W_GRP, W_GRP)]
    pieces.append(col(3 * W_NA + 3 * W_DIL, 2 * D_MODEL))
    return {
        "g1": norm1_g[l][None, :],
        "w_in": jnp.concatenate(pieces, axis=1).astype(BF16),
        "gains": jnp.concatenate([qn_a[l][None], kn_a[l][None], qn_b[l][None], kn_b[l][None], zrow], axis=0),
        "na_bias": _na_bias_table(rpb_a[l]),
        "wa": w_branch_a[l].astype(BF16),
        "wb": w_branch_b[l].astype(BF16),
        "wo": w_out[l].astype(BF16),
        "g2": norm2_g[l][None, :],
        "wr": jnp.concatenate([router_group_w[l], router_expert_w[l], jnp.zeros((D_MODEL, pad_r), F32)],
                              axis=1).astype(BF16),
        "rbias": jnp.concatenate([router_group_b[l], router_expert_b[l], jnp.zeros((pad_r,), F32)])[None, :],
        "wg": w_gate[l].astype(BF16),
        "wu": w_up[l].astype(BF16),
        "wd": w_down[l].astype(BF16),
    }


def kernel(x_prompt, x_sample, norm1_g, w_in, qn_a, kn_a, rpb_a, qn_b, kn_b, w_branch_a, w_branch_b, w_out,
           norm2_g, router_group_w, router_group_b, router_expert_w, router_expert_b, w_gate, w_up, w_down):
    p = _prepare_params(norm1_g, w_in, qn_a, kn_a, rpb_a, qn_b, kn_b, w_branch_a, w_branch_b, w_out, norm2_g,
                        router_group_w, router_group_b, router_expert_w, router_expert_b, w_gate, w_up, w_down)
    return _layer((x_prompt, x_sample), p)
```

```python
import functools

import jax
import jax.numpy as jnp
import numpy as np
from jax import lax
from jax.experimental import pallas as pl
from jax.experimental.pallas import tpu as pltpu

F32 = jnp.float32
BF16 = jnp.bfloat16

D_MODEL = 2048
HEAD_DIM = 128
N_HEADS_NA = 4
N_HEADS_PER_DIL = 4
DIL_CONFIGS = ((128, 1), (512, 4), (2048, 16))
W_NA = N_HEADS_NA * HEAD_DIM
W_DIL = N_HEADS_PER_DIL * len(DIL_CONFIGS) * HEAD_DIM
W_GRP = N_HEADS_PER_DIL * HEAD_DIM
GRID_W = 64
NA_ROWS = 8
NA_COLS = 16
ROT_DIM = HEAD_DIM // 4
ROPE_THETA = 500000.0
BAND_BLOCK = 128
N_GROUPS = 4
EXPERTS_PER_GROUP = 8
N_EXPERTS = N_GROUPS * EXPERTS_PER_GROUP
D_EXPERT = 1024
EPS = 1e-6
NEG = -1e30
IN_WIDTH = 3 * W_NA + 3 * W_DIL + 2 * D_MODEL
SCALE = HEAD_DIM ** -0.5

COL_BLK = 512
JB_G0, JB_G1, JB_G2, JB_GATE = 3, 6, 9, 12
N_COL_BLK = IN_WIDTH // COL_BLK

LANES = 128
VMEM_LIMIT = 56 * 1024 * 1024


def _cparams(sem, vmem=VMEM_LIMIT):
    return pltpu.CompilerParams(dimension_semantics=sem, vmem_limit_bytes=vmem)


def _inproj_kernel(x_hbm, g1_ref, w_ref, gains_ref, cos_ref, sin_ref,
                   main_ref, d1_ref, d2_ref, gate_ref, h_ref, acc_ref, y_ref, xbuf_ref, xsem):
    i = pl.program_id(0)
    j = pl.program_id(1)
    jb = j - 1
    nh = COL_BLK // HEAD_DIM
    tm = h_ref.shape[0]
    slot = i % 2
    kind = jb % 3
    is_v = kind == 2
    gain_row = jnp.where(is_v, 4, kind + jnp.where(jb < JB_G0, 0, 2))

    def x_copy(tile, sl):
        rows = pl.ds(pl.multiple_of(tile * tm, tm), tm)
        return pltpu.make_async_copy(x_hbm.at[rows], xbuf_ref.at[sl], xsem.at[sl])

    def matmul():
        acc_ref[...] = jnp.dot(h_ref[...], w_ref[...], preferred_element_type=F32)

    @pl.when((i == 0) & (j == 0))
    def _():
        x_copy(0, 0).start()

    @pl.when((j == 1) & (i + 1 < pl.num_programs(0)))
    def _():
        x_copy(i + 1, 1 - slot).start()

    @pl.when(j == 0)
    def _():
        x_copy(i, slot).wait()
        x = xbuf_ref[slot]
        ms = jnp.mean(x * x, axis=-1, keepdims=True)
        h_ref[...] = (x * lax.rsqrt(ms + EPS) * g1_ref[...]).astype(BF16)
        matmul()

    def finished_head(hh, gain, cos, sin, low):
        xh = acc_ref[:, hh * HEAD_DIM:(hh + 1) * HEAD_DIM]
        ms = jnp.mean(xh * xh, axis=-1, keepdims=True)
        y = xh * jnp.where(is_v, 1.0, lax.rsqrt(ms + EPS)) * gain
        partner = jnp.where(low, pltpu.roll(y, HEAD_DIM - ROT_DIM // 2, 1), pltpu.roll(y, ROT_DIM // 2, 1))
        return y * cos + partner * sin

    def attention_block(store_head, after=None):
        gain = gains_ref[pl.ds(gain_row, 1), :]
        cos, sin = jnp.where(is_v, 1.0, cos_ref[...]), jnp.where(is_v, 0.0, sin_ref[...])
        low = lax.broadcasted_iota(jnp.int32, cos.shape, 1) < ROT_DIM // 2
        for hh in range(nh):
            store_head(hh, finished_head(hh, gain, cos, sin, low))
        if after is not None:
            after()
        matmul()

    def to_main(hh, y):
        main_ref[:, hh * HEAD_DIM:(hh + 1) * HEAD_DIM] = y.astype(BF16)

    def to_scratch(hh, y):
        y_ref[hh] = y

    def deinterleave(dst_ref, dil):
        rows = y_ref.shape[1] // dil
        for ph in range(dil):
            for hh in range(nh):
                dst_ref[ph, :, hh * HEAD_DIM:(hh + 1) * HEAD_DIM] = (
                    y_ref[hh, pl.ds(ph, rows, stride=dil), :].astype(BF16))

    @pl.when((jb >= 0) & (jb < JB_G1))
    def _():
        attention_block(to_main)

    @pl.when((jb >= JB_G1) & (jb < JB_G2))
    def _():
        attention_block(to_scratch, lambda: deinterleave(d1_ref, DIL_CONFIGS[1][1]))

    @pl.when((jb >= JB_G2) & (jb < JB_GATE))
    def _():
        attention_block(to_scratch, lambda: deinterleave(d2_ref, DIL_CONFIGS[2][1]))

    def gates():
        gate_ref[...] = (0.5 * jnp.tanh(0.5 * acc_ref[...]) + 0.5).astype(BF16)

    @pl.when((jb >= JB_GATE) & (j < N_COL_BLK))
    def _():
        gates()
        matmul()

    @pl.when(j == N_COL_BLK)
    def _():
        gates()


def _inproj(x2, g1, w_in_bf, gains, cos_t, sin_t, B, T, tm):
    n = x2.shape[0]
    tpos = T // tm
    d1, d2 = DIL_CONFIGS[1][1], DIL_CONFIGS[2][1]
    rot = lambda j: jnp.where(j - 1 >= JB_G0, 1, 0)
    tab = pl.BlockSpec((None, tm, HEAD_DIM), lambda i, j: (rot(j), i % tpos, 0))
    jb = lambda j: jnp.maximum(j - 1, 0)
    return pl.pallas_call(
        _inproj_kernel,
        grid=(n // tm, N_COL_BLK + 1),
        in_specs=[
            pl.BlockSpec(memory_space=pl.ANY),
            pl.BlockSpec((1, D_MODEL), lambda i, j: (0, 0)),
            pl.BlockSpec((D_MODEL, COL_BLK), lambda i, j: (0, jnp.minimum(j, N_COL_BLK - 1))),
            pl.BlockSpec((8, HEAD_DIM), lambda i, j: (0, 0)),
            tab, tab,
        ],
        out_specs=[
            pl.BlockSpec((tm, COL_BLK), lambda i, j: (i, jnp.minimum(jb(j), JB_G1 - 1))),
            pl.BlockSpec((None, d1, tm // d1, COL_BLK),
                         lambda i, j: (i // tpos, 0, i % tpos, jnp.clip(jb(j) - JB_G1, 0, 2))),
            pl.BlockSpec((None, d2, tm // d2, COL_BLK),
                         lambda i, j: (i // tpos, 0, i % tpos, jnp.clip(jb(j) - JB_G2, 0, 2))),
            pl.BlockSpec((tm, COL_BLK), lambda i, j: (i, jnp.clip(jb(j) - JB_GATE, 0, N_COL_BLK - JB_GATE - 1))),
        ],
        out_shape=[
            jax.ShapeDtypeStruct((n, JB_G1 * COL_BLK), BF16),
            jax.ShapeDtypeStruct((B, d1, T // d1, 3 * COL_BLK), BF16),
            jax.ShapeDtypeStruct((B, d2, T // d2, 3 * COL_BLK), BF16),
            jax.ShapeDtypeStruct((n, 2 * D_MODEL), BF16),
        ],
        scratch_shapes=[pltpu.VMEM((tm, D_MODEL), BF16), pltpu.VMEM((tm, COL_BLK), F32),
                        pltpu.VMEM((COL_BLK // HEAD_DIM, tm, HEAD_DIM), F32),
                        pltpu.VMEM((2, tm, D_MODEL), F32), pltpu.SemaphoreType.DMA((2,))],
        compiler_params=_cparams(("arbitrary", "arbitrary")),
        name="inproj",
    )(x2, g1, w_in_bf, gains, cos_t, sin_t)


NA_CHUNK_ROWS = 16
NA_HALO_ROWS = 4
NA_TQ = NA_CHUNK_ROWS * GRID_W
NA_TH = NA_HALO_ROWS * GRID_W
NA_GROUP = 4
NA_SPAN_ROWS = 12
NA_GQ = NA_GROUP * GRID_W
NA_SPAN = NA_SPAN_ROWS * GRID_W
NA_CASES = 3
assert NA_CHUNK_ROWS % NA_GROUP == 0 and NA_HALO_ROWS == NA_ROWS // 2
assert NA_ROWS + NA_GROUP - 1 <= NA_SPAN_ROWS <= NA_GROUP + 2 * NA_HALO_ROWS


def _natten_kernel(q_ref, kp_ref, km_ref, kn_ref, vp_ref, vm_ref, vn_ref, bias_ref, o_ref, kw_ref, vw_ref, *, rows):
    i = pl.program_id(1)
    kw_ref[0:NA_TH] = kp_ref[...]
    kw_ref[NA_TH:NA_TH + NA_TQ] = km_ref[...]
    kw_ref[NA_TH + NA_TQ:] = kn_ref[...]
    vw_ref[0:NA_TH] = vp_ref[...]
    vw_ref[NA_TH:NA_TH + NA_TQ] = vm_ref[...]
    vw_ref[NA_TH + NA_TQ:] = vn_ref[...]

    for gg in range(NA_CHUNK_ROWS // NA_GROUP):
        r0 = i * NA_CHUNK_ROWS + gg * NA_GROUP
        rs0 = jnp.clip(r0 - NA_ROWS // 2, 0, rows - NA_ROWS)
        case = jnp.where(r0 == 0, 0, jnp.where(r0 == rows - NA_GROUP, 2, 1))
        off = pl.multiple_of((rs0 - (i * NA_CHUNK_ROWS - NA_HALO_ROWS)) * GRID_W, GRID_W)
        for h in range(N_HEADS_NA):
            cols = slice(h * HEAD_DIM, (h + 1) * HEAD_DIM)
            q = q_ref[gg * NA_GQ:(gg + 1) * NA_GQ, cols]
            k = kw_ref[pl.ds(off, NA_SPAN), cols]
            v = vw_ref[pl.ds(off, NA_SPAN), cols]
            s = lax.dot_general(q, k, (((1,), (1,)), ((), ())), preferred_element_type=F32)
            s = s * SCALE + bias_ref[case, h]
            m = jnp.max(s, axis=-1, keepdims=True)
            p = jnp.exp(s - m)
            l = jnp.sum(p, axis=-1, keepdims=True)
            o = jnp.dot(p.astype(BF16), v, preferred_element_type=F32) / l
            o_ref[gg * NA_GQ:(gg + 1) * NA_GQ, cols] = o.astype(BF16)


def _natten(proj3, bias_tab, B, T):
    rows = T // GRID_W
    nh = T // NA_TH
    ratio = NA_TQ // NA_TH
    blk = lambda rws, fn: pl.BlockSpec((None, rws, COL_BLK), fn)
    return pl.pallas_call(
        functools.partial(_natten_kernel, rows=rows),
        grid=(B, T // NA_TQ),
        in_specs=[
            blk(NA_TQ, lambda b, i: (b, i, 0)),
            blk(NA_TH, lambda b, i: (b, jnp.maximum(i * ratio - 1, 0), 1)),
            blk(NA_TQ, lambda b, i: (b, i, 1)),
            blk(NA_TH, lambda b, i: (b, jnp.minimum((i + 1) * ratio, nh - 1), 1)),
            blk(NA_TH, lambda b, i: (b, jnp.maximum(i * ratio - 1, 0), 2)),
            blk(NA_TQ, lambda b, i: (b, i, 2)),
            blk(NA_TH, lambda b, i: (b, jnp.minimum((i + 1) * ratio, nh - 1), 2)),
            pl.BlockSpec((NA_CASES, N_HEADS_NA, NA_GQ, NA_SPAN), lambda b, i: (0, 0, 0, 0)),
        ],
        out_specs=pl.BlockSpec((None, NA_TQ, W_NA), lambda b, i: (b, i, 0)),
        out_shape=jax.ShapeDtypeStruct((B, T, W_NA), BF16),
        scratch_shapes=[pltpu.VMEM((NA_TQ + 2 * NA_TH, COL_BLK), BF16)] * 2,
        compiler_params=_cparams(("arbitrary", "arbitrary")),
        name="natten",
    )(proj3, proj3, proj3, proj3, proj3, proj3, proj3, bias_tab)


def _na_bias_table(rpb):
    c = np.arange(GRID_W)
    cs = np.clip(c - NA_COLS // 2, 0, GRID_W - NA_COLS)
    cc = np.arange(GRID_W)
    valid = (cc[None, :] >= cs[:, None]) & (cc[None, :] < cs[:, None] + NA_COLS)
    dc_idx = np.clip(cc[None, :] - c[:, None] + NA_COLS - 1, 0, 2 * NA_COLS - 2)
    pick = (dc_idx[..., None] == np.arange(2 * NA_COLS - 1)).astype(np.float32)
    t = jnp.einsum("qkd,hrd->hrqk", pick, rpb.astype(F32), precision=lax.Precision.HIGHEST)
    t = jnp.where(valid[None, None], t, NEG)
    half = NA_ROWS // 2
    cases = ([(m, 0) for m in range(NA_GROUP)],
             [(half, m) for m in range(NA_GROUP)],
             [(half + m, 0) for m in range(NA_GROUP)])
    tiles = []
    for case in cases:
        per_row = []
        for var, woff in case:
            win = t[:, NA_ROWS - 1 - var:2 * NA_ROWS - 1 - var]
            pad = ((0, 0), (woff, NA_SPAN_ROWS - NA_ROWS - woff), (0, 0), (0, 0))
            per_row.append(jnp.pad(win, pad, constant_values=NEG))
        tile = jnp.stack(per_row, axis=1)
        tiles.append(tile.transpose(0, 1, 3, 2, 4).reshape(N_HEADS_NA, NA_GQ, NA_SPAN))
    return jnp.stack(tiles, axis=0)


DIL_HALF = 64
DIL_KB = BAND_BLOCK + 2 * DIL_HALF
DIL_TOKENS_PER_STEP = (1024, 1024, 2048)


def _dilattn_kernel(q_ref, kp_ref, km_ref, kn_ref, vp_ref, vm_ref, vn_ref, o_ref, lse_ref,
                    kw_ref, vw_ref, os_ref, ls_ref, *, L, tq, dil):
    i = pl.program_id(1)
    kw_ref[:, 0:DIL_HALF] = kp_ref[...]
    kw_ref[:, DIL_HALF:DIL_HALF + tq] = km_ref[...]
    kw_ref[:, DIL_HALF + tq:] = kn_ref[...]
    vw_ref[:, 0:DIL_HALF] = vp_ref[...]
    vw_ref[:, DIL_HALF:DIL_HALF + tq] = vm_ref[...]
    vw_ref[:, DIL_HALF + tq:] = vn_ref[...]

    def phase(j):
        r_io = lax.broadcasted_iota(jnp.int32, (BAND_BLOCK, DIL_KB), 0)
        c_io = lax.broadcasted_iota(jnp.int32, (BAND_BLOCK, DIL_KB), 1)
        lane = lax.broadcasted_iota(jnp.int32, (BAND_BLOCK, LANES), 1)
        for s in range(tq // BAND_BLOCK):
            base = i * tq + s * BAND_BLOCK - DIL_HALF
            lo = jnp.maximum(r_io, -base)
            hi = jnp.minimum(r_io + 2 * DIL_HALF, L - 1 - base)
            valid = (c_io >= lo) & (c_io <= hi)
            out_rows = pl.ds(s * BAND_BLOCK * dil + j, BAND_BLOCK, stride=dil)
            lse_tile = jnp.zeros((BAND_BLOCK, LANES), F32)
            for h in range(N_HEADS_PER_DIL):
                cols = slice(h * HEAD_DIM, (h + 1) * HEAD_DIM)
                q = q_ref[j, s * BAND_BLOCK:(s + 1) * BAND_BLOCK, cols]
                k = kw_ref[j, s * BAND_BLOCK:s * BAND_BLOCK + DIL_KB, cols]
                v = vw_ref[j, s * BAND_BLOCK:s * BAND_BLOCK + DIL_KB, cols]
                sc = lax.dot_general(q, k, (((1,), (1,)), ((), ())), preferred_element_type=F32) * SCALE
                sc = jnp.where(valid, sc, NEG)
                m = jnp.max(sc, axis=-1, keepdims=True)
                p = jnp.exp(sc - m)
                l = jnp.sum(p, axis=-1, keepdims=True)
                os_ref[h, out_rows, :] = jnp.dot(p.astype(BF16), v, preferred_element_type=F32) / l
                lse_tile = jnp.where(lane == h, m + jnp.log(l), lse_tile)
            ls_ref[out_rows, :] = lse_tile

    if dil == 1:
        phase(0)
    else:
        def body(j, c):
            phase(j)
            return c
        lax.fori_loop(0, dil, body, 0, unroll=2)
    for h in range(N_HEADS_PER_DIL):
        o_ref[:, h * HEAD_DIM:(h + 1) * HEAD_DIM] = os_ref[h].astype(BF16)
    lse_ref[...] = ls_ref[...]


def _dilattn(qkv, col0, B, T, g):
    dil = DIL_CONFIGS[g][1]
    assert DIL_CONFIGS[g][0] // 2 // dil == DIL_HALF
    L = T // dil
    tq = min(DIL_TOKENS_PER_STEP[g] // dil, L)
    nh = L // DIL_HALF
    ratio = tq // DIL_HALF
    blk = lambda rws, fn: pl.BlockSpec((None, dil, rws, COL_BLK), fn)
    prev = lambda i: jnp.maximum(i * ratio - 1, 0)
    nxt = lambda i: jnp.minimum((i + 1) * ratio, nh - 1)
    o, lse = pl.pallas_call(
        functools.partial(_dilattn_kernel, L=L, tq=tq, dil=dil),
        grid=(B, L // tq),
        in_specs=[
            blk(tq, lambda b, i: (b, 0, i, col0)),
            blk(DIL_HALF, lambda b, i: (b, 0, prev(i), col0 + 1)),
            blk(tq, lambda b, i: (b, 0, i, col0 + 1)),
            blk(DIL_HALF, lambda b, i: (b, 0, nxt(i), col0 + 1)),
            blk(DIL_HALF, lambda b, i: (b, 0, prev(i), col0 + 2)),
            blk(tq, lambda b, i: (b, 0, i, col0 + 2)),
            blk(DIL_HALF, lambda b, i: (b, 0, nxt(i), col0 + 2)),
        ],
        out_specs=[
            pl.BlockSpec((None, tq * dil, W_GRP), lambda b, i: (b, i, 0)),
            pl.BlockSpec((None, tq * dil, LANES), lambda b, i: (b, i, 0)),
        ],
        out_shape=[
            jax.ShapeDtypeStruct((B, T, W_GRP), BF16),
            jax.ShapeDtypeStruct((B, T, LANES), F32),
        ],
        scratch_shapes=[pltpu.VMEM((dil, tq + 2 * DIL_HALF, COL_BLK), BF16)] * 2
        + [pltpu.VMEM((N_HEADS_PER_DIL, tq * dil, HEAD_DIM), F32), pltpu.VMEM((tq * dil, LANES), F32)],
        compiler_params=_cparams(("arbitrary", "arbitrary")),
        name=f"dilattn{g}",
    )(qkv, qkv, qkv, qkv, qkv, qkv, qkv)
    return o.reshape(B * T, W_GRP), lse.reshape(B * T, LANES)


def _outproj_kernel(a_ref, o0_ref, o1_ref, o2_ref, l0_ref, l1_ref, l2_ref, ga_ref, gb_ref, x_ref,
                    wa_ref, wb_ref, wo_ref, g2_ref, wr_ref, x1_ref, lg_ref):
    o_refs = (o0_ref, o1_ref, o2_ref)
    lses = [r[...] for r in (l0_ref, l1_ref, l2_ref)]
    parts = []
    for h in range(N_HEADS_PER_DIL):
        cols = slice(h * HEAD_DIM, (h + 1) * HEAD_DIM)
        lh = [l[:, h:h + 1] for l in lses]
        mx = jnp.maximum(jnp.maximum(lh[0], lh[1]), lh[2])
        e = [jnp.exp(v - mx) for v in lh]
        den = e[0] + e[1] + e[2]
        acc = None
        for g in range(len(DIL_CONFIGS)):
            t = (e[g] / den) * o_refs[g][:, cols].astype(F32)
            acc = t if acc is None else acc + t
        parts.append(acc.astype(BF16))
    att_b = jnp.concatenate(parts, axis=-1)
    ya = jnp.dot(a_ref[...], wa_ref[...], preferred_element_type=F32)
    yb = jnp.dot(att_b, wb_ref[...], preferred_element_type=F32)
    mix = ga_ref[...].astype(F32) * ya + gb_ref[...].astype(F32) * yb
    x1 = x_ref[...] + jnp.dot(mix.astype(BF16), wo_ref[...], preferred_element_type=F32)
    x1_ref[...] = x1
    ms = jnp.mean(x1 * x1, axis=-1, keepdims=True)
    h2 = x1 * lax.rsqrt(ms + EPS) * g2_ref[...]
    lg_ref[...] = jnp.dot(h2.astype(BF16), wr_ref[...], preferred_element_type=F32)


def _outproj(att_a, os_, lses, gates, x2, wa, wb, wo, g2, wr, tm):
    n = x2.shape[0]
    row = lambda w: pl.BlockSpec((tm, w), lambda i: (i, 0))
    full = lambda a: pl.BlockSpec(a.shape, lambda i: (0,) * a.ndim)
    return pl.pallas_call(
        _outproj_kernel,
        grid=(n // tm,),
        in_specs=[row(W_NA), row(W_GRP), row(W_GRP), row(W_GRP), row(LANES), row(LANES), row(LANES),
                  pl.BlockSpec((tm, D_MODEL), lambda i: (i, 0)),
                  pl.BlockSpec((tm, D_MODEL), lambda i: (i, 1)),
                  row(D_MODEL), full(wa), full(wb), full(wo), full(g2), full(wr)],
        out_specs=[row(D_MODEL), row(LANES)],
        out_shape=[jax.ShapeDtypeStruct((n, D_MODEL), F32), jax.ShapeDtypeStruct((n, LANES), F32)],
        compiler_params=_cparams(("arbitrary",)),
        name="outproj",
    )(att_a, *os_, *lses, gates, gates, x2, wa, wb, wo, g2, wr)


ROUTE_E0 = N_GROUPS


def _route_kernel(lg_ref, b_ref, gate_ref, eid_ref, rank_ref, cnt_ref, carry_ref):
    i = pl.program_id(0)

    @pl.when(i == 0)
    def _():
        carry_ref[...] = jnp.zeros_like(carry_ref)

    z = lg_ref[...] + b_ref[...]
    tm = z.shape[0]
    lane = lax.broadcasted_iota(jnp.int32, z.shape, 1)
    lane_f = lane.astype(F32)
    rmax = lambda v: jnp.max(v, axis=-1, keepdims=True)
    rmin = lambda v: jnp.min(v, axis=-1, keepdims=True)
    rsum = lambda v: jnp.sum(v, axis=-1, keepdims=True)

    gmask = lane < N_GROUPS
    gmax = rmax(jnp.where(gmask, z, -jnp.inf))
    gsel = rmin(jnp.where(gmask & (z == gmax), lane_f, float(LANES)))
    gp = 1.0 / rsum(jnp.where(gmask, jnp.exp(z - gmax), 0.0))

    elo = float(ROUTE_E0) + float(EXPERTS_PER_GROUP) * gsel
    emask = (lane_f >= elo) & (lane_f < elo + float(EXPERTS_PER_GROUP))
    emax = rmax(jnp.where(emask, z, -jnp.inf))
    ee = jnp.where(emask, jnp.exp(z - emax), 0.0)
    eprob = ee / rsum(ee)
    p1 = rmax(jnp.where(emask, eprob, -1.0))
    i1 = rmin(jnp.where(emask & (eprob == p1), lane_f, float(LANES)))
    m2 = emask & (lane_f != i1)
    p2 = rmax(jnp.where(m2, eprob, -1.0))
    i2 = rmin(jnp.where(m2 & (eprob == p2), lane_f, float(LANES)))
    den = p1 + p2
    g1 = gp * p1 / den
    g2 = gp * p2 / den

    oh1 = (lane_f == i1).astype(F32)
    oh2 = (lane_f == i2).astype(F32)
    oh = oh1 + oh2
    tri = (lax.broadcasted_iota(jnp.int32, (tm, tm), 1) < lax.broadcasted_iota(jnp.int32, (tm, tm), 0)).astype(BF16)
    tot = carry_ref[...] + jnp.dot(tri, oh.astype(BF16), preferred_element_type=F32)
    r1 = rsum(oh1 * tot)
    r2 = rsum(oh2 * tot)
    carry = carry_ref[...] + jnp.sum(oh, axis=0, keepdims=True)
    carry_ref[...] = carry
    cnt_ref[...] = carry

    two = lambda a, b: jnp.where(lane == 0, a, jnp.where(lane == 1, b, 0.0))
    gate_ref[...] = two(g1, g2)
    eid_ref[...] = two(i1 - float(ROUTE_E0), i2 - float(ROUTE_E0)).astype(jnp.int32)
    rank_ref[...] = two(r1, r2).astype(jnp.int32)


def _route(logits, rbias, tm):
    n = logits.shape[0]
    row = pl.BlockSpec((tm, LANES), lambda i: (i, 0))
    one = pl.BlockSpec((1, LANES), lambda i: (0, 0))
    return pl.pallas_call(
        _route_kernel,
        grid=(n // tm,),
        in_specs=[row, one],
        out_specs=[row, row, row, one],
        out_shape=[jax.ShapeDtypeStruct((n, LANES), F32), jax.ShapeDtypeStruct((n, LANES), jnp.int32),
                   jax.ShapeDtypeStruct((n, LANES), jnp.int32), jax.ShapeDtypeStruct((1, LANES), F32)],
        scratch_shapes=[pltpu.VMEM((1, LANES), F32)],
        compiler_params=_cparams(("arbitrary",)),
        name="route",
    )(logits, rbias)


TOP_K = 2


ROW_SUB = 8
HALF_D = D_MODEL // 2
assert ROW_SUB * LANES == HALF_D


def _pack_pair(lo, hi):
    lo_b = lax.bitcast_convert_type(lo.astype(BF16).astype(F32), jnp.uint32)
    hi_b = lax.bitcast_convert_type(hi.astype(BF16).astype(F32), jnp.uint32)
    return (lo_b >> 16) | hi_b


def _unpack_pair(w):
    lo = lax.bitcast_convert_type(w << 16, F32)
    hi = lax.bitcast_convert_type(w & jnp.uint32(0xFFFF0000), F32)
    return lo, hi


def _store_packed_rows(dst, x):
    for c in range(ROW_SUB):
        dst(c, _pack_pair(x[:, c * LANES:(c + 1) * LANES], x[:, HALF_D + c * LANES:HALF_D + (c + 1) * LANES]))


def _dispatch_kernel(pos_ref, *refs, bounds):
    nsrc = len(bounds) - 1
    x1_refs, (g2_ref, xs_ref, hp_ref, sem) = refs[:nsrc], refs[nsrc:]
    i = pl.program_id(0)
    last = pl.num_programs(0) - 1
    slot = i % 2
    tm = x1_refs[0].shape[0]

    def put(c, words):
        hp_ref[slot, pl.ds(c, tm, stride=ROW_SUB), :] = words

    for s in range(nsrc):
        @pl.when((i >= bounds[s]) & (i < bounds[s + 1]))
        def _(x1_ref=x1_refs[s]):
            x1 = x1_ref[...]
            ms = jnp.mean(x1 * x1, axis=-1, keepdims=True)
            _store_packed_rows(put, x1 * lax.rsqrt(ms + EPS) * g2_ref[...])

    def row_copy(sl, r, p):
        src = hp_ref.at[sl, pl.ds(pl.multiple_of(r * ROW_SUB, ROW_SUB), ROW_SUB)]
        dst = xs_ref.at[pl.ds(pl.multiple_of(p * ROW_SUB, ROW_SUB), ROW_SUB)]
        return pltpu.make_async_copy(src, dst, sem.at[sl])

    def start(r, c):
        for k in range(TOP_K):
            row_copy(slot, r, pos_ref[0, TOP_K * r + k]).start(priority=k)
        return c

    def wait_all(sl):
        def wait(r, c):
            for k in range(TOP_K):
                row_copy(sl, 0, 0).wait()
            return c
        lax.fori_loop(0, tm, wait, 0)

    lax.fori_loop(0, tm, start, 0, unroll=4)

    @pl.when(i > 0)
    def _():
        wait_all(1 - slot)

    @pl.when(i == last)
    def _():
        wait_all(slot)


def _dispatch(pos3, x1s, g2, tm):
    bounds = [0]
    for x1 in x1s:
        bounds.append(bounds[-1] + x1.shape[0] // tm)
    n = bounds[-1] * tm
    src = lambda s: pl.BlockSpec(
        (tm, D_MODEL), lambda i: (jnp.clip(i - bounds[s], 0, bounds[s + 1] - bounds[s] - 1), 0))
    return pl.pallas_call(
        functools.partial(_dispatch_kernel, bounds=tuple(bounds)),
        grid=(bounds[-1],),
        in_specs=[pl.BlockSpec((None, 1, TOP_K * tm), lambda i: (i, 0, 0), memory_space=pltpu.SMEM)]
        + [src(s) for s in range(len(x1s))]
        + [pl.BlockSpec((1, D_MODEL), lambda i: (0, 0))],
        out_specs=pl.BlockSpec(memory_space=pl.ANY),
        out_shape=jax.ShapeDtypeStruct((n * TOP_K * ROW_SUB, LANES), jnp.uint32),
        scratch_shapes=[pltpu.VMEM((2, tm * ROW_SUB, LANES), jnp.uint32), pltpu.SemaphoreType.DMA((2,))],
        compiler_params=_cparams(("arbitrary",)),
        name="dispatch",
    )(pos3, *x1s, g2)


def _combine_kernel(pos_ref, posn_ref, gate_ref, x1_ref, ys_ref, y_ref, buf_ref, sem):
    i = pl.program_id(0)
    last = pl.num_programs(0) - 1
    slot = i % 2
    tm = x1_ref.shape[0]

    def row_copy(sl, k, r, p):
        src = ys_ref.at[pl.ds(pl.multiple_of(p * ROW_SUB, ROW_SUB), ROW_SUB)]
        dst = buf_ref.at[sl, k, pl.ds(pl.multiple_of(r * ROW_SUB, ROW_SUB), ROW_SUB)]
        return pltpu.make_async_copy(src, dst, sem.at[sl])

    def start_all(sl, p_ref):
        def start(r, c):
            for k in range(TOP_K):
                row_copy(sl, k, r, p_ref[0, TOP_K * r + k]).start(priority=k)
            return c
        lax.fori_loop(0, tm, start, 0, unroll=4)

    @pl.when(i == 0)
    def _():
        start_all(0, pos_ref)

    @pl.when(i < last)
    def _():
        start_all(1 - slot, posn_ref)

    def wait(r, c):
        for k in range(TOP_K):
            row_copy(slot, k, 0, 0).wait()
        return c

    lax.fori_loop(0, tm, wait, 0)
    gates = gate_ref[...]
    g0 = jnp.broadcast_to(gates[:, 0:1], (tm, LANES))
    g1 = jnp.broadcast_to(gates[:, 1:2], (tm, LANES))
    for c in range(ROW_SUB):
        lo0, hi0 = _unpack_pair(buf_ref[slot, 0, pl.ds(c, tm, stride=ROW_SUB), :])
        lo1, hi1 = _unpack_pair(buf_ref[slot, 1, pl.ds(c, tm, stride=ROW_SUB), :])
        cl = slice(c * LANES, (c + 1) * LANES)
        ch = slice(HALF_D + c * LANES, HALF_D + (c + 1) * LANES)
        y_ref[:, cl] = x1_ref[:, cl] + (g0 * lo0 + g1 * lo1)
        y_ref[:, ch] = x1_ref[:, ch] + (g0 * hi0 + g1 * hi1)


def _combine(pos3, gates, x1, ys, tm, row0):
    n = x1.shape[0]
    nt = n // tm
    b0 = row0 // tm
    pos_blk = lambda fn: pl.BlockSpec((None, 1, TOP_K * tm), fn, memory_space=pltpu.SMEM)
    return pl.pallas_call(
        _combine_kernel,
        grid=(nt,),
        in_specs=[pos_blk(lambda i: (i + b0, 0, 0)),
                  pos_blk(lambda i: (jnp.minimum(i + 1, nt - 1) + b0, 0, 0)),
                  pl.BlockSpec((tm, LANES), lambda i: (i + b0, 0)),
                  pl.BlockSpec((tm, D_MODEL), lambda i: (i, 0)),
                  pl.BlockSpec(memory_space=pl.ANY)],
        out_specs=pl.BlockSpec((tm, D_MODEL), lambda i: (i, 0)),
        out_shape=jax.ShapeDtypeStruct((n, D_MODEL), F32),
        scratch_shapes=[pltpu.VMEM((2, TOP_K, tm * ROW_SUB, LANES), jnp.uint32), pltpu.SemaphoreType.DMA((2,))],
        compiler_params=_cparams(("arbitrary",)),
        name="combine",
    )(pos3, pos3, gates, x1, ys)


def _experts_kernel(tile_s, exp_s, lo_s, hi_s, first_s, x_ref, wg_ref, wu_ref, wd_ref, y_ref, xb_ref):
    w = pl.program_id(0)
    lo, hi = lo_s[w], hi_s[w]
    bm = xb_ref.shape[0]

    def work_item(first_visit):
        for c in range(ROW_SUB):
            xl, xh = _unpack_pair(x_ref[pl.ds(c, bm, stride=ROW_SUB), :])
            xb_ref[:, c * LANES:(c + 1) * LANES] = xl.astype(BF16)
            xb_ref[:, HALF_D + c * LANES:HALF_D + (c + 1) * LANES] = xh.astype(BF16)
        x = xb_ref[...]
        a = jnp.dot(x, wg_ref[...], preferred_element_type=F32)
        u = jnp.dot(x, wu_ref[...], preferred_element_type=F32)
        hm = (a * jax.nn.sigmoid(a) * u).astype(BF16)
        y = jnp.dot(hm, wd_ref[...], preferred_element_type=F32)
        rows = lax.broadcasted_iota(jnp.int32, (bm, LANES), 0)
        mine = (rows >= lo) & (rows < hi)

        def put(c, words):
            rws = pl.ds(c, bm, stride=ROW_SUB)
            y_ref[rws, :] = jnp.where(mine, words, jnp.uint32(0) if first_visit else y_ref[rws, :])
        _store_packed_rows(put, y)

    @pl.when((hi > lo) & (first_s[w] == 1))
    def _():
        work_item(True)

    @pl.when((hi > lo) & (first_s[w] == 0))
    def _():
        work_item(False)


def _experts(meta, xs, wg, wu, wd, bm):
    p = xs.shape[0] // ROW_SUB
    nw = p // bm + N_EXPERTS
    grid_spec = pltpu.PrefetchScalarGridSpec(
        num_scalar_prefetch=5,
        grid=(nw,),
        in_specs=[
            pl.BlockSpec((bm * ROW_SUB, LANES), lambda w, t, e, lo, hi, f: (t[w], 0)),
            pl.BlockSpec((None, D_MODEL, D_EXPERT), lambda w, t, e, lo, hi, f: (e[w], 0, 0)),
            pl.BlockSpec((None, D_MODEL, D_EXPERT), lambda w, t, e, lo, hi, f: (e[w], 0, 0)),
            pl.BlockSpec((None, D_EXPERT, D_MODEL), lambda w, t, e, lo, hi, f: (e[w], 0, 0)),
        ],
        out_specs=pl.BlockSpec((bm * ROW_SUB, LANES), lambda w, t, e, lo, hi, f: (t[w], 0)),
        scratch_shapes=[pltpu.VMEM((bm, D_MODEL), BF16)],
    )
    return pl.pallas_call(
        _experts_kernel,
        grid_spec=grid_spec,
        out_shape=jax.ShapeDtypeStruct((p * ROW_SUB, LANES), jnp.uint32),
        compiler_params=_cparams(("arbitrary",)),
        name="experts",
    )(*meta, xs, wg, wu, wd)


def _expert_work_items(counts, p, bm):
    ntiles = p // bm
    nw = ntiles + N_EXPERTS
    ends = jnp.cumsum(counts)
    starts = ends - counts
    ft = starts // bm
    nt = jnp.where(counts > 0, (ends - 1) // bm - ft + 1, 0)
    wend = jnp.cumsum(nt)
    wstart = wend - nt
    w = jnp.arange(nw, dtype=jnp.int32)
    used = w < wend[-1]
    e = jnp.minimum(jnp.sum((w[:, None] >= wend[None, :]).astype(jnp.int32), axis=1), N_EXPERTS - 1)
    e_last = jnp.max(jnp.where(counts > 0, jnp.arange(N_EXPERTS, dtype=jnp.int32), 0))
    e = jnp.where(used, e, e_last)
    tile = jnp.where(used, ft[e] + (w - wstart[e]), ntiles - 1).astype(jnp.int32)
    lo = jnp.where(used, jnp.maximum(starts[e], tile * bm) - tile * bm, 0).astype(jnp.int32)
    hi = jnp.where(used, jnp.minimum(ends[e], (tile + 1) * bm) - tile * bm, 0).astype(jnp.int32)
    first = jnp.concatenate([jnp.ones((1,), jnp.int32), (tile[1:] != tile[:-1]).astype(jnp.int32)])
    return (tile, e, lo, hi, first), starts


def _rope_tables(T):
    half = ROT_DIM // 2
    inv = 1.0 / (ROPE_THETA ** (np.arange(half, dtype=np.float64) * (2.0 / ROT_DIM)))
    ang = np.arange(T, dtype=np.float64)[:, None] * inv[None, :]
    cos, sin = np.cos(ang), np.sin(ang)
    rest = HEAD_DIM - ROT_DIM
    z_half, z_rest = np.zeros((T, half)), np.zeros((T, rest))
    cos_t = np.concatenate([cos, cos, np.ones((T, rest))], axis=1)
    sin_t = np.concatenate([-sin, sin, z_rest], axis=1)
    cos_t = np.stack([np.ones_like(cos_t), cos_t])
    sin_t = np.stack([np.zeros_like(sin_t), sin_t])
    return tuple(jnp.asarray(t, dtype=F32) for t in (cos_t, sin_t))


TM_IN = 1024
TM_OUT = 256
TM_ROUTE = 512
TM_MOVE = 512
BM_EXPERT = 256


def _attention_half(x, p):
    B, T, D = x.shape
    n = B * T
    assert D == D_MODEL and T % NA_TQ == 0 and T // GRID_W >= NA_ROWS + NA_HALO_ROWS
    x2 = x.reshape(n, D)
    main, qkv1, qkv2, sig = _inproj(x2, p["g1"], p["w_in"], p["gains"], *_rope_tables(T), B, T, TM_IN)
    main3 = main.reshape(B, T, JB_G1 * COL_BLK)
    att_a = _natten(main3, p["na_bias"], B, T).reshape(n, W_NA)
    dil = [_dilattn(main3[:, None], JB_G0, B, T, 0), _dilattn(qkv1, 0, B, T, 1), _dilattn(qkv2, 0, B, T, 2)]
    return _outproj(att_a, [d[0] for d in dil], [d[1] for d in dil], sig, x2,
                    p["wa"], p["wb"], p["wo"], p["g2"], p["wr"], TM_OUT)


def _layer(xs_in, p):
    halves = [_attention_half(x, p) for x in xs_in]
    x1s = [h[0] for h in halves]
    sizes = [x1.shape[0] for x1 in x1s]
    n = sum(sizes)
    logits = jnp.concatenate([h[1] for h in halves], axis=0)

    gates, eid, rank, cnt = _route(logits, p["rbias"], TM_ROUTE)
    counts = cnt[0, ROUTE_E0:ROUTE_E0 + N_EXPERTS].astype(jnp.int32)
    meta, starts = _expert_work_items(counts, n * TOP_K, BM_EXPERT)
    is_e = eid[:, :TOP_K, None] == jnp.arange(N_EXPERTS, dtype=jnp.int32)
    pos = jnp.sum(jnp.where(is_e, starts, 0), axis=-1) + rank[:, :TOP_K]
    pos3 = pos.astype(jnp.int32).reshape(n // TM_MOVE, 1, TOP_K * TM_MOVE)

    xs = _dispatch(pos3, x1s, p["g2"], TM_MOVE)
    ys = _experts(meta, xs, p["wg"], p["wu"], p["wd"], BM_EXPERT)
    outs, row0 = [], 0
    for x, x1, m in zip(xs_in, x1s, sizes):
        outs.append(_combine(pos3, gates, x1, ys, TM_MOVE, row0).reshape(x.shape))
        row0 += m
    return tuple(outs)


def _prepare_params(norm1_g, w_in, qn_a, kn_a, rpb_a, qn_b, kn_b, w_branch_a, w_branch_b, w_out,
                    norm2_g, router_group_w, router_group_b, router_expert_w, router_expert_b, w_gate, w_up, w_down):
    assert norm1_g.shape[0] == 1
    l = 0
    pad_r = LANES - N_GROUPS - N_EXPERTS
    zrow = jnp.concatenate([jnp.ones((1, HEAD_DIM), F32), jnp.zeros((3, HEAD_DIM), F32)], axis=0)
    col = lambda start, width: w_in[l][:, start:start + width]
    qb0, kb0, vb0 = 3 * W_NA, 3 * W_NA + W_DIL, 3 * W_NA + 2 * W_DIL
    pieces = [col(0, 3 * W_NA)]
    for g in range(len(DIL_CONFIGS)):
        pieces += [col(qb0 + g * W_GRP, W_GRP), col(kb0 + g * W_GRP, W_GRP), col(vb0 + g * W_GRP, W_GRP)]
    pieces.append(col(3 * W_NA + 3 * W_DIL, 2 * D_MODEL))
    return {
        "g1": norm1_g[l][None, :],
        "w_in": jnp.concatenate(pieces, axis=1).astype(BF16),
        "gains": jnp.concatenate([qn_a[l][None], kn_a[l][None], qn_b[l][None], kn_b[l][None], zrow], axis=0),
        "na_bias": _na_bias_table(rpb_a[l]),
        "wa": w_branch_a[l].astype(BF16),
        "wb": w_branch_b[l].astype(BF16),
        "wo": w_out[l].astype(BF16),
        "g2": norm2_g[l][None, :],
        "wr": jnp.concatenate([router_group_w[l], router_expert_w[l], jnp.zeros((D_MODEL, pad_r), F32)],
                              axis=1).astype(BF16),
        "rbias": jnp.concatenate([router_group_b[l], router_expert_b[l], jnp.zeros((pad_r,), F32)])[None, :],
        "wg": w_gate[l].astype(BF16),
        "wu": w_up[l].astype(BF16),
        "wd": w_down[l].astype(BF16),
    }


def kernel(x_prompt, x_sample, norm1_g, w_in, qn_a, kn_a, rpb_a, qn_b, kn_b, w_branch_a, w_branch_b, w_out,
           norm2_g, router_group_w, router_group_b, router_expert_w, router_expert_b, w_gate, w_up, w_down):
    p = _prepare_params(norm1_g, w_in, qn_a, kn_a, rpb_a, qn_b, kn_b, w_branch_a, w_branch_b, w_out, norm2_g,
                        router_group_w, router_group_b, router_expert_w, router_expert_b, w_gate, w_up, w_down)
    return _layer((x_prompt, x_sample), p)
```

```python
import functools

import jax
import jax.numpy as jnp
import numpy as np
from jax import lax
from jax.experimental import pallas as pl
from jax.experimental.pallas import tpu as pltpu

F32 = jnp.float32
BF16 = jnp.bfloat16

D_MODEL = 2048
HEAD_DIM = 128
N_HEADS_NA = 4
N_HEADS_PER_DIL = 4
DIL_CONFIGS = ((128, 1), (512, 4), (2048, 16))
W_NA = N_HEADS_NA * HEAD_DIM
W_DIL = N_HEADS_PER_DIL * len(DIL_CONFIGS) * HEAD_DIM
W_GRP = N_HEADS_PER_DIL * HEAD_DIM
GRID_W = 64
NA_ROWS = 8
NA_COLS = 16
ROT_DIM = HEAD_DIM // 4
ROPE_THETA = 500000.0
BAND_BLOCK = 128
N_GROUPS = 4
EXPERTS_PER_GROUP = 8
N_EXPERTS = N_GROUPS * EXPERTS_PER_GROUP
D_EXPERT = 1024
EPS = 1e-6
NEG = -1e30
IN_WIDTH = 3 * W_NA + 3 * W_DIL + 2 * D_MODEL
SCALE = HEAD_DIM ** -0.5

COL_BLK = 512
JB_G0, JB_G1, JB_G2, JB_GATE = 3, 6, 9, 12
N_COL_BLK = IN_WIDTH // COL_BLK

LANES = 128
VMEM_LIMIT = 56 * 1024 * 1024


def _cparams(sem, vmem=VMEM_LIMIT):
    return pltpu.CompilerParams(dimension_semantics=sem, vmem_limit_bytes=vmem)


def _inproj_kernel(x_hbm, g1_ref, w_ref, gains_ref, cos_ref, sin_ref,
                   main_ref, d1_ref, d2_ref, gate_ref, h_ref, acc_ref, y_ref, xbuf_ref, xsem):
    i = pl.program_id(0)
    j = pl.program_id(1)
    jb = j - 1
    nh = COL_BLK // HEAD_DIM
    tm = h_ref.shape[0]
    slot = i % 2
    kind = jb % 3
    is_v = kind == 2
    gain_row = jnp.where(is_v, 4, kind + jnp.where(jb < JB_G0, 0, 2))

    def x_copy(tile, sl):
        rows = pl.ds(pl.multiple_of(tile * tm, tm), tm)
        return pltpu.make_async_copy(x_hbm.at[rows], xbuf_ref.at[sl], xsem.at[sl])

    def matmul():
        acc_ref[...] = jnp.dot(h_ref[...], w_ref[...], preferred_element_type=F32)

    @pl.when((i == 0) & (j == 0))
    def _():
        x_copy(0, 0).start()

    @pl.when((j == 1) & (i + 1 < pl.num_programs(0)))
    def _():
        x_copy(i + 1, 1 - slot).start()

    @pl.when(j == 0)
    def _():
        x_copy(i, slot).wait()
        x = xbuf_ref[slot]
        ms = jnp.mean(x * x, axis=-1, keepdims=True)
        h_ref[...] = (x * lax.rsqrt(ms + EPS) * g1_ref[...]).astype(BF16)
        matmul()

    def finished_head(hh, gain, cos, sin, low):
        xh = acc_ref[:, hh * HEAD_DIM:(hh + 1) * HEAD_DIM]
        ms = jnp.mean(xh * xh, axis=-1, keepdims=True)
        y = xh * jnp.where(is_v, 1.0, lax.rsqrt(ms + EPS)) * gain
        partner = jnp.where(low, pltpu.roll(y, HEAD_DIM - ROT_DIM // 2, 1), pltpu.roll(y, ROT_DIM // 2, 1))
        return y * cos + partner * sin

    def attention_block(store_head, after=None):
        gain = gains_ref[pl.ds(gain_row, 1), :]
        cos, sin = jnp.where(is_v, 1.0, cos_ref[...]), jnp.where(is_v, 0.0, sin_ref[...])
        low = lax.broadcasted_iota(jnp.int32, cos.shape, 1) < ROT_DIM // 2
        for hh in range(nh):
            store_head(hh, finished_head(hh, gain, cos, sin, low))
        if after is not None:
            after()
        matmul()

    def to_main(hh, y):
        main_ref[:, hh * HEAD_DIM:(hh + 1) * HEAD_DIM] = y.astype(BF16)

    def to_scratch(hh, y):
        y_ref[hh] = y

    def deinterleave(dst_ref, dil):
        rows = y_ref.shape[1] // dil
        for ph in range(dil):
            for hh in range(nh):
                dst_ref[ph, :, hh * HEAD_DIM:(hh + 1) * HEAD_DIM] = (
                    y_ref[hh, pl.ds(ph, rows, stride=dil), :].astype(BF16))

    @pl.when((jb >= 0) & (jb < JB_G1))
    def _():
        attention_block(to_main)

    @pl.when((jb >= JB_G1) & (jb < JB_G2))
    def _():
        attention_block(to_scratch, lambda: deinterleave(d1_ref, DIL_CONFIGS[1][1]))

    @pl.when((jb >= JB_G2) & (jb < JB_GATE))
    def _():
        attention_block(to_scratch, lambda: deinterleave(d2_ref, DIL_CONFIGS[2][1]))

    def gates():
        gate_ref[...] = (0.5 * jnp.tanh(0.5 * acc_ref[...]) + 0.5).astype(BF16)

    @pl.when((jb >= JB_GATE) & (j < N_COL_BLK))
    def _():
        gates()
        matmul()

    @pl.when(j == N_COL_BLK)
    def _():
        gates()


def _inproj(x2, g1, w_in_bf, gains, cos_t, sin_t, B, T, tm):
    n = x2.shape[0]
    tpos = T // tm
    d1, d2 = DIL_CONFIGS[1][1], DIL_CONFIGS[2][1]
    rot = lambda j: jnp.where(j - 1 >= JB_G0, 1, 0)
    tab = pl.BlockSpec((None, tm, HEAD_DIM), lambda i, j: (rot(j), i % tpos, 0))
    jb = lambda j: jnp.maximum(j - 1, 0)
    return pl.pallas_call(
        _inproj_kernel,
        grid=(n // tm, N_COL_BLK + 1),
        in_specs=[
            pl.BlockSpec(memory_space=pl.ANY),
            pl.BlockSpec((1, D_MODEL), lambda i, j: (0, 0)),
            pl.BlockSpec((D_MODEL, COL_BLK), lambda i, j: (0, jnp.minimum(j, N_COL_BLK - 1))),
            pl.BlockSpec((8, HEAD_DIM), lambda i, j: (0, 0)),
            tab, tab,
        ],
        out_specs=[
            pl.BlockSpec((tm, COL_BLK), lambda i, j: (i, jnp.minimum(jb(j), JB_G1 - 1))),
            pl.BlockSpec((None, d1, tm // d1, COL_BLK),
                         lambda i, j: (i // tpos, 0, i % tpos, jnp.clip(jb(j) - JB_G1, 0, 2))),
            pl.BlockSpec((None, d2, tm // d2, COL_BLK),
                         lambda i, j: (i // tpos, 0, i % tpos, jnp.clip(jb(j) - JB_G2, 0, 2))),
            pl.BlockSpec((tm, COL_BLK), lambda i, j: (i, jnp.clip(jb(j) - JB_GATE, 0, N_COL_BLK - JB_GATE - 1))),
        ],
        out_shape=[
            jax.ShapeDtypeStruct((n, JB_G1 * COL_BLK), BF16),
            jax.ShapeDtypeStruct((B, d1, T // d1, 3 * COL_BLK), BF16),
            jax.ShapeDtypeStruct((B, d2, T // d2, 3 * COL_BLK), BF16),
            jax.ShapeDtypeStruct((n, 2 * D_MODEL), BF16),
        ],
        scratch_shapes=[pltpu.VMEM((tm, D_MODEL), BF16), pltpu.VMEM((tm, COL_BLK), F32),
                        pltpu.VMEM((COL_BLK // HEAD_DIM, tm, HEAD_DIM), F32),
                        pltpu.VMEM((2, tm, D_MODEL), F32), pltpu.SemaphoreType.DMA((2,))],
        compiler_params=_cparams(("arbitrary", "arbitrary")),
        name="inproj",
    )(x2, g1, w_in_bf, gains, cos_t, sin_t)


NA_CHUNK_ROWS = 16
NA_HALO_ROWS = 4
NA_TQ = NA_CHUNK_ROWS * GRID_W
NA_TH = NA_HALO_ROWS * GRID_W
NA_GROUP = 4
NA_SPAN_ROWS = 12
NA_GQ = NA_GROUP * GRID_W
NA_SPAN = NA_SPAN_ROWS * GRID_W
NA_CASES = 3
assert NA_CHUNK_ROWS % NA_GROUP == 0 and NA_HALO_ROWS == NA_ROWS // 2
assert NA_ROWS + NA_GROUP - 1 <= NA_SPAN_ROWS <= NA_GROUP + 2 * NA_HALO_ROWS


def _natten_kernel(q_ref, kp_ref, km_ref, kn_ref, vp_ref, vm_ref, vn_ref, bias_ref, o_ref, kw_ref, vw_ref, *, rows):
    i = pl.program_id(1)
    kw_ref[0:NA_TH] = kp_ref[...]
    kw_ref[NA_TH:NA_TH + NA_TQ] = km_ref[...]
    kw_ref[NA_TH + NA_TQ:] = kn_ref[...]
    vw_ref[0:NA_TH] = vp_ref[...]
    vw_ref[NA_TH:NA_TH + NA_TQ] = vm_ref[...]
    vw_ref[NA_TH + NA_TQ:] = vn_ref[...]

    for gg in range(NA_CHUNK_ROWS // NA_GROUP):
        r0 = i * NA_CHUNK_ROWS + gg * NA_GROUP
        rs0 = jnp.clip(r0 - NA_ROWS // 2, 0, rows - NA_ROWS)
        case = jnp.where(r0 == 0, 0, jnp.where(r0 == rows - NA_GROUP, 2, 1))
        off = pl.multiple_of((rs0 - (i * NA_CHUNK_ROWS - NA_HALO_ROWS)) * GRID_W, GRID_W)
        for h in range(N_HEADS_NA):
            cols = slice(h * HEAD_DIM, (h + 1) * HEAD_DIM)
            q = q_ref[gg * NA_GQ:(gg + 1) * NA_GQ, cols]
            k = kw_ref[pl.ds(off, NA_SPAN), cols]
            v = vw_ref[pl.ds(off, NA_SPAN), cols]
            s = lax.dot_general(q, k, (((1,), (1,)), ((), ())), preferred_element_type=F32)
            s = s * SCALE + bias_ref[case, h]
            m = jnp.max(s, axis=-1, keepdims=True)
            p = jnp.exp(s - m)
            l = jnp.sum(p, axis=-1, keepdims=True)
            o = jnp.dot(p.astype(BF16), v, preferred_element_type=F32) / l
            o_ref[gg * NA_GQ:(gg + 1) * NA_GQ, cols] = o.astype(BF16)


def _natten(proj3, bias_tab, B, T):
    rows = T // GRID_W
    nh = T // NA_TH
    ratio = NA_TQ // NA_TH
    blk = lambda rws, fn: pl.BlockSpec((None, rws, COL_BLK), fn)
    return pl.pallas_call(
        functools.partial(_natten_kernel, rows=rows),
        grid=(B, T // NA_TQ),
        in_specs=[
            blk(NA_TQ, lambda b, i: (b, i, 0)),
            blk(NA_TH, lambda b, i: (b, jnp.maximum(i * ratio - 1, 0), 1)),
            blk(NA_TQ, lambda b, i: (b, i, 1)),
            blk(NA_TH, lambda b, i: (b, jnp.minimum((i + 1) * ratio, nh - 1), 1)),
            blk(NA_TH, lambda b, i: (b, jnp.maximum(i * ratio - 1, 0), 2)),
            blk(NA_TQ, lambda b, i: (b, i, 2)),
            blk(NA_TH, lambda b, i: (b, jnp.minimum((i + 1) * ratio, nh - 1), 2)),
            pl.BlockSpec((NA_CASES, N_HEADS_NA, NA_GQ, NA_SPAN), lambda b, i: (0, 0, 0, 0)),
        ],
        out_specs=pl.BlockSpec((None, NA_TQ, W_NA), lambda b, i: (b, i, 0)),
        out_shape=jax.ShapeDtypeStruct((B, T, W_NA), BF16),
        scratch_shapes=[pltpu.VMEM((NA_TQ + 2 * NA_TH, COL_BLK), BF16)] * 2,
        compiler_params=_cparams(("arbitrary", "arbitrary")),
        name="natten",
    )(proj3, proj3, proj3, proj3, proj3, proj3, proj3, bias_tab)


def _na_bias_table(rpb):
    c = np.arange(GRID_W)
    cs = np.clip(c - NA_COLS // 2, 0, GRID_W - NA_COLS)
    cc = np.arange(GRID_W)
    valid = (cc[None, :] >= cs[:, None]) & (cc[None, :] < cs[:, None] + NA_COLS)
    dc_idx = np.clip(cc[None, :] - c[:, None] + NA_COLS - 1, 0, 2 * NA_COLS - 2)
    pick = (dc_idx[..., None] == np.arange(2 * NA_COLS - 1)).astype(np.float32)
    t = jnp.einsum("qkd,hrd->hrqk", pick, rpb.astype(F32), precision=lax.Precision.HIGHEST)
    t = jnp.where(valid[None, None], t, NEG)
    half = NA_ROWS // 2
    cases = ([(m, 0) for m in range(NA_GROUP)],
             [(half, m) for m in range(NA_GROUP)],
             [(half + m, 0) for m in range(NA_GROUP)])
    tiles = []
    for case in cases:
        per_row = []
        for var, woff in case:
            win = t[:, NA_ROWS - 1 - var:2 * NA_ROWS - 1 - var]
            pad = ((0, 0), (woff, NA_SPAN_ROWS - NA_ROWS - woff), (0, 0), (0, 0))
            per_row.append(jnp.pad(win, pad, constant_values=NEG))
        tile = jnp.stack(per_row, axis=1)
        tiles.append(tile.transpose(0, 1, 3, 2, 4).reshape(N_HEADS_NA, NA_GQ, NA_SPAN))
    return jnp.stack(tiles, axis=0)


DIL_HALF = 64
DIL_KB = BAND_BLOCK + 2 * DIL_HALF
DIL_TOKENS_PER_STEP = (1024, 1024, 2048)


def _dilattn_kernel(q_ref, kp_ref, km_ref, kn_ref, vp_ref, vm_ref, vn_ref, o_ref, lse_ref,
                    kw_ref, vw_ref, os_ref, ls_ref, *, L, tq, dil):
    i = pl.program_id(1)
    kw_ref[:, 0:DIL_HALF] = kp_ref[...]
    kw_ref[:, DIL_HALF:DIL_HALF + tq] = km_ref[...]
    kw_ref[:, DIL_HALF + tq:] = kn_ref[...]
    vw_ref[:, 0:DIL_HALF] = vp_ref[...]
    vw_ref[:, DIL_HALF:DIL_HALF + tq] = vm_ref[...]
    vw_ref[:, DIL_HALF + tq:] = vn_ref[...]

    def phase(j):
        r_io = lax.broadcasted_iota(jnp.int32, (BAND_BLOCK, DIL_KB), 0)
        c_io = lax.broadcasted_iota(jnp.int32, (BAND_BLOCK, DIL_KB), 1)
        lane = lax.broadcasted_iota(jnp.int32, (BAND_BLOCK, LANES), 1)
        for s in range(tq // BAND_BLOCK):
            base = i * tq + s * BAND_BLOCK - DIL_HALF
            lo = jnp.maximum(r_io, -base)
            hi = jnp.minimum(r_io + 2 * DIL_HALF, L - 1 - base)
            valid = (c_io >= lo) & (c_io <= hi)
            out_rows = pl.ds(s * BAND_BLOCK * dil + j, BAND_BLOCK, stride=dil)
            lse_tile = jnp.zeros((BAND_BLOCK, LANES), F32)
            for h in range(N_HEADS_PER_DIL):
                cols = slice(h * HEAD_DIM, (h + 1) * HEAD_DIM)
                q = q_ref[j, s * BAND_BLOCK:(s + 1) * BAND_BLOCK, cols]
                k = kw_ref[j, s * BAND_BLOCK:s * BAND_BLOCK + DIL_KB, cols]
                v = vw_ref[j, s * BAND_BLOCK:s * BAND_BLOCK + DIL_KB, cols]
                sc = lax.dot_general(q, k, (((1,), (1,)), ((), ())), preferred_element_type=F32) * SCALE
                sc = jnp.where(valid, sc, NEG)
                m = jnp.max(sc, axis=-1, keepdims=True)
                p = jnp.exp(sc - m)
                l = jnp.sum(p, axis=-1, keepdims=True)
                os_ref[h, out_rows, :] = jnp.dot(p.astype(BF16), v, preferred_element_type=F32) / l
                lse_tile = jnp.where(lane == h, m + jnp.log(l), lse_tile)
            ls_ref[out_rows, :] = lse_tile

    if dil == 1:
        phase(0)
    else:
        def body(j, c):
            phase(j)
            return c
        lax.fori_loop(0, dil, body, 0, unroll=2)
    for h in range(N_HEADS_PER_DIL):
        o_ref[:, h * HEAD_DIM:(h + 1) * HEAD_DIM] = os_ref[h].astype(BF16)
    lse_ref[...] = ls_ref[...]


def _dilattn(qkv, col0, B, T, g):
    dil = DIL_CONFIGS[g][1]
    assert DIL_CONFIGS[g][0] // 2 // dil == DIL_HALF
    L = T // dil
    tq = min(DIL_TOKENS_PER_STEP[g] // dil, L)
    nh = L // DIL_HALF
    ratio = tq // DIL_HALF
    blk = lambda rws, fn: pl.BlockSpec((None, dil, rws, COL_BLK), fn)
    prev = lambda i: jnp.maximum(i * ratio - 1, 0)
    nxt = lambda i: jnp.minimum((i + 1) * ratio, nh - 1)
    o, lse = pl.pallas_call(
        functools.partial(_dilattn_kernel, L=L, tq=tq, dil=dil),
        grid=(B, L // tq),
        in_specs=[
            blk(tq, lambda b, i: (b, 0, i, col0)),
            blk(DIL_HALF, lambda b, i: (b, 0, prev(i), col0 + 1)),
            blk(tq, lambda b, i: (b, 0, i, col0 + 1)),
            blk(DIL_HALF, lambda b, i: (b, 0, nxt(i), col0 + 1)),
            blk(DIL_HALF, lambda b, i: (b, 0, prev(i), col0 + 2)),
            blk(tq, lambda b, i: (b, 0, i, col0 + 2)),
            blk(DIL_HALF, lambda b, i: (b, 0, nxt(i), col0 + 2)),
        ],
        out_specs=[
            pl.BlockSpec((None, tq * dil, W_GRP), lambda b, i: (b, i, 0)),
            pl.BlockSpec((None, tq * dil, LANES), lambda b, i: (b, i, 0)),
        ],
        out_shape=[
            jax.ShapeDtypeStruct((B, T, W_GRP), BF16),
            jax.ShapeDtypeStruct((B, T, LANES), F32),
        ],
        scratch_shapes=[pltpu.VMEM((dil, tq + 2 * DIL_HALF, COL_BLK), BF16)] * 2
        + [pltpu.VMEM((N_HEADS_PER_DIL, tq * dil, HEAD_DIM), F32), pltpu.VMEM((tq * dil, LANES), F32)],
        compiler_params=_cparams(("arbitrary", "arbitrary")),
        name=f"dilattn{g}",
    )(qkv, qkv, qkv, qkv, qkv, qkv, qkv)
    return o.reshape(B * T, W_GRP), lse.reshape(B * T, LANES)


def _outproj_kernel(a_ref, o0_ref, o1_ref, o2_ref, l0_ref, l1_ref, l2_ref, ga_ref, gb_ref, x_ref,
                    wa_ref, wb_ref, wo_ref, g2_ref, wr_ref, x1_ref, lg_ref):
    o_refs = (o0_ref, o1_ref, o2_ref)
    lses = [r[...] for r in (l0_ref, l1_ref, l2_ref)]
    parts = []
    for h in range(N_HEADS_PER_DIL):
        cols = slice(h * HEAD_DIM, (h + 1) * HEAD_DIM)
        lh = [l[:, h:h + 1] for l in lses]
        mx = jnp.maximum(jnp.maximum(lh[0], lh[1]), lh[2])
        e = [jnp.exp(v - mx) for v in lh]
        den = e[0] + e[1] + e[2]
        acc = None
        for g in range(len(DIL_CONFIGS)):
            t = (e[g] / den) * o_refs[g][:, cols].astype(F32)
            acc = t if acc is None else acc + t
        parts.append(acc.astype(BF16))
    att_b = jnp.concatenate(parts, axis=-1)
    ya = jnp.dot(a_ref[...], wa_ref[...], preferred_element_type=F32)
    yb = jnp.dot(att_b, wb_ref[...], preferred_element_type=F32)
    mix = ga_ref[...].astype(F32) * ya + gb_ref[...].astype(F32) * yb
    x1 = x_ref[...] + jnp.dot(mix.astype(BF16), wo_ref[...], preferred_element_type=F32)
    x1_ref[...] = x1
    ms = jnp.mean(x1 * x1, axis=-1, keepdims=True)
    h2 = x1 * lax.rsqrt(ms + EPS) * g2_ref[...]
    lg_ref[...] = jnp.dot(h2.astype(BF16), wr_ref[...], preferred_element_type=F32)


def _outproj(att_a, os_, lses, gates, x2, wa, wb, wo, g2, wr, tm):
    n = x2.shape[0]
    row = lambda w: pl.BlockSpec((tm, w), lambda i: (i, 0))
    full = lambda a: pl.BlockSpec(a.shape, lambda i: (0,) * a.ndim)
    return pl.pallas_call(
        _outproj_kernel,
        grid=(n // tm,),
        in_specs=[row(W_NA), row(W_GRP), row(W_GRP), row(W_GRP), row(LANES), row(LANES), row(LANES),
                  pl.BlockSpec((tm, D_MODEL), lambda i: (i, 0)),
                  pl.BlockSpec((tm, D_MODEL), lambda i: (i, 1)),
                  row(D_MODEL), full(wa), full(wb), full(wo), full(g2), full(wr)],
        out_specs=[row(D_MODEL), row(LANES)],
        out_shape=[jax.ShapeDtypeStruct((n, D_MODEL), F32), jax.ShapeDtypeStruct((n, LANES), F32)],
        compiler_params=_cparams(("arbitrary",)),
        name="outproj",
    )(att_a, *os_, *lses, gates, gates, x2, wa, wb, wo, g2, wr)


ROUTE_E0 = N_GROUPS


def _route_kernel(lg_ref, b_ref, gate_ref, eid_ref, rank_ref, cnt_ref, carry_ref):
    i = pl.program_id(0)

    @pl.when(i == 0)
    def _():
        carry_ref[...] = jnp.zeros_like(carry_ref)

    z = lg_ref[...] + b_ref[...]
    tm = z.shape[0]
    lane = lax.broadcasted_iota(jnp.int32, z.shape, 1)
    lane_f = lane.astype(F32)
    rmax = lambda v: jnp.max(v, axis=-1, keepdims=True)
    rmin = lambda v: jnp.min(v, axis=-1, keepdims=True)
    rsum = lambda v: jnp.sum(v, axis=-1, keepdims=True)

    gmask = lane < N_GROUPS
    gmax = rmax(jnp.where(gmask, z, -jnp.inf))
    gsel = rmin(jnp.where(gmask & (z == gmax), lane_f, float(LANES)))
    gp = 1.0 / rsum(jnp.where(gmask, jnp.exp(z - gmax), 0.0))

    elo = float(ROUTE_E0) + float(EXPERTS_PER_GROUP) * gsel
    emask = (lane_f >= elo) & (lane_f < elo + float(EXPERTS_PER_GROUP))
    emax = rmax(jnp.where(emask, z, -jnp.inf))
    ee = jnp.where(emask, jnp.exp(z - emax), 0.0)
    eprob = ee / rsum(ee)
    p1 = rmax(jnp.where(emask, eprob, -1.0))
    i1 = rmin(jnp.where(emask & (eprob == p1), lane_f, float(LANES)))
    m2 = emask & (lane_f != i1)
    p2 = rmax(jnp.where(m2, eprob, -1.0))
    i2 = rmin(jnp.where(m2 & (eprob == p2), lane_f, float(LANES)))
    den = p1 + p2
    g1 = gp * p1 / den
    g2 = gp * p2 / den

    oh1 = (lane_f == i1).astype(F32)
    oh2 = (lane_f == i2).astype(F32)
    oh = oh1 + oh2
    tri = (lax.broadcasted_iota(jnp.int32, (tm, tm), 1) < lax.broadcasted_iota(jnp.int32, (tm, tm), 0)).astype(BF16)
    tot = carry_ref[...] + jnp.dot(tri, oh.astype(BF16), preferred_element_type=F32)
    r1 = rsum(oh1 * tot)
    r2 = rsum(oh2 * tot)
    carry = carry_ref[...] + jnp.sum(oh, axis=0, keepdims=True)
    carry_ref[...] = carry
    cnt_ref[...] = carry

    two = lambda a, b: jnp.where(lane == 0, a, jnp.where(lane == 1, b, 0.0))
    gate_ref[...] = two(g1, g2)
    eid_ref[...] = two(i1 - float(ROUTE_E0), i2 - float(ROUTE_E0)).astype(jnp.int32)
    rank_ref[...] = two(r1, r2).astype(jnp.int32)


def _route(logits, rbias, tm):
    n = logits.shape[0]
    row = pl.BlockSpec((tm, LANES), lambda i: (i, 0))
    one = pl.BlockSpec((1, LANES), lambda i: (0, 0))
    return pl.pallas_call(
        _route_kernel,
        grid=(n // tm,),
        in_specs=[row, one],
        out_specs=[row, row, row, one],
        out_shape=[jax.ShapeDtypeStruct((n, LANES), F32), jax.ShapeDtypeStruct((n, LANES), jnp.int32),
                   jax.ShapeDtypeStruct((n, LANES), jnp.int32), jax.ShapeDtypeStruct((1, LANES), F32)],
        scratch_shapes=[pltpu.VMEM((1, LANES), F32)],
        compiler_params=_cparams(("arbitrary",)),
        name="route",
    )(logits, rbias)


TOP_K = 2


ROW_SUB = 8
HALF_D = D_MODEL // 2
assert ROW_SUB * LANES == HALF_D


def _pack_pair(lo, hi):
    lo_b = lax.bitcast_convert_type(lo.astype(BF16).astype(F32), jnp.uint32)
    hi_b = lax.bitcast_convert_type(hi.astype(BF16).astype(F32), jnp.uint32)
    return (lo_b >> 16) | hi_b


def _unpack_pair(w):
    lo = lax.bitcast_convert_type(w << 16, F32)
    hi = lax.bitcast_convert_type(w & jnp.uint32(0xFFFF0000), F32)
    return lo, hi


def _store_packed_rows(dst, x):
    for c in range(ROW_SUB):
        dst(c, _pack_pair(x[:, c * LANES:(c + 1) * LANES], x[:, HALF_D + c * LANES:HALF_D + (c + 1) * LANES]))


def _dispatch_kernel(pos_ref, *refs, bounds, nw, wsteps):
    nsrc = len(bounds) - 1
    x1_refs, g2_ref = refs[:nsrc], refs[nsrc]
    w_in_refs = refs[nsrc + 1:nsrc + 1 + nw]
    xs_ref = refs[nsrc + 1 + nw]
    w_out_refs = refs[nsrc + 2 + nw:nsrc + 2 + 2 * nw]
    hp_ref, sem = refs[nsrc + 2 + 2 * nw:]
    i = pl.program_id(0)
    last = pl.num_programs(0) - 1
    slot = i % 2
    tm = x1_refs[0].shape[0]

    for m in range(nw):
        @pl.when((i >= m * wsteps) & (i < (m + 1) * wsteps))
        def _(src=w_in_refs[m], dst=w_out_refs[m]):
            dst[...] = src[...].astype(BF16)

    def put(c, words):
        hp_ref[slot, pl.ds(c, tm, stride=ROW_SUB), :] = words

    for s in range(nsrc):
        @pl.when((i >= bounds[s]) & (i < bounds[s + 1]))
        def _(x1_ref=x1_refs[s]):
            x1 = x1_ref[...]
            ms = jnp.mean(x1 * x1, axis=-1, keepdims=True)
            _store_packed_rows(put, x1 * lax.rsqrt(ms + EPS) * g2_ref[...])

    def row_copy(sl, r, p):
        src = hp_ref.at[sl, pl.ds(pl.multiple_of(r * ROW_SUB, ROW_SUB), ROW_SUB)]
        dst = xs_ref.at[pl.ds(pl.multiple_of(p * ROW_SUB, ROW_SUB), ROW_SUB)]
        return pltpu.make_async_copy(src, dst, sem.at[sl])

    def start(r, c):
        for k in range(TOP_K):
            row_copy(slot, r, pos_ref[0, TOP_K * r + k]).start(priority=k)
        return c

    def wait_all(sl):
        def wait(r, c):
            for k in range(TOP_K):
                row_copy(sl, 0, 0).wait()
            return c
        lax.fori_loop(0, tm, wait, 0, unroll=8)

    lax.fori_loop(0, tm, start, 0, unroll=4)

    @pl.when(i > 0)
    def _():
        wait_all(1 - slot)

    @pl.when(i == last)
    def _():
        wait_all(slot)


def _dispatch(pos3, x1s, g2, weights, tm):
    bounds = [0]
    for x1 in x1s:
        bounds.append(bounds[-1] + x1.shape[0] // tm)
    steps = bounds[-1]
    n = steps * tm
    nw = len(weights)
    wsteps = steps // nw
    w2d = [w.reshape(-1, w.shape[-1]) for w in weights]
    assert steps % nw == 0 and all(w.shape[0] % (wsteps * 16) == 0 for w in w2d)
    src = lambda s: pl.BlockSpec(
        (tm, D_MODEL), lambda i: (jnp.clip(i - bounds[s], 0, bounds[s + 1] - bounds[s] - 1), 0))
    wblk = lambda m: pl.BlockSpec(
        (w2d[m].shape[0] // wsteps, w2d[m].shape[1]), lambda i: (jnp.clip(i - m * wsteps, 0, wsteps - 1), 0))
    outs = pl.pallas_call(
        functools.partial(_dispatch_kernel, bounds=tuple(bounds), nw=nw, wsteps=wsteps),
        grid=(steps,),
        in_specs=[pl.BlockSpec((None, 1, TOP_K * tm), lambda i: (i, 0, 0), memory_space=pltpu.SMEM)]
        + [src(s) for s in range(len(x1s))]
        + [pl.BlockSpec((1, D_MODEL), lambda i: (0, 0))]
        + [wblk(m) for m in range(nw)],
        out_specs=[pl.BlockSpec(memory_space=pl.ANY)] + [wblk(m) for m in range(nw)],
        out_shape=[jax.ShapeDtypeStruct((n * TOP_K * ROW_SUB, LANES), jnp.uint32)]
        + [jax.ShapeDtypeStruct(w.shape, BF16) for w in w2d],
        scratch_shapes=[pltpu.VMEM((2, tm * ROW_SUB, LANES), jnp.uint32), pltpu.SemaphoreType.DMA((2,))],
        compiler_params=_cparams(("arbitrary",)),
        name="dispatch",
    )(pos3, *x1s, g2, *w2d)
    return outs[0], [o.reshape(w.shape) for o, w in zip(outs[1:], weights)]


def _combine_kernel(pos_ref, posn_ref, gate_ref, x1_ref, ys_ref, y_ref, buf_ref, sem):
    i = pl.program_id(0)
    last = pl.num_programs(0) - 1
    slot = i % 2
    tm = x1_ref.shape[0]

    def row_copy(sl, k, r, p):
        src = ys_ref.at[pl.ds(pl.multiple_of(p * ROW_SUB, ROW_SUB), ROW_SUB)]
        dst = buf_ref.at[sl, k, pl.ds(pl.multiple_of(r * ROW_SUB, ROW_SUB), ROW_SUB)]
        return pltpu.make_async_copy(src, dst, sem.at[sl])

    def start_all(sl, p_ref):
        def start(r, c):
            for k in range(TOP_K):
                row_copy(sl, k, r, p_ref[0, TOP_K * r + k]).start(priority=k)
            return c
        lax.fori_loop(0, tm, start, 0, unroll=4)

    @pl.when(i == 0)
    def _():
        start_all(0, pos_ref)

    @pl.when(i < last)
    def _():
        start_all(1 - slot, posn_ref)

    def wait(r, c):
        for k in range(TOP_K):
            row_copy(slot, k, 0, 0).wait()
        return c

    lax.fori_loop(0, tm, wait, 0, unroll=8)
    gates = gate_ref[...]
    g0 = jnp.broadcast_to(gates[:, 0:1], (tm, LANES))
    g1 = jnp.broadcast_to(gates[:, 1:2], (tm, LANES))
    for c in range(ROW_SUB):
        lo0, hi0 = _unpack_pair(buf_ref[slot, 0, pl.ds(c, tm, stride=ROW_SUB), :])
        lo1, hi1 = _unpack_pair(buf_ref[slot, 1, pl.ds(c, tm, stride=ROW_SUB), :])
        cl = slice(c * LANES, (c + 1) * LANES)
        ch = slice(HALF_D + c * LANES, HALF_D + (c + 1) * LANES)
        y_ref[:, cl] = x1_ref[:, cl] + (g0 * lo0 + g1 * lo1)
        y_ref[:, ch] = x1_ref[:, ch] + (g0 * hi0 + g1 * hi1)


def _combine(pos3, gates, x1, ys, tm, row0):
    n = x1.shape[0]
    nt = n // tm
    b0 = row0 // tm
    pos_blk = lambda fn: pl.BlockSpec((None, 1, TOP_K * tm), fn, memory_space=pltpu.SMEM)
    return pl.pallas_call(
        _combine_kernel,
        grid=(nt,),
        in_specs=[pos_blk(lambda i: (i + b0, 0, 0)),
                  pos_blk(lambda i: (jnp.minimum(i + 1, nt - 1) + b0, 0, 0)),
                  pl.BlockSpec((tm, LANES), lambda i: (i + b0, 0)),
                  pl.BlockSpec((tm, D_MODEL), lambda i: (i, 0)),
                  pl.BlockSpec(memory_space=pl.ANY)],
        out_specs=pl.BlockSpec((tm, D_MODEL), lambda i: (i, 0)),
        out_shape=jax.ShapeDtypeStruct((n, D_MODEL), F32),
        scratch_shapes=[pltpu.VMEM((2, TOP_K, tm * ROW_SUB, LANES), jnp.uint32), pltpu.SemaphoreType.DMA((2,))],
        compiler_params=_cparams(("arbitrary",)),
        name="combine",
    )(pos3, pos3, gates, x1, ys)


def _experts_kernel(tile_s, exp_s, lo_s, hi_s, first_s, x_ref, wg_ref, wu_ref, wd_ref, y_ref, xb_ref):
    w = pl.program_id(0)
    lo, hi = lo_s[w], hi_s[w]
    bm = xb_ref.shape[0]

    def work_item(first_visit):
        for c in range(ROW_SUB):
            xl, xh = _unpack_pair(x_ref[pl.ds(c, bm, stride=ROW_SUB), :])
            xb_ref[:, c * LANES:(c + 1) * LANES] = xl.astype(BF16)
            xb_ref[:, HALF_D + c * LANES:HALF_D + (c + 1) * LANES] = xh.astype(BF16)
        x = xb_ref[...]
        a = jnp.dot(x, wg_ref[...], preferred_element_type=F32)
        u = jnp.dot(x, wu_ref[...], preferred_element_type=F32)
        hm = (a * jax.nn.sigmoid(a) * u).astype(BF16)
        y = jnp.dot(hm, wd_ref[...], preferred_element_type=F32)
        rows = lax.broadcasted_iota(jnp.int32, (bm, LANES), 0)
        mine = (rows >= lo) & (rows < hi)

        def put(c, words):
            rws = pl.ds(c, bm, stride=ROW_SUB)
            y_ref[rws, :] = jnp.where(mine, words, jnp.uint32(0) if first_visit else y_ref[rws, :])
        _store_packed_rows(put, y)

    @pl.when((hi > lo) & (first_s[w] == 1))
    def _():
        work_item(True)

    @pl.when((hi > lo) & (first_s[w] == 0))
    def _():
        work_item(False)


def _experts(meta, xs, wg, wu, wd, bm):
    p = xs.shape[0] // ROW_SUB
    nw = p // bm + N_EXPERTS
    grid_spec = pltpu.PrefetchScalarGridSpec(
        num_scalar_prefetch=5,
        grid=(nw,),
        in_specs=[
            pl.BlockSpec((bm * ROW_SUB, LANES), lambda w, t, e, lo, hi, f: (t[w], 0)),
            pl.BlockSpec((None, D_MODEL, D_EXPERT), lambda w, t, e, lo, hi, f: (e[w], 0, 0)),
            pl.BlockSpec((None, D_MODEL, D_EXPERT), lambda w, t, e, lo, hi, f: (e[w], 0, 0)),
            pl.BlockSpec((None, D_EXPERT, D_MODEL), lambda w, t, e, lo, hi, f: (e[w], 0, 0)),
        ],
        out_specs=pl.BlockSpec((bm * ROW_SUB, LANES), lambda w, t, e, lo, hi, f: (t[w], 0)),
        scratch_shapes=[pltpu.VMEM((bm, D_MODEL), BF16)],
    )
    return pl.pallas_call(
        _experts_kernel,
        grid_spec=grid_spec,
        out_shape=jax.ShapeDtypeStruct((p * ROW_SUB, LANES), jnp.uint32),
        compiler_params=_cparams(("arbitrary",)),
        name="experts",
    )(*meta, xs, wg, wu, wd)


def _expert_work_items(counts, p, bm):
    ntiles = p // bm
    nw = ntiles + N_EXPERTS
    ends = jnp.cumsum(counts)
    starts = ends - counts
    ft = starts // bm
    nt = jnp.where(counts > 0, (ends - 1) // bm - ft + 1, 0)
    wend = jnp.cumsum(nt)
    wstart = wend - nt
    w = jnp.arange(nw, dtype=jnp.int32)
    used = w < wend[-1]
    e = jnp.minimum(jnp.sum((w[:, None] >= wend[None, :]).astype(jnp.int32), axis=1), N_EXPERTS - 1)
    e_last = jnp.max(jnp.where(counts > 0, jnp.arange(N_EXPERTS, dtype=jnp.int32), 0))
    e = jnp.where(used, e, e_last)
    tile = jnp.where(used, ft[e] + (w - wstart[e]), ntiles - 1).astype(jnp.int32)
    lo = jnp.where(used, jnp.maximum(starts[e], tile * bm) - tile * bm, 0).astype(jnp.int32)
    hi = jnp.where(used, jnp.minimum(ends[e], (tile + 1) * bm) - tile * bm, 0).astype(jnp.int32)
    first = jnp.concatenate([jnp.ones((1,), jnp.int32), (tile[1:] != tile[:-1]).astype(jnp.int32)])
    return (tile, e, lo, hi, first), starts


def _rope_tables(T):
    half = ROT_DIM // 2
    inv = 1.0 / (ROPE_THETA ** (np.arange(half, dtype=np.float64) * (2.0 / ROT_DIM)))
    ang = np.arange(T, dtype=np.float64)[:, None] * inv[None, :]
    cos, sin = np.cos(ang), np.sin(ang)
    rest = HEAD_DIM - ROT_DIM
    z_half, z_rest = np.zeros((T, half)), np.zeros((T, rest))
    cos_t = np.concatenate([cos, cos, np.ones((T, rest))], axis=1)
    sin_t = np.concatenate([-sin, sin, z_rest], axis=1)
    cos_t = np.stack([np.ones_like(cos_t), cos_t])
    sin_t = np.stack([np.zeros_like(sin_t), sin_t])
    return tuple(jnp.asarray(t, dtype=F32) for t in (cos_t, sin_t))


TM_IN = 1024
TM_OUT = 256
TM_ROUTE = 512
TM_DISPATCH = 256
TM_COMBINE = 512
BM_EXPERT = 256


def _attention_half(x, p):
    B, T, D = x.shape
    n = B * T
    assert D == D_MODEL and T % NA_TQ == 0 and T // GRID_W >= NA_ROWS + NA_HALO_ROWS
    x2 = x.reshape(n, D)
    main, qkv1, qkv2, sig = _inproj(x2, p["g1"], p["w_in"], p["gains"], *_rope_tables(T), B, T, TM_IN)
    main3 = main.reshape(B, T, JB_G1 * COL_BLK)
    att_a = _natten(main3, p["na_bias"], B, T).reshape(n, W_NA)
    dil = [_dilattn(main3[:, None], JB_G0, B, T, 0), _dilattn(qkv1, 0, B, T, 1), _dilattn(qkv2, 0, B, T, 2)]
    return _outproj(att_a, [d[0] for d in dil], [d[1] for d in dil], sig, x2,
                    p["wa"], p["wb"], p["wo"], p["g2"], p["wr"], TM_OUT)


def _layer(xs_in, p):
    halves = [_attention_half(x, p) for x in xs_in]
    x1s = [h[0] for h in halves]
    sizes = [x1.shape[0] for x1 in x1s]
    n = sum(sizes)
    logits = jnp.concatenate([h[1] for h in halves], axis=0)

    gates, eid, rank, cnt = _route(logits, p["rbias"], TM_ROUTE)
    counts = cnt[0, ROUTE_E0:ROUTE_E0 + N_EXPERTS].astype(jnp.int32)
    meta, starts = _expert_work_items(counts, n * TOP_K, BM_EXPERT)
    is_e = eid[:, :TOP_K, None] == jnp.arange(N_EXPERTS, dtype=jnp.int32)
    pos = jnp.sum(jnp.where(is_e, starts, 0), axis=-1) + rank[:, :TOP_K]
    pos = pos.astype(jnp.int32)
    pos_d = pos.reshape(n // TM_DISPATCH, 1, TOP_K * TM_DISPATCH)
    pos_c = pos.reshape(n // TM_COMBINE, 1, TOP_K * TM_COMBINE)

    xs, (wg, wu, wd) = _dispatch(pos_d, x1s, p["g2"], (p["wg"], p["wu"], p["wd"]), TM_DISPATCH)
    ys = _experts(meta, xs, wg, wu, wd, BM_EXPERT)
    outs, row0 = [], 0
    for x, x1, m in zip(xs_in, x1s, sizes):
        outs.append(_combine(pos_c, gates, x1, ys, TM_COMBINE, row0).reshape(x.shape))
        row0 += m
    return tuple(outs)


def _prepare_params(norm1_g, w_in, qn_a, kn_a, rpb_a, qn_b, kn_b, w_branch_a, w_branch_b, w_out,
                    norm2_g, router_group_w, router_group_b, router_expert_w, router_expert_b, w_gate, w_up, w_down):
    assert norm1_g.shape[0] == 1
    l = 0
    pad_r = LANES - N_GROUPS - N_EXPERTS
    zrow = jnp.concatenate([jnp.ones((1, HEAD_DIM), F32), jnp.zeros((3, HEAD_DIM), F32)], axis=0)
    col = lambda start, width: w_in[l][:, start:start + width]
    qb0, kb0, vb0 = 3 * W_NA, 3 * W_NA + W_DIL, 3 * W_NA + 2 * W_DIL
    pieces = [col(0, 3 * W_NA)]
    for g in range(len(DIL_CONFIGS)):
        pieces += [col(qb0 + g * W_GRP, W_GRP), col(kb0 + g * W_GRP, W_GRP), col(vb0 + g * W_GRP, W_GRP)]
    pieces.append(col(3 * W_NA + 3 * W_DIL, 2 * D_MODEL))
    return {
        "g1": norm1_g[l][None, :],
        "w_in": jnp.concatenate(pieces, axis=1).astype(BF16),
        "gains": jnp.concatenate([qn_a[l][None], kn_a[l][None], qn_b[l][None], kn_b[l][None], zrow], axis=0),
        "na_bias": _na_bias_table(rpb_a[l]),
        "wa": w_branch_a[l].astype(BF16),
        "wb": w_branch_b[l].astype(BF16),
        "wo": w_out[l].astype(BF16),
        "g2": norm2_g[l][None, :],
        "wr": jnp.concatenate([router_group_w[l], router_expert_w[l], jnp.zeros((D_MODEL, pad_r), F32)],
                              axis=1).astype(BF16),
        "rbias": jnp.concatenate([router_group_b[l], router_expert_b[l], jnp.zeros((pad_r,), F32)])[None, :],
        "wg": w_gate[l],
        "wu": w_up[l],
        "wd": w_down[l],
    }


def kernel(x_prompt, x_sample, norm1_g, w_in, qn_a, kn_a, rpb_a, qn_b, kn_b, w_branch_a, w_branch_b, w_out,
           norm2_g, router_group_w, router_group_b, router_expert_w, router_expert_b, w_gate, w_up, w_down):
    p = _prepare_params(norm1_g, w_in, qn_a, kn_a, rpb_a, qn_b, kn_b, w_branch_a, w_branch_b, w_out, norm2_g,
                        router_group_w, router_group_b, router_expert_w, router_expert_b, w_gate, w_up, w_down)
    return _layer((x_prompt, x_sample), p)
```

```python
import functools

import jax
import jax.numpy as jnp
import numpy as np
from jax import lax
from jax.experimental import pallas as pl
from jax.experimental.pallas import tpu as pltpu

F32 = jnp.float32
BF16 = jnp.bfloat16

D_MODEL = 2048
HEAD_DIM = 128
N_HEADS_NA = 4
N_HEADS_PER_DIL = 4
DIL_CONFIGS = ((128, 1), (512, 4), (2048, 16))
W_NA = N_HEADS_NA * HEAD_DIM
W_DIL = N_HEADS_PER_DIL * len(DIL_CONFIGS) * HEAD_DIM
W_GRP = N_HEADS_PER_DIL * HEAD_DIM
GRID_W = 64
NA_ROWS = 8
NA_COLS = 16
ROT_DIM = HEAD_DIM // 4
ROPE_THETA = 500000.0
BAND_BLOCK = 128
N_GROUPS = 4
EXPERTS_PER_GROUP = 8
N_EXPERTS = N_GROUPS * EXPERTS_PER_GROUP
D_EXPERT = 1024
EPS = 1e-6
NEG = -1e30
IN_WIDTH = 3 * W_NA + 3 * W_DIL + 2 * D_MODEL
SCALE = HEAD_DIM ** -0.5

COL_BLK = 512
JB_G0, JB_G1, JB_G2, JB_GATE = 3, 6, 9, 12
N_COL_BLK = IN_WIDTH // COL_BLK

LANES = 128
VMEM_LIMIT = 56 * 1024 * 1024


def _cparams(sem, vmem=VMEM_LIMIT):
    return pltpu.CompilerParams(dimension_semantics=sem, vmem_limit_bytes=vmem)


def _inproj_kernel(x_hbm, g1_ref, w_ref, gains_ref, cos_ref, sin_ref,
                   main_ref, d1_ref, d2_ref, gate_ref, h_ref, acc_ref, y_ref, xbuf_ref, xsem):
    i = pl.program_id(0)
    j = pl.program_id(1)
    jb = j - 1
    nh = COL_BLK // HEAD_DIM
    tm = h_ref.shape[0]
    slot = i % 2
    kind = jb % 3
    is_v = kind == 2
    gain_row = jnp.where(is_v, 4, kind + jnp.where(jb < JB_G0, 0, 2))

    def x_copy(tile, sl):
        rows = pl.ds(pl.multiple_of(tile * tm, tm), tm)
        return pltpu.make_async_copy(x_hbm.at[rows], xbuf_ref.at[sl], xsem.at[sl])

    def matmul():
        acc_ref[...] = jnp.dot(h_ref[...], w_ref[...], preferred_element_type=F32)

    @pl.when((i == 0) & (j == 0))
    def _():
        x_copy(0, 0).start()

    @pl.when((j == 1) & (i + 1 < pl.num_programs(0)))
    def _():
        x_copy(i + 1, 1 - slot).start()

    @pl.when(j == 0)
    def _():
        x_copy(i, slot).wait()
        x = xbuf_ref[slot]
        ms = jnp.mean(x * x, axis=-1, keepdims=True)
        h_ref[...] = (x * lax.rsqrt(ms + EPS) * g1_ref[...]).astype(BF16)
        matmul()

    def finished_head(hh, gain, cos, sin, low):
        xh = acc_ref[:, hh * HEAD_DIM:(hh + 1) * HEAD_DIM]
        ms = jnp.mean(xh * xh, axis=-1, keepdims=True)
        y = xh * jnp.where(is_v, 1.0, lax.rsqrt(ms + EPS)) * gain
        partner = jnp.where(low, pltpu.roll(y, HEAD_DIM - ROT_DIM // 2, 1), pltpu.roll(y, ROT_DIM // 2, 1))
        return y * cos + partner * sin

    def attention_block(store_head, after=None):
        gain = gains_ref[pl.ds(gain_row, 1), :]
        cos, sin = jnp.where(is_v, 1.0, cos_ref[...]), jnp.where(is_v, 0.0, sin_ref[...])
        low = lax.broadcasted_iota(jnp.int32, cos.shape, 1) < ROT_DIM // 2
        for hh in range(nh):
            store_head(hh, finished_head(hh, gain, cos, sin, low))
        if after is not None:
            after()
        matmul()

    def to_main(hh, y):
        main_ref[:, hh * HEAD_DIM:(hh + 1) * HEAD_DIM] = y.astype(BF16)

    def to_scratch(hh, y):
        y_ref[hh] = y

    def deinterleave(dst_ref, dil):
        rows = y_ref.shape[1] // dil
        for ph in range(dil):
            for hh in range(nh):
                dst_ref[ph, :, hh * HEAD_DIM:(hh + 1) * HEAD_DIM] = (
                    y_ref[hh, pl.ds(ph, rows, stride=dil), :].astype(BF16))

    @pl.when((jb >= 0) & (jb < JB_G1))
    def _():
        attention_block(to_main)

    @pl.when((jb >= JB_G1) & (jb < JB_G2))
    def _():
        attention_block(to_scratch, lambda: deinterleave(d1_ref, DIL_CONFIGS[1][1]))

    @pl.when((jb >= JB_G2) & (jb < JB_GATE))
    def _():
        attention_block(to_scratch, lambda: deinterleave(d2_ref, DIL_CONFIGS[2][1]))

    def gates():
        gate_ref[...] = (0.5 * jnp.tanh(0.5 * acc_ref[...]) + 0.5).astype(BF16)

    @pl.when((jb >= JB_GATE) & (j < N_COL_BLK))
    def _():
        gates()
        matmul()

    @pl.when(j == N_COL_BLK)
    def _():
        gates()


def _inproj(x2, g1, w_in_bf, gains, cos_t, sin_t, B, T, tm):
    n = x2.shape[0]
    tpos = T // tm
    d1, d2 = DIL_CONFIGS[1][1], DIL_CONFIGS[2][1]
    rot = lambda j: jnp.where(j - 1 >= JB_G0, 1, 0)
    tab = pl.BlockSpec((None, tm, HEAD_DIM), lambda i, j: (rot(j), i % tpos, 0))
    jb = lambda j: jnp.maximum(j - 1, 0)
    return pl.pallas_call(
        _inproj_kernel,
        grid=(n // tm, N_COL_BLK + 1),
        in_specs=[
            pl.BlockSpec(memory_space=pl.ANY),
            pl.BlockSpec((1, D_MODEL), lambda i, j: (0, 0)),
            pl.BlockSpec((D_MODEL, COL_BLK), lambda i, j: (0, jnp.minimum(j, N_COL_BLK - 1))),
            pl.BlockSpec((8, HEAD_DIM), lambda i, j: (0, 0)),
            tab, tab,
        ],
        out_specs=[
            pl.BlockSpec((tm, COL_BLK), lambda i, j: (i, jnp.minimum(jb(j), JB_G1 - 1))),
            pl.BlockSpec((None, d1, tm // d1, COL_BLK),
                         lambda i, j: (i // tpos, 0, i % tpos, jnp.clip(jb(j) - JB_G1, 0, 2))),
            pl.BlockSpec((None, d2, tm // d2, COL_BLK),
                         lambda i, j: (i // tpos, 0, i % tpos, jnp.clip(jb(j) - JB_G2, 0, 2))),
            pl.BlockSpec((tm, COL_BLK), lambda i, j: (i, jnp.clip(jb(j) - JB_GATE, 0, N_COL_BLK - JB_GATE - 1))),
        ],
        out_shape=[
            jax.ShapeDtypeStruct((n, JB_G1 * COL_BLK), BF16),
            jax.ShapeDtypeStruct((B, d1, T // d1, 3 * COL_BLK), BF16),
            jax.ShapeDtypeStruct((B, d2, T // d2, 3 * COL_BLK), BF16),
            jax.ShapeDtypeStruct((n, 2 * D_MODEL), BF16),
        ],
        scratch_shapes=[pltpu.VMEM((tm, D_MODEL), BF16), pltpu.VMEM((tm, COL_BLK), F32),
                        pltpu.VMEM((COL_BLK // HEAD_DIM, tm, HEAD_DIM), F32),
                        pltpu.VMEM((2, tm, D_MODEL), F32), pltpu.SemaphoreType.DMA((2,))],
        compiler_params=_cparams(("arbitrary", "arbitrary")),
        name="inproj",
    )(x2, g1, w_in_bf, gains, cos_t, sin_t)


NA_CHUNK_ROWS = 16
NA_HALO_ROWS = 4
NA_TQ = NA_CHUNK_ROWS * GRID_W
NA_TH = NA_HALO_ROWS * GRID_W
NA_GROUP = 4
NA_SPAN_ROWS = 12
NA_GQ = NA_GROUP * GRID_W
NA_SPAN = NA_SPAN_ROWS * GRID_W
NA_CASES = 3
assert NA_CHUNK_ROWS % NA_GROUP == 0 and NA_HALO_ROWS == NA_ROWS // 2
assert NA_ROWS + NA_GROUP - 1 <= NA_SPAN_ROWS <= NA_GROUP + 2 * NA_HALO_ROWS


def _natten_kernel(q_ref, kp_ref, km_ref, kn_ref, vp_ref, vm_ref, vn_ref, bias_ref, o_ref, kw_ref, vw_ref, *, rows):
    i = pl.program_id(1)
    kw_ref[0:NA_TH] = kp_ref[...]
    kw_ref[NA_TH:NA_TH + NA_TQ] = km_ref[...]
    kw_ref[NA_TH + NA_TQ:] = kn_ref[...]
    vw_ref[0:NA_TH] = vp_ref[...]
    vw_ref[NA_TH:NA_TH + NA_TQ] = vm_ref[...]
    vw_ref[NA_TH + NA_TQ:] = vn_ref[...]

    for gg in range(NA_CHUNK_ROWS // NA_GROUP):
        r0 = i * NA_CHUNK_ROWS + gg * NA_GROUP
        rs0 = jnp.clip(r0 - NA_ROWS // 2, 0, rows - NA_ROWS)
        case = jnp.where(r0 == 0, 0, jnp.where(r0 == rows - NA_GROUP, 2, 1))
        off = pl.multiple_of((rs0 - (i * NA_CHUNK_ROWS - NA_HALO_ROWS)) * GRID_W, GRID_W)
        for h in range(N_HEADS_NA):
            cols = slice(h * HEAD_DIM, (h + 1) * HEAD_DIM)
            q = q_ref[gg * NA_GQ:(gg + 1) * NA_GQ, cols]
            k = kw_ref[pl.ds(off, NA_SPAN), cols]
            v = vw_ref[pl.ds(off, NA_SPAN), cols]
            s = lax.dot_general(q, k, (((1,), (1,)), ((), ())), preferred_element_type=F32)
            s = s * SCALE + bias_ref[case, h]
            m = jnp.max(s, axis=-1, keepdims=True)
            p = jnp.exp(s - m)
            l = jnp.sum(p, axis=-1, keepdims=True)
            o = jnp.dot(p.astype(BF16), v, preferred_element_type=F32) / l
            o_ref[gg * NA_GQ:(gg + 1) * NA_GQ, cols] = o.astype(BF16)


def _natten(proj3, bias_tab, B, T):
    rows = T // GRID_W
    nh = T // NA_TH
    ratio = NA_TQ // NA_TH
    blk = lambda rws, fn: pl.BlockSpec((None, rws, COL_BLK), fn)
    return pl.pallas_call(
        functools.partial(_natten_kernel, rows=rows),
        grid=(B, T // NA_TQ),
        in_specs=[
            blk(NA_TQ, lambda b, i: (b, i, 0)),
            blk(NA_TH, lambda b, i: (b, jnp.maximum(i * ratio - 1, 0), 1)),
            blk(NA_TQ, lambda b, i: (b, i, 1)),
            blk(NA_TH, lambda b, i: (b, jnp.minimum((i + 1) * ratio, nh - 1), 1)),
            blk(NA_TH, lambda b, i: (b, jnp.maximum(i * ratio - 1, 0), 2)),
            blk(NA_TQ, lambda b, i: (b, i, 2)),
            blk(NA_TH, lambda b, i: (b, jnp.minimum((i + 1) * ratio, nh - 1), 2)),
            pl.BlockSpec((NA_CASES, N_HEADS_NA, NA_GQ, NA_SPAN), lambda b, i: (0, 0, 0, 0)),
        ],
        out_specs=pl.BlockSpec((None, NA_TQ, W_NA), lambda b, i: (b, i, 0)),
        out_shape=jax.ShapeDtypeStruct((B, T, W_NA), BF16),
        scratch_shapes=[pltpu.VMEM((NA_TQ + 2 * NA_TH, COL_BLK), BF16)] * 2,
        compiler_params=_cparams(("arbitrary", "arbitrary")),
        name="natten",
    )(proj3, proj3, proj3, proj3, proj3, proj3, proj3, bias_tab)


def _na_bias_table(rpb):
    c = np.arange(GRID_W)
    cs = np.clip(c - NA_COLS // 2, 0, GRID_W - NA_COLS)
    cc = np.arange(GRID_W)
    valid = (cc[None, :] >= cs[:, None]) & (cc[None, :] < cs[:, None] + NA_COLS)
    dc_idx = np.clip(cc[None, :] - c[:, None] + NA_COLS - 1, 0, 2 * NA_COLS - 2)
    pick = (dc_idx[..., None] == np.arange(2 * NA_COLS - 1)).astype(np.float32)
    t = jnp.einsum("qkd,hrd->hrqk", pick, rpb.astype(F32), precision=lax.Precision.HIGHEST)
    t = jnp.where(valid[None, None], t, NEG)
    half = NA_ROWS // 2
    cases = ([(m, 0) for m in range(NA_GROUP)],
             [(half, m) for m in range(NA_GROUP)],
             [(half + m, 0) for m in range(NA_GROUP)])
    tiles = []
    for case in cases:
        per_row = []
        for var, woff in case:
            win = t[:, NA_ROWS - 1 - var:2 * NA_ROWS - 1 - var]
            pad = ((0, 0), (woff, NA_SPAN_ROWS - NA_ROWS - woff), (0, 0), (0, 0))
            per_row.append(jnp.pad(win, pad, constant_values=NEG))
        tile = jnp.stack(per_row, axis=1)
        tiles.append(tile.transpose(0, 1, 3, 2, 4).reshape(N_HEADS_NA, NA_GQ, NA_SPAN))
    return jnp.stack(tiles, axis=0)


DIL_HALF = 64
DIL_KB = BAND_BLOCK + 2 * DIL_HALF
DIL_TOKENS_PER_STEP = (1024, 1024, 2048)


def _dilattn_kernel(q_ref, kp_ref, km_ref, kn_ref, vp_ref, vm_ref, vn_ref, o_ref, lse_ref,
                    kw_ref, vw_ref, os_ref, ls_ref, *, L, tq, dil):
    i = pl.program_id(1)
    kw_ref[:, 0:DIL_HALF] = kp_ref[...]
    kw_ref[:, DIL_HALF:DIL_HALF + tq] = km_ref[...]
    kw_ref[:, DIL_HALF + tq:] = kn_ref[...]
    vw_ref[:, 0:DIL_HALF] = vp_ref[...]
    vw_ref[:, DIL_HALF:DIL_HALF + tq] = vm_ref[...]
    vw_ref[:, DIL_HALF + tq:] = vn_ref[...]

    def phase(j):
        r_io = lax.broadcasted_iota(jnp.int32, (BAND_BLOCK, DIL_KB), 0)
        c_io = lax.broadcasted_iota(jnp.int32, (BAND_BLOCK, DIL_KB), 1)
        lane = lax.broadcasted_iota(jnp.int32, (BAND_BLOCK, LANES), 1)
        for s in range(tq // BAND_BLOCK):
            base = i * tq + s * BAND_BLOCK - DIL_HALF
            lo = jnp.maximum(r_io, -base)
            hi = jnp.minimum(r_io + 2 * DIL_HALF, L - 1 - base)
            valid = (c_io >= lo) & (c_io <= hi)
            out_rows = pl.ds(s * BAND_BLOCK * dil + j, BAND_BLOCK, stride=dil)
            lse_tile = jnp.zeros((BAND_BLOCK, LANES), F32)
            for h in range(N_HEADS_PER_DIL):
                cols = slice(h * HEAD_DIM, (h + 1) * HEAD_DIM)
                q = q_ref[j, s * BAND_BLOCK:(s + 1) * BAND_BLOCK, cols]
                k = kw_ref[j, s * BAND_BLOCK:s * BAND_BLOCK + DIL_KB, cols]
                v = vw_ref[j, s * BAND_BLOCK:s * BAND_BLOCK + DIL_KB, cols]
                sc = lax.dot_general(q, k, (((1,), (1,)), ((), ())), preferred_element_type=F32) * SCALE
                sc = jnp.where(valid, sc, NEG)
                m = jnp.max(sc, axis=-1, keepdims=True)
                p = jnp.exp(sc - m)
                l = jnp.sum(p, axis=-1, keepdims=True)
                os_ref[h, out_rows, :] = jnp.dot(p.astype(BF16), v, preferred_element_type=F32) / l
                lse_tile = jnp.where(lane == h, m + jnp.log(l), lse_tile)
            ls_ref[out_rows, :] = lse_tile

    if dil == 1:
        phase(0)
    else:
        def body(j, c):
            phase(j)
            return c
        lax.fori_loop(0, dil, body, 0, unroll=4)
    for h in range(N_HEADS_PER_DIL):
        o_ref[:, h * HEAD_DIM:(h + 1) * HEAD_DIM] = os_ref[h].astype(BF16)
    lse_ref[...] = ls_ref[...]


def _dilattn(qkv, col0, B, T, g):
    dil = DIL_CONFIGS[g][1]
    assert DIL_CONFIGS[g][0] // 2 // dil == DIL_HALF
    L = T // dil
    tq = min(DIL_TOKENS_PER_STEP[g] // dil, L)
    nh = L // DIL_HALF
    ratio = tq // DIL_HALF
    blk = lambda rws, fn: pl.BlockSpec((None, dil, rws, COL_BLK), fn)
    prev = lambda i: jnp.maximum(i * ratio - 1, 0)
    nxt = lambda i: jnp.minimum((i + 1) * ratio, nh - 1)
    o, lse = pl.pallas_call(
        functools.partial(_dilattn_kernel, L=L, tq=tq, dil=dil),
        grid=(B, L // tq),
        in_specs=[
            blk(tq, lambda b, i: (b, 0, i, col0)),
            blk(DIL_HALF, lambda b, i: (b, 0, prev(i), col0 + 1)),
            blk(tq, lambda b, i: (b, 0, i, col0 + 1)),
            blk(DIL_HALF, lambda b, i: (b, 0, nxt(i), col0 + 1)),
            blk(DIL_HALF, lambda b, i: (b, 0, prev(i), col0 + 2)),
            blk(tq, lambda b, i: (b, 0, i, col0 + 2)),
            blk(DIL_HALF, lambda b, i: (b, 0, nxt(i), col0 + 2)),
        ],
        out_specs=[
            pl.BlockSpec((None, tq * dil, W_GRP), lambda b, i: (b, i, 0)),
            pl.BlockSpec((None, tq * dil, LANES), lambda b, i: (b, i, 0)),
        ],
        out_shape=[
            jax.ShapeDtypeStruct((B, T, W_GRP), BF16),
            jax.ShapeDtypeStruct((B, T, LANES), F32),
        ],
        scratch_shapes=[pltpu.VMEM((dil, tq + 2 * DIL_HALF, COL_BLK), BF16)] * 2
        + [pltpu.VMEM((N_HEADS_PER_DIL, tq * dil, HEAD_DIM), F32), pltpu.VMEM((tq * dil, LANES), F32)],
        compiler_params=_cparams(("arbitrary", "arbitrary")),
        name=f"dilattn{g}",
    )(qkv, qkv, qkv, qkv, qkv, qkv, qkv)
    return o.reshape(B * T, W_GRP), lse.reshape(B * T, LANES)


def _outproj_kernel(a_ref, o0_ref, o1_ref, o2_ref, l0_ref, l1_ref, l2_ref, ga_ref, gb_ref, x_ref,
                    wa_ref, wb_ref, wo_ref, g2_ref, wr_ref, x1_ref, lg_ref):
    o_refs = (o0_ref, o1_ref, o2_ref)
    lses = [r[...] for r in (l0_ref, l1_ref, l2_ref)]
    parts = []
    for h in range(N_HEADS_PER_DIL):
        cols = slice(h * HEAD_DIM, (h + 1) * HEAD_DIM)
        lh = [l[:, h:h + 1] for l in lses]
        mx = jnp.maximum(jnp.maximum(lh[0], lh[1]), lh[2])
        e = [jnp.exp(v - mx) for v in lh]
        den = e[0] + e[1] + e[2]
        acc = None
        for g in range(len(DIL_CONFIGS)):
            t = (e[g] / den) * o_refs[g][:, cols].astype(F32)
            acc = t if acc is None else acc + t
        parts.append(acc.astype(BF16))
    att_b = jnp.concatenate(parts, axis=-1)
    ya = jnp.dot(a_ref[...], wa_ref[...], preferred_element_type=F32)
    yb = jnp.dot(att_b, wb_ref[...], preferred_element_type=F32)
    mix = ga_ref[...].astype(F32) * ya + gb_ref[...].astype(F32) * yb
    x1 = x_ref[...] + jnp.dot(mix.astype(BF16), wo_ref[...], preferred_element_type=F32)
    x1_ref[...] = x1
    ms = jnp.mean(x1 * x1, axis=-1, keepdims=True)
    h2 = x1 * lax.rsqrt(ms + EPS) * g2_ref[...]
    lg_ref[...] = jnp.dot(h2.astype(BF16), wr_ref[...], preferred_element_type=F32)


def _outproj(att_a, os_, lses, gates, x2, wa, wb, wo, g2, wr, tm):
    n = x2.shape[0]
    row = lambda w: pl.BlockSpec((tm, w), lambda i: (i, 0))
    full = lambda a: pl.BlockSpec(a.shape, lambda i: (0,) * a.ndim)
    return pl.pallas_call(
        _outproj_kernel,
        grid=(n // tm,),
        in_specs=[row(W_NA), row(W_GRP), row(W_GRP), row(W_GRP), row(LANES), row(LANES), row(LANES),
                  pl.BlockSpec((tm, D_MODEL), lambda i: (i, 0)),
                  pl.BlockSpec((tm, D_MODEL), lambda i: (i, 1)),
                  row(D_MODEL), full(wa), full(wb), full(wo), full(g2), full(wr)],
        out_specs=[row(D_MODEL), row(LANES)],
        out_shape=[jax.ShapeDtypeStruct((n, D_MODEL), F32), jax.ShapeDtypeStruct((n, LANES), F32)],
        compiler_params=_cparams(("arbitrary",)),
        name="outproj",
    )(att_a, *os_, *lses, gates, gates, x2, wa, wb, wo, g2, wr)


ROUTE_E0 = N_GROUPS


def _route_kernel(lg_ref, b_ref, gate_ref, eid_ref, rank_ref, cnt_ref, carry_ref):
    i = pl.program_id(0)

    @pl.when(i == 0)
    def _():
        carry_ref[...] = jnp.zeros_like(carry_ref)

    z = lg_ref[...] + b_ref[...]
    tm = z.shape[0]
    lane = lax.broadcasted_iota(jnp.int32, z.shape, 1)
    lane_f = lane.astype(F32)
    rmax = lambda v: jnp.max(v, axis=-1, keepdims=True)
    rmin = lambda v: jnp.min(v, axis=-1, keepdims=True)
    rsum = lambda v: jnp.sum(v, axis=-1, keepdims=True)

    gmask = lane < N_GROUPS
    gmax = rmax(jnp.where(gmask, z, -jnp.inf))
    gsel = rmin(jnp.where(gmask & (z == gmax), lane_f, float(LANES)))
    gp = 1.0 / rsum(jnp.where(gmask, jnp.exp(z - gmax), 0.0))

    elo = float(ROUTE_E0) + float(EXPERTS_PER_GROUP) * gsel
    emask = (lane_f >= elo) & (lane_f < elo + float(EXPERTS_PER_GROUP))
    emax = rmax(jnp.where(emask, z, -jnp.inf))
    ee = jnp.where(emask, jnp.exp(z - emax), 0.0)
    eprob = ee / rsum(ee)
    p1 = rmax(jnp.where(emask, eprob, -1.0))
    i1 = rmin(jnp.where(emask & (eprob == p1), lane_f, float(LANES)))
    m2 = emask & (lane_f != i1)
    p2 = rmax(jnp.where(m2, eprob, -1.0))
    i2 = rmin(jnp.where(m2 & (eprob == p2), lane_f, float(LANES)))
    den = p1 + p2
    g1 = gp * p1 / den
    g2 = gp * p2 / den

    oh1 = (lane_f == i1).astype(F32)
    oh2 = (lane_f == i2).astype(F32)
    oh = oh1 + oh2
    tri = (lax.broadcasted_iota(jnp.int32, (tm, tm), 1) < lax.broadcasted_iota(jnp.int32, (tm, tm), 0)).astype(BF16)
    tot = carry_ref[...] + jnp.dot(tri, oh.astype(BF16), preferred_element_type=F32)
    r1 = rsum(oh1 * tot)
    r2 = rsum(oh2 * tot)
    carry = carry_ref[...] + jnp.sum(oh, axis=0, keepdims=True)
    carry_ref[...] = carry
    cnt_ref[...] = carry

    two = lambda a, b: jnp.where(lane == 0, a, jnp.where(lane == 1, b, 0.0))
    gate_ref[...] = two(g1, g2)
    eid_ref[...] = two(i1 - float(ROUTE_E0), i2 - float(ROUTE_E0)).astype(jnp.int32)
    rank_ref[...] = two(r1, r2).astype(jnp.int32)


def _route(logits, rbias, tm):
    n = logits.shape[0]
    row = pl.BlockSpec((tm, LANES), lambda i: (i, 0))
    one = pl.BlockSpec((1, LANES), lambda i: (0, 0))
    return pl.pallas_call(
        _route_kernel,
        grid=(n // tm,),
        in_specs=[row, one],
        out_specs=[row, row, row, one],
        out_shape=[jax.ShapeDtypeStruct((n, LANES), F32), jax.ShapeDtypeStruct((n, LANES), jnp.int32),
                   jax.ShapeDtypeStruct((n, LANES), jnp.int32), jax.ShapeDtypeStruct((1, LANES), F32)],
        scratch_shapes=[pltpu.VMEM((1, LANES), F32)],
        compiler_params=_cparams(("arbitrary",)),
        name="route",
    )(logits, rbias)


TOP_K = 2


ROW_SUB = 8
HALF_D = D_MODEL // 2
assert ROW_SUB * LANES == HALF_D


def _pack_pair(lo, hi):
    lo_b = lax.bitcast_convert_type(lo.astype(BF16).astype(F32), jnp.uint32)
    hi_b = lax.bitcast_convert_type(hi.astype(BF16).astype(F32), jnp.uint32)
    return (lo_b >> 16) | hi_b


def _unpack_pair(w):
    lo = lax.bitcast_convert_type(w << 16, F32)
    hi = lax.bitcast_convert_type(w & jnp.uint32(0xFFFF0000), F32)
    return lo, hi


def _store_packed_rows(dst, x):
    for c in range(ROW_SUB):
        dst(c, _pack_pair(x[:, c * LANES:(c + 1) * LANES], x[:, HALF_D + c * LANES:HALF_D + (c + 1) * LANES]))


def _dispatch_kernel(pos_ref, *refs, bounds, nw, wsteps):
    nsrc = len(bounds) - 1
    x1_refs, g2_ref = refs[:nsrc], refs[nsrc]
    w_in_refs = refs[nsrc + 1:nsrc + 1 + nw]
    xs_ref = refs[nsrc + 1 + nw]
    w_out_refs = refs[nsrc + 2 + nw:nsrc + 2 + 2 * nw]
    hp_ref, sem = refs[nsrc + 2 + 2 * nw:]
    i = pl.program_id(0)
    last = pl.num_programs(0) - 1
    slot = i % 2
    tm = x1_refs[0].shape[0]

    for m in range(nw):
        @pl.when((i >= m * wsteps) & (i < (m + 1) * wsteps))
        def _(src=w_in_refs[m], dst=w_out_refs[m]):
            dst[...] = src[...].astype(BF16)

    def put(c, words):
        hp_ref[slot, pl.ds(c, tm, stride=ROW_SUB), :] = words

    for s in range(nsrc):
        @pl.when((i >= bounds[s]) & (i < bounds[s + 1]))
        def _(x1_ref=x1_refs[s]):
            x1 = x1_ref[...]
            ms = jnp.mean(x1 * x1, axis=-1, keepdims=True)
            _store_packed_rows(put, x1 * lax.rsqrt(ms + EPS) * g2_ref[...])

    def row_copy(sl, r, p):
        src = hp_ref.at[sl, pl.ds(pl.multiple_of(r * ROW_SUB, ROW_SUB), ROW_SUB)]
        dst = xs_ref.at[pl.ds(pl.multiple_of(p * ROW_SUB, ROW_SUB), ROW_SUB)]
        return pltpu.make_async_copy(src, dst, sem.at[sl])

    def start(r, c):
        for k in range(TOP_K):
            row_copy(slot, r, pos_ref[0, TOP_K * r + k]).start(priority=k)
        return c

    def wait_all(sl):
        def wait(r, c):
            for k in range(TOP_K):
                row_copy(sl, 0, 0).wait()
            return c
        lax.fori_loop(0, tm, wait, 0, unroll=8)

    lax.fori_loop(0, tm, start, 0, unroll=4)

    @pl.when(i > 0)
    def _():
        wait_all(1 - slot)

    @pl.when(i == last)
    def _():
        wait_all(slot)


def _dispatch(pos3, x1s, g2, weights, tm):
    bounds = [0]
    for x1 in x1s:
        bounds.append(bounds[-1] + x1.shape[0] // tm)
    steps = bounds[-1]
    n = steps * tm
    nw = len(weights)
    wsteps = steps // nw
    w2d = [w.reshape(-1, w.shape[-1]) for w in weights]
    assert steps % nw == 0 and all(w.shape[0] % (wsteps * 16) == 0 for w in w2d)
    src = lambda s: pl.BlockSpec(
        (tm, D_MODEL), lambda i: (jnp.clip(i - bounds[s], 0, bounds[s + 1] - bounds[s] - 1), 0))
    wblk = lambda m: pl.BlockSpec(
        (w2d[m].shape[0] // wsteps, w2d[m].shape[1]), lambda i: (jnp.clip(i - m * wsteps, 0, wsteps - 1), 0))
    outs = pl.pallas_call(
        functools.partial(_dispatch_kernel, bounds=tuple(bounds), nw=nw, wsteps=wsteps),
        grid=(steps,),
        in_specs=[pl.BlockSpec((None, 1, TOP_K * tm), lambda i: (i, 0, 0), memory_space=pltpu.SMEM)]
        + [src(s) for s in range(len(x1s))]
        + [pl.BlockSpec((1, D_MODEL), lambda i: (0, 0))]
        + [wblk(m) for m in range(nw)],
        out_specs=[pl.BlockSpec(memory_space=pl.ANY)] + [wblk(m) for m in range(nw)],
        out_shape=[jax.ShapeDtypeStruct((n * TOP_K * ROW_SUB, LANES), jnp.uint32)]
        + [jax.ShapeDtypeStruct(w.shape, BF16) for w in w2d],
        scratch_shapes=[pltpu.VMEM((2, tm * ROW_SUB, LANES), jnp.uint32), pltpu.SemaphoreType.DMA((2,))],
        compiler_params=_cparams(("arbitrary",)),
        name="dispatch",
    )(pos3, *x1s, g2, *w2d)
    return outs[0], [o.reshape(w.shape) for o, w in zip(outs[1:], weights)]


def _combine_kernel(pos_ref, posn_ref, gate_ref, x1_ref, ys_ref, y_ref, buf_ref, sem):
    i = pl.program_id(0)
    last = pl.num_programs(0) - 1
    slot = i % 2
    tm = x1_ref.shape[0]

    def row_copy(sl, k, r, p):
        src = ys_ref.at[pl.ds(pl.multiple_of(p * ROW_SUB, ROW_SUB), ROW_SUB)]
        dst = buf_ref.at[sl, k, pl.ds(pl.multiple_of(r * ROW_SUB, ROW_SUB), ROW_SUB)]
        return pltpu.make_async_copy(src, dst, sem.at[sl])

    def start_all(sl, p_ref):
        def start(r, c):
            for k in range(TOP_K):
                row_copy(sl, k, r, p_ref[0, TOP_K * r + k]).start(priority=k)
            return c
        lax.fori_loop(0, tm, start, 0, unroll=4)

    @pl.when(i == 0)
    def _():
        start_all(0, pos_ref)

    @pl.when(i < last)
    def _():
        start_all(1 - slot, posn_ref)

    def wait(r, c):
        for k in range(TOP_K):
            row_copy(slot, k, 0, 0).wait()
        return c

    lax.fori_loop(0, tm, wait, 0, unroll=8)
    gates = gate_ref[...]
    g0 = jnp.broadcast_to(gates[:, 0:1], (tm, LANES))
    g1 = jnp.broadcast_to(gates[:, 1:2], (tm, LANES))
    for c in range(ROW_SUB):
        lo0, hi0 = _unpack_pair(buf_ref[slot, 0, pl.ds(c, tm, stride=ROW_SUB), :])
        lo1, hi1 = _unpack_pair(buf_ref[slot, 1, pl.ds(c, tm, stride=ROW_SUB), :])
        cl = slice(c * LANES, (c + 1) * LANES)
        ch = slice(HALF_D + c * LANES, HALF_D + (c + 1) * LANES)
        y_ref[:, cl] = x1_ref[:, cl] + (g0 * lo0 + g1 * lo1)
        y_ref[:, ch] = x1_ref[:, ch] + (g0 * hi0 + g1 * hi1)


def _combine(pos3, gates, x1, ys, tm, row0):
    n = x1.shape[0]
    nt = n // tm
    b0 = row0 // tm
    pos_blk = lambda fn: pl.BlockSpec((None, 1, TOP_K * tm), fn, memory_space=pltpu.SMEM)
    return pl.pallas_call(
        _combine_kernel,
        grid=(nt,),
        in_specs=[pos_blk(lambda i: (i + b0, 0, 0)),
                  pos_blk(lambda i: (jnp.minimum(i + 1, nt - 1) + b0, 0, 0)),
                  pl.BlockSpec((tm, LANES), lambda i: (i + b0, 0)),
                  pl.BlockSpec((tm, D_MODEL), lambda i: (i, 0)),
                  pl.BlockSpec(memory_space=pl.ANY)],
        out_specs=pl.BlockSpec((tm, D_MODEL), lambda i: (i, 0)),
        out_shape=jax.ShapeDtypeStruct((n, D_MODEL), F32),
        scratch_shapes=[pltpu.VMEM((2, TOP_K, tm * ROW_SUB, LANES), jnp.uint32), pltpu.SemaphoreType.DMA((2,))],
        compiler_params=_cparams(("arbitrary",)),
        name="combine",
    )(pos3, pos3, gates, x1, ys)


def _experts_kernel(tile_s, exp_s, lo_s, hi_s, first_s, x_ref, wg_ref, wu_ref, wd_ref, y_ref, xb_ref):
    w = pl.program_id(0)
    lo, hi = lo_s[w], hi_s[w]
    bm = xb_ref.shape[0]

    def work_item(first_visit):
        for c in range(ROW_SUB):
            xl, xh = _unpack_pair(x_ref[pl.ds(c, bm, stride=ROW_SUB), :])
            xb_ref[:, c * LANES:(c + 1) * LANES] = xl.astype(BF16)
            xb_ref[:, HALF_D + c * LANES:HALF_D + (c + 1) * LANES] = xh.astype(BF16)
        x = xb_ref[...]
        a = jnp.dot(x, wg_ref[...], preferred_element_type=F32)
        u = jnp.dot(x, wu_ref[...], preferred_element_type=F32)
        hm = (a * jax.nn.sigmoid(a) * u).astype(BF16)
        y = jnp.dot(hm, wd_ref[...], preferred_element_type=F32)
        rows = lax.broadcasted_iota(jnp.int32, (bm, LANES), 0)
        mine = (rows >= lo) & (rows < hi)

        def put(c, words):
            rws = pl.ds(c, bm, stride=ROW_SUB)
            y_ref[rws, :] = jnp.where(mine, words, jnp.uint32(0) if first_visit else y_ref[rws, :])
        _store_packed_rows(put, y)

    @pl.when((hi > lo) & (first_s[w] == 1))
    def _():
        work_item(True)

    @pl.when((hi > lo) & (first_s[w] == 0))
    def _():
        work_item(False)


def _experts(meta, xs, wg, wu, wd, bm):
    p = xs.shape[0] // ROW_SUB
    nw = p // bm + N_EXPERTS
    grid_spec = pltpu.PrefetchScalarGridSpec(
        num_scalar_prefetch=5,
        grid=(nw,),
        in_specs=[
            pl.BlockSpec((bm * ROW_SUB, LANES), lambda w, t, e, lo, hi, f: (t[w], 0)),
            pl.BlockSpec((None, D_MODEL, D_EXPERT), lambda w, t, e, lo, hi, f: (e[w], 0, 0)),
            pl.BlockSpec((None, D_MODEL, D_EXPERT), lambda w, t, e, lo, hi, f: (e[w], 0, 0)),
            pl.BlockSpec((None, D_EXPERT, D_MODEL), lambda w, t, e, lo, hi, f: (e[w], 0, 0)),
        ],
        out_specs=pl.BlockSpec((bm * ROW_SUB, LANES), lambda w, t, e, lo, hi, f: (t[w], 0)),
        scratch_shapes=[pltpu.VMEM((bm, D_MODEL), BF16)],
    )
    return pl.pallas_call(
        _experts_kernel,
        grid_spec=grid_spec,
        out_shape=jax.ShapeDtypeStruct((p * ROW_SUB, LANES), jnp.uint32),
        compiler_params=_cparams(("arbitrary",)),
        name="experts",
    )(*meta, xs, wg, wu, wd)


def _expert_work_items(counts, p, bm):
    ntiles = p // bm
    nw = ntiles + N_EXPERTS
    ends = jnp.cumsum(counts)
    starts = ends - counts
    ft = starts // bm
    nt = jnp.where(counts > 0, (ends - 1) // bm - ft + 1, 0)
    wend = jnp.cumsum(nt)
    wstart = wend - nt
    w = jnp.arange(nw, dtype=jnp.int32)
    used = w < wend[-1]
    e = jnp.minimum(jnp.sum((w[:, None] >= wend[None, :]).astype(jnp.int32), axis=1), N_EXPERTS - 1)
    e_last = jnp.max(jnp.where(counts > 0, jnp.arange(N_EXPERTS, dtype=jnp.int32), 0))
    e = jnp.where(used, e, e_last)
    tile = jnp.where(used, ft[e] + (w - wstart[e]), ntiles - 1).astype(jnp.int32)
    lo = jnp.where(used, jnp.maximum(starts[e], tile * bm) - tile * bm, 0).astype(jnp.int32)
    hi = jnp.where(used, jnp.minimum(ends[e], (tile + 1) * bm) - tile * bm, 0).astype(jnp.int32)
    first = jnp.concatenate([jnp.ones((1,), jnp.int32), (tile[1:] != tile[:-1]).astype(jnp.int32)])
    return (tile, e, lo, hi, first), starts


def _rope_tables(T):
    half = ROT_DIM // 2
    inv = 1.0 / (ROPE_THETA ** (np.arange(half, dtype=np.float64) * (2.0 / ROT_DIM)))
    ang = np.arange(T, dtype=np.float64)[:, None] * inv[None, :]
    cos, sin = np.cos(ang), np.sin(ang)
    rest = HEAD_DIM - ROT_DIM
    z_half, z_rest = np.zeros((T, half)), np.zeros((T, rest))
    cos_t = np.concatenate([cos, cos, np.ones((T, rest))], axis=1)
    sin_t = np.concatenate([-sin, sin, z_rest], axis=1)
    cos_t = np.stack([np.ones_like(cos_t), cos_t])
    sin_t = np.stack([np.zeros_like(sin_t), sin_t])
    return tuple(jnp.asarray(t, dtype=F32) for t in (cos_t, sin_t))


TM_IN = 1024
TM_OUT = 256
TM_ROUTE = 512
TM_DISPATCH = 256
TM_COMBINE = 512
BM_EXPERT = 256


def _attention_half(x, p):
    B, T, D = x.shape
    n = B * T
    assert D == D_MODEL and T % NA_TQ == 0 and T // GRID_W >= NA_ROWS + NA_HALO_ROWS
    x2 = x.reshape(n, D)
    main, qkv1, qkv2, sig = _inproj(x2, p["g1"], p["w_in"], p["gains"], *_rope_tables(T), B, T, TM_IN)
    main3 = main.reshape(B, T, JB_G1 * COL_BLK)
    att_a = _natten(main3, p["na_bias"], B, T).reshape(n, W_NA)
    dil = [_dilattn(main3[:, None], JB_G0, B, T, 0), _dilattn(qkv1, 0, B, T, 1), _dilattn(qkv2, 0, B, T, 2)]
    return _outproj(att_a, [d[0] for d in dil], [d[1] for d in dil], sig, x2,
                    p["wa"], p["wb"], p["wo"], p["g2"], p["wr"], TM_OUT)


def _layer(xs_in, p):
    halves = [_attention_half(x, p) for x in xs_in]
    x1s = [h[0] for h in halves]
    sizes = [x1.shape[0] for x1 in x1s]
    n = sum(sizes)
    logits = jnp.concatenate([h[1] for h in halves], axis=0)

    gates, eid, rank, cnt = _route(logits, p["rbias"], TM_ROUTE)
    counts = cnt[0, ROUTE_E0:ROUTE_E0 + N_EXPERTS].astype(jnp.int32)
    meta, starts = _expert_work_items(counts, n * TOP_K, BM_EXPERT)
    is_e = eid[:, :TOP_K, None] == jnp.arange(N_EXPERTS, dtype=jnp.int32)
    pos = jnp.sum(jnp.where(is_e, starts, 0), axis=-1) + rank[:, :TOP_K]
    pos = pos.astype(jnp.int32)
    pos_d = pos.reshape(n // TM_DISPATCH, 1, TOP_K * TM_DISPATCH)
    pos_c = pos.reshape(n // TM_COMBINE, 1, TOP_K * TM_COMBINE)

    xs, (wg, wu, wd) = _dispatch(pos_d, x1s, p["g2"], (p["wg"], p["wu"], p["wd"]), TM_DISPATCH)
    ys = _experts(meta, xs, wg, wu, wd, BM_EXPERT)
    outs, row0 = [], 0
    for x, x1, m in zip(xs_in, x1s, sizes):
        outs.append(_combine(pos_c, gates, x1, ys, TM_COMBINE, row0).reshape(x.shape))
        row0 += m
    return tuple(outs)


def _prepare_params(norm1_g, w_in, qn_a, kn_a, rpb_a, qn_b, kn_b, w_branch_a, w_branch_b, w_out,
                    norm2_g, router_group_w, router_group_b, router_expert_w, router_expert_b, w_gate, w_up, w_down):
    assert norm1_g.shape[0] == 1
    l = 0
    pad_r = LANES - N_GROUPS - N_EXPERTS
    zrow = jnp.concatenate([jnp.ones((1, HEAD_DIM), F32), jnp.zeros((3, HEAD_DIM), F32)], axis=0)
    col = lambda start, width: w_in[l][:, start:start + width]
    qb0, kb0, vb0 = 3 * W_NA, 3 * W_NA + W_DIL, 3 * W_NA + 2 * W_DIL
    pieces = [col(0, 3 * W_NA)]
    for g in range(len(DIL_CONFIGS)):
        pieces += [col(qb0 + g * W_GRP, W_GRP), col(kb0 + g * W_GRP, W_GRP), col(vb0 + g * W_GRP, W_GRP)]
    pieces.append(col(3 * W_NA + 3 * W_DIL, 2 * D_MODEL))
    return {
        "g1": norm1_g[l][None, :],
        "w_in": jnp.concatenate(pieces, axis=1).astype(BF16),
        "gains": jnp.concatenate([qn_a[l][None], kn_a[l][None], qn_b[l][None], kn_b[l][None], zrow], axis=0),
        "na_bias": _na_bias_table(rpb_a[l]),
        "wa": w_branch_a[l].astype(BF16),
        "wb": w_branch_b[l].astype(BF16),
        "wo": w_out[l].astype(BF16),
        "g2": norm2_g[l][None, :],
        "wr": jnp.concatenate([router_group_w[l], router_expert_w[l], jnp.zeros((D_MODEL, pad_r), F32)],
                              axis=1).astype(BF16),
        "rbias": jnp.concatenate([router_group_b[l], router_expert_b[l], jnp.zeros((pad_r,), F32)])[None, :],
        "wg": w_gate[l],
        "wu": w_up[l],
        "wd": w_down[l],
    }


def kernel(x_prompt, x_sample, norm1_g, w_in, qn_a, kn_a, rpb_a, qn_b, kn_b, w_branch_a, w_branch_b, w_out,
           norm2_g, router_group_w, router_group_b, router_expert_w, router_expert_b, w_gate, w_up, w_down):
    p = _prepare_params(norm1_g, w_in, qn_a, kn_a, rpb_a, qn_b, kn_b, w_branch_a, w_branch_b, w_out, norm2_g,
                        router_group_w, router_group_b, router_expert_w, router_expert_b, w_gate, w_up, w_down)
    return _layer((x_prompt, x_sample), p)
```
